```python
import jax, jax.numpy as jnp
from jax import lax
import numpy as np

D_MODEL = 1024
BATCH = 8
SEQ = 2048
DEPTH = 2
DEC_BATCH = 128
DEC_SEQ = 4
PAST_LEN = 16384
PAGE_SIZE = 128

CONV_WIDTH = 3
W_CONV = D_MODEL
D_SG = D_MODEL
SG_CHUNK = 128
SG_GROUPS = 4
N_MEM = 256
MEM_HEADS = 4
MEM_HEAD_DIM = D_MODEL // MEM_HEADS
D_ATT = MEM_HEADS * MEM_HEAD_DIM
N_BRANCH = 3
D_FF = 2816
D_IN = 3 * W_CONV + 2 * D_SG + D_ATT + N_BRANCH * D_MODEL
EPS = 1e-6

kernel_name = "gated_branch_shortconv_sgu_memattn_convffn_step"


def _rmsnorm(x, g):
    xf = x.astype(jnp.float32)
    y = xf * lax.rsqrt(jnp.mean(xf * xf, axis=-1, keepdims=True) + EPS)
    return (y * g.astype(jnp.float32)).astype(x.dtype)


def _layernorm(x, g):
    xf = x.astype(jnp.float32)
    mu = jnp.mean(xf, axis=-1, keepdims=True)
    xc = xf - mu
    y = xc * lax.rsqrt(jnp.mean(xc * xc, axis=-1, keepdims=True) + EPS)
    return (y * g.astype(jnp.float32)).astype(x.dtype)


def _causal_dwconv3(x, prev, w):
    xp = jnp.concatenate([prev.astype(x.dtype), x], axis=1)
    y = w[0] * xp[:, :-2] + w[1] * xp[:, 1:-1] + w[2] * xp[:, 2:]
    return y, xp[:, -(CONV_WIDTH - 1):]


def _spatial_gate(v, sg_w, sg_b):
    bsz, t, _ = v.shape
    L = min(t, SG_CHUNK)
    n_chunks = t // L
    mask = jnp.tril(jnp.ones((L, L), dtype=bool))
    w = jnp.where(mask[None], sg_w[:, :L, :L], jnp.zeros((), sg_w.dtype))
    vr = v.reshape(bsz, n_chunks, L, SG_GROUPS, D_SG // SG_GROUPS)
    s = jnp.einsum('gts,bcsgd->bctgd', w, vr) + sg_b[:, :L].T[None, None, :, :, None]
    return s.reshape(bsz, t, D_SG)


def _mem_attention(q, mem_k, mem_v):
    bsz, t, _ = q.shape
    qh = q.reshape(bsz, t, MEM_HEADS, MEM_HEAD_DIM)
    s = jnp.einsum('bthd,bmhd->bhtm', qh.astype(jnp.float32), mem_k.astype(jnp.float32))
    p = jax.nn.softmax(s * (MEM_HEAD_DIM ** -0.5), axis=-1).astype(q.dtype)
    o = jnp.einsum('bhtm,bmhd->bthd', p, mem_v.astype(q.dtype))
    return o.reshape(bsz, t, D_ATT)


def _mem_kv(mem, norm_mem_g, w_k, w_v):
    m = _rmsnorm(mem, norm_mem_g)
    bsz = mem.shape[0]
    k = (m @ w_k).reshape(bsz, N_MEM, MEM_HEADS, MEM_HEAD_DIM)
    v = (m @ w_v).reshape(bsz, N_MEM, MEM_HEADS, MEM_HEAD_DIM)
    return k, v


def _layer(x, conv_a_prev, conv_f_prev, mem_k, mem_v, norm_mix_g, w_in, conv_a_w, sg_ln_g,
           sg_w, sg_b, w_o, norm_ffn_g, w_up, conv_f_w, conv_f_b, w_down):
    z = _rmsnorm(x, norm_mix_g)
    p = z @ w_in
    cuts = np.cumsum([W_CONV, W_CONV, W_CONV, D_SG, D_SG, D_ATT]).tolist()
    a_h, a_c, a_b, u, v, q, g = jnp.split(p, cuts, axis=-1)
    conv_out, conv_a_state = _causal_dwconv3(a_c * a_h, conv_a_prev, conv_a_w)
    y_a = a_b * conv_out
    v_n = _layernorm(v, sg_ln_g)
    y_b = u * _spatial_gate(v_n, sg_w, sg_b)
    y_m = _mem_attention(q, mem_k, mem_v)
    g = jax.nn.sigmoid(g.astype(jnp.float32)).astype(x.dtype)
    g_a, g_b, g_m = jnp.split(g, N_BRANCH, axis=-1)
    x = x + (g_a * y_a + g_b * y_b + g_m * y_m) @ w_o
    h = _rmsnorm(x, norm_ffn_g) @ w_up
    hc, conv_f_state = _causal_dwconv3(h, conv_f_prev, conv_f_w)
    hc = hc + conv_f_b
    a, gt = jnp.split(hc, 2, axis=-1)
    x = x + (jax.nn.silu(gt) * a) @ w_down
    return x, conv_a_state, conv_f_state, v_n


def setup_inputs(seed: int = 0) -> dict:
    key = jax.random.key(seed)
    ks = jax.random.split(key, 24)
    f32 = jnp.float32

    def nrm(k, shape, scale=1.0):
        return jax.random.normal(k, shape, f32) * scale

    def gain(k, shape):
        return 1.0 + 0.01 * jax.random.normal(k, shape, f32)

    return {
        "x_prompt": nrm(ks[0], (BATCH, SEQ, D_MODEL)),
        "x_sample": nrm(ks[1], (DEC_BATCH, DEC_SEQ, D_MODEL)),
        "mem_prompt": nrm(ks[2], (BATCH, N_MEM, D_MODEL)),
        "cache_conv_a": nrm(ks[3], (DEPTH, DEC_BATCH, CONV_WIDTH - 1, W_CONV)),
        "cache_conv_ffn": nrm(ks[4], (DEPTH, DEC_BATCH, CONV_WIDTH - 1, 2 * D_FF)),
        "cache_mem_k": nrm(ks[5], (DEPTH, DEC_BATCH, N_MEM, MEM_HEADS, MEM_HEAD_DIM)),
        "cache_mem_v": nrm(ks[6], (DEPTH, DEC_BATCH, N_MEM, MEM_HEADS, MEM_HEAD_DIM)),
        "norm_mix_g": gain(ks[7], (DEPTH, D_MODEL)),
        "w_in": nrm(ks[8], (DEPTH, D_MODEL, D_IN), D_MODEL ** -0.5),
        "conv_a_w": nrm(ks[9], (DEPTH, CONV_WIDTH, W_CONV), CONV_WIDTH ** -0.5),
        "sg_ln_g": gain(ks[10], (DEPTH, D_SG)),
        "sg_w": nrm(ks[11], (DEPTH, SG_GROUPS, SG_CHUNK, SG_CHUNK), SG_CHUNK ** -0.5),
        "sg_b": gain(ks[12], (DEPTH, SG_GROUPS, SG_CHUNK)),
        "norm_mem_g": gain(ks[13], (DEPTH, D_MODEL)),
        "w_k": nrm(ks[14], (DEPTH, D_MODEL, D_ATT), D_MODEL ** -0.5),
        "w_v": nrm(ks[15], (DEPTH, D_MODEL, D_ATT), D_MODEL ** -0.5),
        "w_o": nrm(ks[16], (DEPTH, D_MODEL, D_MODEL), D_MODEL ** -0.5),
        "norm_ffn_g": gain(ks[17], (DEPTH, D_MODEL)),
        "w_up": nrm(ks[18], (DEPTH, D_MODEL, 2 * D_FF), D_MODEL ** -0.5),
        "conv_f_w": nrm(ks[19], (DEPTH, CONV_WIDTH, 2 * D_FF), CONV_WIDTH ** -0.5),
        "conv_f_b": nrm(ks[20], (DEPTH, 2 * D_FF), 0.01),
        "w_down": nrm(ks[21], (DEPTH, D_FF, D_MODEL), D_FF ** -0.5),
        "norm_final_g": gain(ks[22], (D_MODEL,)),
    }


def reference(x_prompt, x_sample, mem_prompt, cache_conv_a, cache_conv_ffn, cache_mem_k,
              cache_mem_v, norm_mix_g, w_in, conv_a_w, sg_ln_g, sg_w, sg_b, norm_mem_g, w_k,
              w_v, w_o, norm_ffn_g, w_up, conv_f_w, conv_f_b, w_down, norm_final_g):
    xp, xs = x_prompt, x_sample
    bp = x_prompt.shape[0]
    pa_list, pf_list, pk_list, pv_list = [], [], [], []
    sa_list, sf_list, sv_list = [], [], []
    for l in range(DEPTH):
        shared = (norm_mix_g[l], w_in[l], conv_a_w[l], sg_ln_g[l], sg_w[l], sg_b[l], w_o[l],
                  norm_ffn_g[l], w_up[l], conv_f_w[l], conv_f_b[l], w_down[l])
        mk, mv = _mem_kv(mem_prompt, norm_mem_g[l], w_k[l], w_v[l])
        zero_a = jnp.zeros((bp, CONV_WIDTH - 1, W_CONV), xp.dtype)
        zero_f = jnp.zeros((bp, CONV_WIDTH - 1, 2 * D_FF), xp.dtype)
        xp, pa, pf, _ = _layer(xp, zero_a, zero_f, mk, mv, *shared)
        pa_list.append(pa)
        pf_list.append(pf)
        pk_list.append(mk)
        pv_list.append(mv)
        xs, sa, sf, sv = _layer(xs, cache_conv_a[l], cache_conv_ffn[l], cache_mem_k[l],
                                cache_mem_v[l], *shared)
        sa_list.append(sa)
        sf_list.append(sf)
        sv_list.append(sv)
    y_prompt = _rmsnorm(xp, norm_final_g)
    y_sample = _rmsnorm(xs, norm_final_g)
    return (y_prompt, y_sample,
            jnp.stack(pa_list), jnp.stack(pf_list), jnp.stack(pk_list), jnp.stack(pv_list),
            jnp.stack(sa_list), jnp.stack(sf_list), jnp.stack(sv_list))
```

```python
import functools

import jax
import jax.numpy as jnp
from jax import lax
from jax.experimental import pallas as pl
from jax.experimental.pallas import tpu as pltpu

F32 = jnp.float32
BF16 = jnp.bfloat16

D = 1024
D_IN = 9 * D
D_FF = 2816
D_FF2 = 2 * D_FF
N_MEM = 256
HEADS = 4
HEAD_DIM = D // HEADS
SG_GROUPS = 4
SG_COLS = D // SG_GROUPS
SG_CHUNK = 128
EPS = 1e-6

V7X_VMEM_BYTES = 64 * 1024 * 1024
VMEM_LIMIT = V7X_VMEM_BYTES - 6 * 1024 * 1024
SUBLANES = 8

TM_MIX = 256
TM_FFN = 256
TM_KV = 512
ATT_BB = 8
Q_PAD = SUBLANES


def _rms(x, g):
    y = x * lax.rsqrt(jnp.mean(x * x, axis=-1, keepdims=True) + EPS)
    return y * g


def _dot(a, b):
    return jnp.dot(a, b, preferred_element_type=F32)


def _resident(shape, index_map):
    return pl.BlockSpec(shape, index_map, pipeline_mode=pl.Buffered(1))


def _causal_conv(scr, w_ref, cur, rows, prev_rows, stride):
    scr[prev_rows:prev_rows + rows, :] = cur
    x2 = scr[prev_rows - 2 * stride:prev_rows - 2 * stride + rows, :]
    x1 = scr[prev_rows - stride:prev_rows - stride + rows, :]
    return w_ref[0:1, :] * x2 + w_ref[1:2, :] * x1 + w_ref[2:3, :] * cur


def _kv_kernel(mem_ref, g_ref, wkv_ref, k_ref, v_ref, kb_ref, vb_ref):
    m = _rms(mem_ref[...], g_ref[...]).astype(BF16)
    k = _dot(m, wkv_ref[:, :D])
    v = _dot(m, wkv_ref[:, D:])
    k_ref[...] = k
    v_ref[...] = v
    kb_ref[...] = k.astype(BF16)
    vb_ref[...] = v.astype(BF16)


def _mem_kv(mem2d, norm_mem_g, wkv):
    depth = wkv.shape[0]
    rows = mem2d.shape[0]
    out_f = jax.ShapeDtypeStruct((depth, rows, D), F32)
    out_b = jax.ShapeDtypeStruct((depth, rows, D), BF16)
    oblk = pl.BlockSpec((None, TM_KV, D), lambda l, i: (l, i, 0))
    return pl.pallas_call(
        _kv_kernel,
        grid=(depth, rows // TM_KV),
        in_specs=[
            pl.BlockSpec((TM_KV, D), lambda l, i: (i, 0)),
            pl.BlockSpec((None, 1, D), lambda l, i: (l, 0, 0)),
            pl.BlockSpec((None, D, 2 * D), lambda l, i: (l, 0, 0)),
        ],
        out_specs=[oblk, oblk, oblk, oblk],
        out_shape=[out_f, out_f, out_b, out_b],
        compiler_params=pltpu.CompilerParams(
            dimension_semantics=("arbitrary", "arbitrary"), vmem_limit_bytes=VMEM_LIMIT),
        name="mem_kv",
    )(mem2d, norm_mem_g, wkv)


def _mixer_kernel(*refs, sample, rows, prev_rows, stride, chunk):
    if sample:
        (x_ref, nm_ref, win_ref, caw_ref, lng_ref, sgw_ref, sgb_ref, prev_ref,
         mixab_ref, gm_ref, q_ref, vn_ref, state_ref, a_scr) = refs
    else:
        (x_ref, nm_ref, win_ref, caw_ref, lng_ref, sgw_ref, sgb_ref, kb_ref, vb_ref,
         mix_ref, state_ref, a_scr) = refs

    z = _rms(x_ref[...], nm_ref[...]).astype(BF16)

    def proj(i):
        return _dot(z, win_ref[:, i * D:(i + 1) * D])

    if sample:
        a_scr[0:prev_rows, :] = prev_ref[...]
    else:
        @pl.when(pl.program_id(1) == 0)
        def _():
            a_scr[0:prev_rows, :] = jnp.zeros((prev_rows, D), F32)

    a = proj(1) * proj(0)
    conv = _causal_conv(a_scr, caw_ref, a, rows, prev_rows, stride)
    acc = jax.nn.sigmoid(proj(6)) * (proj(2) * conv)
    tail = a_scr[rows:rows + prev_rows, :]
    if sample:
        state_ref[...] = tail
    else:
        a_scr[0:prev_rows, :] = tail

        @pl.when(pl.program_id(1) == pl.num_programs(1) - 1)
        def _():
            state_ref[...] = tail[prev_rows - 2:, :]

    v = proj(4)
    vc = v - jnp.mean(v, axis=-1, keepdims=True)
    vn = vc * lax.rsqrt(jnp.mean(vc * vc, axis=-1, keepdims=True) + EPS) * lng_ref[...]
    if sample:
        vn_ref[...] = vn
    vnb = vn.astype(BF16)
    ri = lax.broadcasted_iota(jnp.int32, (chunk, chunk), 0)
    ci = lax.broadcasted_iota(jnp.int32, (chunk, chunk), 1)
    if sample:
        shift = stride.bit_length() - 1
        keep = ((ri & (stride - 1)) == (ci & (stride - 1))) & ((ci >> shift) <= (ri >> shift))
    else:
        keep = ci <= ri
    s_rows = []
    for c in range(rows // chunk):
        s_cols = []
        for g in range(SG_GROUPS):
            w = jnp.where(keep, sgw_ref[g], 0.0).astype(BF16)
            vg = vnb[c * chunk:(c + 1) * chunk, g * SG_COLS:(g + 1) * SG_COLS]
            s_cols.append(_dot(w, vg) + sgb_ref[:, g:g + 1])
        s_rows.append(jnp.concatenate(s_cols, axis=1))
    s = s_rows[0] if len(s_rows) == 1 else jnp.concatenate(s_rows, axis=0)
    acc = acc + jax.nn.sigmoid(proj(7)) * (proj(3) * s)

    q = proj(5)
    gm = jax.nn.sigmoid(proj(8))
    if sample:
        mixab_ref[...] = acc
        gm_ref[...] = gm
        q_ref[...] = q
        return
    o_cols = []
    for h in range(HEADS):
        hc = slice(h * HEAD_DIM, (h + 1) * HEAD_DIM)
        sc = lax.dot_general(q[:, hc].astype(BF16), kb_ref[:, hc], (((1,), (1,)), ((), ())),
                             preferred_element_type=F32) * (HEAD_DIM ** -0.5)
        e = jnp.exp(sc - jnp.max(sc, axis=-1, keepdims=True))
        p = e * (1.0 / jnp.sum(e, axis=-1, keepdims=True))
        o_cols.append(_dot(p.astype(BF16), vb_ref[:, hc]))
    acc = acc + gm * jnp.concatenate(o_cols, axis=1)
    mix_ref[...] = acc.astype(BF16)


def _mixer_prompt(l, x, norm_mix_g, w_in, conv_a_w, sg_ln_g, sg_w, sg_b_t, kb, vb):
    bsz, t, _ = x.shape
    tm = TM_MIX
    prev_rows = SUBLANES
    wspec = lambda shape: _resident((None,) + shape, lambda b, i: (l,) + (0,) * len(shape))
    return pl.pallas_call(
        functools.partial(_mixer_kernel, sample=False, rows=tm, prev_rows=prev_rows, stride=1,
                          chunk=SG_CHUNK),
        grid=(bsz, t // tm),
        in_specs=[
            pl.BlockSpec((None, tm, D), lambda b, i: (b, i, 0)),
            wspec((1, D)),
            wspec((D, D_IN)),
            wspec((3, D)),
            wspec((1, D)),
            wspec((SG_GROUPS, SG_CHUNK, SG_CHUNK)),
            wspec((SG_CHUNK, SG_GROUPS)),
            pl.BlockSpec((None, N_MEM, D), lambda b, i: (l * bsz + b, 0, 0)),
            pl.BlockSpec((None, N_MEM, D), lambda b, i: (l * bsz + b, 0, 0)),
        ],
        out_specs=[
            pl.BlockSpec((None, tm, D), lambda b, i: (b, i, 0)),
            pl.BlockSpec((None, 2, D), lambda b, i: (b, 0, 0)),
        ],
        out_shape=[
            jax.ShapeDtypeStruct((bsz, t, D), BF16),
            jax.ShapeDtypeStruct((bsz, 2, D), F32),
        ],
        scratch_shapes=[pltpu.VMEM((tm + prev_rows, D), F32)],
        compiler_params=pltpu.CompilerParams(
            dimension_semantics=("arbitrary", "arbitrary"), vmem_limit_bytes=VMEM_LIMIT),
        name="mixer_prompt",
    )(x, norm_mix_g, w_in, conv_a_w, sg_ln_g, sg_w, sg_b_t, kb, vb)


def _mixer_sample(l, x_tm, norm_mix_g, w_in, conv_a_w, sg_ln_g, sg_w_tiled, sg_b_rows, prev_tm):
    rows = x_tm.shape[0]
    stride = rows // 4
    prev_rows = 2 * stride
    assert stride & (stride - 1) == 0
    wspec = lambda shape: _resident((None,) + shape, lambda i: (l,) + (0,) * len(shape))
    full = lambda r: pl.BlockSpec((r, D), lambda i: (0, 0))
    out_f = jax.ShapeDtypeStruct((rows, D), F32)
    return pl.pallas_call(
        functools.partial(_mixer_kernel, sample=True, rows=rows, prev_rows=prev_rows,
                          stride=stride, chunk=rows),
        grid=(1,),
        in_specs=[
            full(rows),
            wspec((1, D)),
            wspec((D, D_IN)),
            wspec((3, D)),
            wspec((1, D)),
            wspec((SG_GROUPS, rows, rows)),
            wspec((rows, SG_GROUPS)),
            wspec((prev_rows, D)),
        ],
        out_specs=[full(rows), full(rows), full(rows), full(rows), full(prev_rows)],
        out_shape=[out_f, out_f, out_f, out_f, jax.ShapeDtypeStruct((prev_rows, D), F32)],
        scratch_shapes=[pltpu.VMEM((rows + prev_rows, D), F32)],
        compiler_params=pltpu.CompilerParams(
            dimension_semantics=("arbitrary",), vmem_limit_bytes=VMEM_LIMIT),
        name="mixer_sample",
    )(x_tm, norm_mix_g, w_in, conv_a_w, sg_ln_g, sg_w_tiled, sg_b_rows, prev_tm)


def _sample_attn_kernel(q_ref, gm_ref, mixab_ref, k_ref, v_ref, mix_ref):
    def one(b, carry):
        q = q_ref[b]
        o_cols = []
        for h in range(HEADS):
            hc = slice(h * HEAD_DIM, (h + 1) * HEAD_DIM)
            kh = k_ref[b, :, hc].astype(BF16)
            vh = v_ref[b, :, hc].astype(BF16)
            sc = lax.dot_general(q[:, hc].astype(BF16), kh, (((1,), (1,)), ((), ())),
                                 preferred_element_type=F32) * (HEAD_DIM ** -0.5)
            e = jnp.exp(sc - jnp.max(sc, axis=-1, keepdims=True))
            p = e * (1.0 / jnp.sum(e, axis=-1, keepdims=True))
            o_cols.append(_dot(p.astype(BF16), vh))
        y = jnp.concatenate(o_cols, axis=1)
        mix_ref[b] = mixab_ref[b] + gm_ref[b] * y
        return carry

    lax.fori_loop(0, ATT_BB, one, 0)


def _sample_attn(l, q, gm, mixab, cache_k, cache_v):
    bsz = q.shape[0]
    small = pl.BlockSpec((ATT_BB, Q_PAD, D), lambda i: (i, 0, 0))
    big = pl.BlockSpec((None, ATT_BB, N_MEM, D), lambda i: (l, i, 0, 0))
    return pl.pallas_call(
        _sample_attn_kernel,
        grid=(bsz // ATT_BB,),
        in_specs=[small, small, small, big, big],
        out_specs=small,
        out_shape=jax.ShapeDtypeStruct((bsz, Q_PAD, D), F32),
        compiler_params=pltpu.CompilerParams(
            dimension_semantics=("arbitrary",), vmem_limit_bytes=VMEM_LIMIT),
        name="sample_attn",
    )(q, gm, mixab, cache_k, cache_v)


def _ffn_kernel(*refs, sample, final, rows, prev_rows, stride):
    refs = list(refs)
    x_ref, mix_ref, wo_ref, g_ref, wup_ref, cw_ref, cb_ref, wdn_ref = refs[:8]
    refs = refs[8:]
    prev_ref = refs.pop(0) if sample else None
    gfin_ref = refs.pop(0) if final else None
    xo_ref, state_ref = refs[:2]
    refs = refs[2:]
    y_ref = refs.pop(0) if final else None
    h_scr, = refs

    step = pl.program_id(0) if sample else pl.program_id(1)
    last = (pl.num_programs(0) if sample else pl.num_programs(1)) - 1

    @pl.when(step == 0)
    def _():
        if sample:
            h_scr[0:prev_rows, :] = prev_ref[...]
        else:
            h_scr[0:prev_rows, :] = jnp.zeros((prev_rows, D_FF2), F32)

    x1 = x_ref[...] + _dot(mix_ref[...].astype(BF16), wo_ref[...])
    z = _rms(x1, g_ref[...]).astype(BF16)
    h = _dot(z, wup_ref[...])
    hc = _causal_conv(h_scr, cw_ref, h, rows, prev_rows, stride) + cb_ref[...]
    tail = h_scr[rows:rows + prev_rows, :]
    h_scr[0:prev_rows, :] = tail

    @pl.when(step == last)
    def _():
        state_ref[...] = tail if sample else tail[prev_rows - 2:, :]

    act = (jax.nn.silu(hc[:, D_FF:]) * hc[:, :D_FF]).astype(BF16)
    x2 = x1 + _dot(act, wdn_ref[...])
    xo_ref[...] = x2
    if final:
        y_ref[...] = _rms(x2, gfin_ref[...])


def _ffn(l, final, sample, x, mix, w_o, norm_ffn_g, w_up, conv_f_w, conv_f_b, w_down,
         norm_final_g, prev_tm=None):
    if sample:
        rows_total = x.shape[0]
        stride = rows_total // 4
        tm, prev_rows = stride, 2 * stride
        grid = (4,)
        tok = pl.BlockSpec((tm, D), lambda i: (i, 0))
        wmap = lambda n: (lambda i: (l,) + (0,) * n)
        state_spec = pl.BlockSpec((prev_rows, D_FF2), lambda i: (0, 0))
        state_shape = jax.ShapeDtypeStruct((prev_rows, D_FF2), F32)
        x_shape = jax.ShapeDtypeStruct((rows_total, D), F32)
        gfin_spec = pl.BlockSpec((1, D), lambda i: (0, 0))
        sem = ("arbitrary",)
    else:
        bsz, t, _ = x.shape
        stride, tm, prev_rows = 1, TM_FFN, SUBLANES
        grid = (bsz, t // tm)
        tok = pl.BlockSpec((None, tm, D), lambda b, i: (b, i, 0))
        wmap = lambda n: (lambda b, i: (l,) + (0,) * n)
        state_spec = pl.BlockSpec((None, 2, D_FF2), lambda b, i: (b, 0, 0))
        state_shape = jax.ShapeDtypeStruct((bsz, 2, D_FF2), F32)
        x_shape = jax.ShapeDtypeStruct((bsz, t, D), F32)
        gfin_spec = pl.BlockSpec((1, D), lambda b, i: (0, 0))
        sem = ("arbitrary", "arbitrary")
    wspec = lambda shape: _resident((None,) + shape, wmap(len(shape)))
    in_specs = [tok, tok, wspec((D, D)), wspec((1, D)), wspec((D, D_FF2)), wspec((3, D_FF2)),
                wspec((1, D_FF2)), wspec((D_FF, D))]
    args = [x, mix, w_o, norm_ffn_g, w_up, conv_f_w, conv_f_b, w_down]
    if sample:
        in_specs.append(wspec((prev_rows, D_FF2)))
        args.append(prev_tm)
    out_specs = [tok, state_spec]
    out_shape = [x_shape, state_shape]
    if final:
        in_specs.append(gfin_spec)
        args.append(norm_final_g)
        out_specs.append(tok)
        out_shape.append(x_shape)
    return pl.pallas_call(
        functools.partial(_ffn_kernel, sample=sample, final=final, rows=tm,
                          prev_rows=prev_rows, stride=stride),
        grid=grid,
        in_specs=in_specs,
        out_specs=out_specs,
        out_shape=out_shape,
        scratch_shapes=[pltpu.VMEM((tm + prev_rows, D_FF2), F32)],
        compiler_params=pltpu.CompilerParams(dimension_semantics=sem,
                                             vmem_limit_bytes=VMEM_LIMIT),
        name="ffn_sample" if sample else "ffn_prompt",
    )(*args)


def _to_token_major(a):
    return jnp.swapaxes(a, 0, 1).reshape(a.shape[0] * a.shape[1], a.shape[2])


def _to_batch_major(a, t):
    return jnp.swapaxes(a.reshape(t, a.shape[0] // t, a.shape[1]), 0, 1)


def kernel(x_prompt, x_sample, mem_prompt, cache_conv_a, cache_conv_ffn, cache_mem_k, cache_mem_v, norm_mix_g, w_in, conv_a_w, sg_ln_g, sg_w, sg_b, norm_mem_g, w_k, w_v, w_o, norm_ffn_g, w_up, conv_f_w, conv_f_b, w_down, norm_final_g):
    depth = w_in.shape[0]
    bp, seq, _ = x_prompt.shape
    bs, ts, _ = x_sample.shape
    rows_s = bs * ts

    w_in_b = w_in.astype(BF16)
    w_o_b = w_o.astype(BF16)
    w_up_b = w_up.astype(BF16)
    w_down_b = w_down.astype(BF16)
    wkv_b = jnp.concatenate([w_k, w_v], axis=-1).astype(BF16)

    nmg = norm_mix_g[:, None, :]
    lng = sg_ln_g[:, None, :]
    nfg = norm_ffn_g[:, None, :]
    nmemg = norm_mem_g[:, None, :]
    cfb = conv_f_b[:, None, :]
    gfin = norm_final_g[None, :]
    sg_b_t = jnp.swapaxes(sg_b, 1, 2)
    sg_w_s = jnp.repeat(jnp.repeat(sg_w[:, :, :ts, :ts], bs, axis=2), bs, axis=3)
    sg_b_s = jnp.repeat(sg_b_t[:, :ts, :], bs, axis=1)
    prev_a_s = jnp.swapaxes(cache_conv_a, 1, 2).reshape(depth, 2 * bs, D)
    prev_f_s = jnp.swapaxes(cache_conv_ffn, 1, 2).reshape(depth, 2 * bs, D_FF2)
    cache_k = cache_mem_k.reshape(depth, bs, N_MEM, D)
    cache_v = cache_mem_v.reshape(depth, bs, N_MEM, D)

    k_f, v_f, k_b, v_b = _mem_kv(mem_prompt.reshape(bp * N_MEM, D), nmemg, wkv_b)
    k_b = k_b.reshape(depth * bp, N_MEM, D)
    v_b = v_b.reshape(depth * bp, N_MEM, D)

    xp = x_prompt
    xs = _to_token_major(x_sample)
    pa, pf, sa, sf, sv = [], [], [], [], []
    yp = ys = None
    for l in range(depth):
        final = l == depth - 1
        mix, a_state = _mixer_prompt(l, xp, nmg, w_in_b, conv_a_w, lng, sg_w, sg_b_t, k_b, v_b)
        outs = _ffn(l, final, False, xp, mix, w_o_b, nfg, w_up_b, conv_f_w, cfb, w_down_b, gfin)
        xp, f_state = outs[0], outs[1]
        if final:
            yp = outs[2]
        pa.append(a_state)
        pf.append(f_state)
        mixab, gm, q, vn, a_state_s = _mixer_sample(l, xs, nmg, w_in_b, conv_a_w, lng, sg_w_s,
                                                    sg_b_s, prev_a_s)
        pad = lambda a: jnp.pad(_to_batch_major(a, ts), ((0, 0), (0, Q_PAD - ts), (0, 0)))
        mix_s = _sample_attn(l, pad(q), pad(gm), pad(mixab), cache_k, cache_v)
        mix_s = _to_token_major(mix_s[:, :ts, :])
        outs = _ffn(l, final, True, xs, mix_s, w_o_b, nfg, w_up_b, conv_f_w, cfb, w_down_b, gfin,
                    prev_tm=prev_f_s)
        xs, f_state_s = outs[0], outs[1]
        if final:
            ys = outs[2]
        sa.append(_to_batch_major(a_state_s, 2))
        sf.append(_to_batch_major(f_state_s, 2))
        sv.append(_to_batch_major(vn, ts))

    kv_shape = (depth, bp, N_MEM, HEADS, HEAD_DIM)
    return (yp, _to_batch_major(ys, ts),
            jnp.stack(pa), jnp.stack(pf), k_f.reshape(kv_shape), v_f.reshape(kv_shape),
            jnp.stack(sa), jnp.stack(sf), jnp.stack(sv))
```

```python
import functools

import jax
import jax.numpy as jnp
from jax import lax
from jax.experimental import pallas as pl
from jax.experimental.pallas import tpu as pltpu

F32 = jnp.float32
BF16 = jnp.bfloat16

D = 1024
D_IN = 9 * D
D_FF = 2816
D_FF2 = 2 * D_FF
N_MEM = 256
HEADS = 4
HEAD_DIM = D // HEADS
SG_GROUPS = 4
SG_COLS = D // SG_GROUPS
SG_CHUNK = 128
EPS = 1e-6

V7X_VMEM_BYTES = 64 * 1024 * 1024
VMEM_LIMIT = V7X_VMEM_BYTES - 6 * 1024 * 1024
SUBLANES = 8

TM_MIX = 256
TM_FFN = 256
TM_KV = 512
ATT_BB = 4
Q_PAD = SUBLANES


def _rms(x, g):
    y = x * lax.rsqrt(jnp.mean(x * x, axis=-1, keepdims=True) + EPS)
    return y * g


def _dot(a, b):
    return jnp.dot(a, b, preferred_element_type=F32)


def _resident(shape, index_map):
    return pl.BlockSpec(shape, index_map, pipeline_mode=pl.Buffered(1))


def _causal_conv(scr, w_ref, cur, rows, prev_rows, stride):
    scr[prev_rows:prev_rows + rows, :] = cur
    x2 = scr[prev_rows - 2 * stride:prev_rows - 2 * stride + rows, :]
    x1 = scr[prev_rows - stride:prev_rows - stride + rows, :]
    return w_ref[0:1, :] * x2 + w_ref[1:2, :] * x1 + w_ref[2:3, :] * cur


def _kv_kernel(mem_ref, g_ref, wkv_ref, k_ref, v_ref, kb_ref, vb_ref):
    m = _rms(mem_ref[...], g_ref[...]).astype(BF16)
    k = _dot(m, wkv_ref[:, :D])
    v = _dot(m, wkv_ref[:, D:])
    for h in range(HEADS):
        hc = slice(h * HEAD_DIM, (h + 1) * HEAD_DIM)
        k_ref[:, h, :] = k[:, hc]
        v_ref[:, h, :] = v[:, hc]
    kb_ref[...] = k.astype(BF16)
    vb_ref[...] = v.astype(BF16)


def _mem_kv(mem2d, norm_mem_g, wkv):
    depth = wkv.shape[0]
    rows = mem2d.shape[0]
    out_f = jax.ShapeDtypeStruct((depth, rows, HEADS, HEAD_DIM), F32)
    out_b = jax.ShapeDtypeStruct((depth, rows, D), BF16)
    oblk = pl.BlockSpec((None, TM_KV, D), lambda l, i: (l, i, 0))
    oblk_f = pl.BlockSpec((None, TM_KV, HEADS, HEAD_DIM), lambda l, i: (l, i, 0, 0))
    return pl.pallas_call(
        _kv_kernel,
        grid=(depth, rows // TM_KV),
        in_specs=[
            pl.BlockSpec((TM_KV, D), lambda l, i: (i, 0)),
            pl.BlockSpec((None, 1, D), lambda l, i: (l, 0, 0)),
            pl.BlockSpec((None, D, 2 * D), lambda l, i: (l, 0, 0)),
        ],
        out_specs=[oblk_f, oblk_f, oblk, oblk],
        out_shape=[out_f, out_f, out_b, out_b],
        compiler_params=pltpu.CompilerParams(
            dimension_semantics=("arbitrary", "arbitrary"), vmem_limit_bytes=VMEM_LIMIT),
        name="mem_kv",
    )(mem2d, norm_mem_g, wkv)


def _mixer_kernel(*refs, sample, rows, prev_rows, stride, chunk):
    if sample:
        (x_ref, nm_ref, win_ref, caw_ref, lng_ref, sgw_ref, sgb_ref, prev_ref,
         mixab_ref, gm_ref, q_ref, vn_ref, state_ref, a_scr) = refs
    else:
        (x_ref, nm_ref, win_ref, caw_ref, lng_ref, sgw_ref, sgb_ref, kb_ref, vb_ref,
         mix_ref, state_ref, a_scr) = refs

    z = _rms(x_ref[...], nm_ref[...]).astype(BF16)

    def proj(i):
        return _dot(z, win_ref[:, i * D:(i + 1) * D])

    if sample:
        a_scr[0:prev_rows, :] = prev_ref[...]
    else:
        @pl.when(pl.program_id(1) == 0)
        def _():
            a_scr[0:prev_rows, :] = jnp.zeros((prev_rows, D), F32)

    a = proj(1) * proj(0)
    conv = _causal_conv(a_scr, caw_ref, a, rows, prev_rows, stride)
    acc = jax.nn.sigmoid(proj(6)) * (proj(2) * conv)
    tail = a_scr[rows:rows + prev_rows, :]
    if sample:
        state_ref[...] = tail
    else:
        a_scr[0:prev_rows, :] = tail

        @pl.when(pl.program_id(1) == pl.num_programs(1) - 1)
        def _():
            state_ref[...] = tail[prev_rows - 2:, :]

    v = proj(4)
    vc = v - jnp.mean(v, axis=-1, keepdims=True)
    vn = vc * lax.rsqrt(jnp.mean(vc * vc, axis=-1, keepdims=True) + EPS) * lng_ref[...]
    if sample:
        vn_ref[...] = vn
    vnb = vn.astype(BF16)
    ri = lax.broadcasted_iota(jnp.int32, (chunk, chunk), 0)
    ci = lax.broadcasted_iota(jnp.int32, (chunk, chunk), 1)
    if sample:
        shift = stride.bit_length() - 1
        keep = ((ri & (stride - 1)) == (ci & (stride - 1))) & ((ci >> shift) <= (ri >> shift))
    else:
        keep = ci <= ri
    s_rows = []
    for c in range(rows // chunk):
        s_cols = []
        for g in range(SG_GROUPS):
            w = jnp.where(keep, sgw_ref[g], 0.0).astype(BF16)
            vg = vnb[c * chunk:(c + 1) * chunk, g * SG_COLS:(g + 1) * SG_COLS]
            s_cols.append(_dot(w, vg) + sgb_ref[:, g:g + 1])
        s_rows.append(jnp.concatenate(s_cols, axis=1))
    s = s_rows[0] if len(s_rows) == 1 else jnp.concatenate(s_rows, axis=0)
    acc = acc + jax.nn.sigmoid(proj(7)) * (proj(3) * s)

    q = proj(5)
    gm = jax.nn.sigmoid(proj(8))
    if sample:
        mixab_ref[...] = acc
        gm_ref[...] = gm
        q_ref[...] = q
        return
    o_cols = []
    for h in range(HEADS):
        hc = slice(h * HEAD_DIM, (h + 1) * HEAD_DIM)
        sc = lax.dot_general(q[:, hc].astype(BF16), kb_ref[:, hc], (((1,), (1,)), ((), ())),
                             preferred_element_type=F32) * (HEAD_DIM ** -0.5)
        e = jnp.exp(sc - jnp.max(sc, axis=-1, keepdims=True))
        p = e * (1.0 / jnp.sum(e, axis=-1, keepdims=True))
        o_cols.append(_dot(p.astype(BF16), vb_ref[:, hc]))
    acc = acc + gm * jnp.concatenate(o_cols, axis=1)
    mix_ref[...] = acc.astype(BF16)


def _mixer_prompt(l, x, norm_mix_g, w_in, conv_a_w, sg_ln_g, sg_w, sg_b_t, kb, vb):
    bsz, t, _ = x.shape
    tm = TM_MIX
    prev_rows = SUBLANES
    wspec = lambda shape: _resident((None,) + shape, lambda b, i: (l,) + (0,) * len(shape))
    return pl.pallas_call(
        functools.partial(_mixer_kernel, sample=False, rows=tm, prev_rows=prev_rows, stride=1,
                          chunk=SG_CHUNK),
        grid=(bsz, t // tm),
        in_specs=[
            pl.BlockSpec((None, tm, D), lambda b, i: (b, i, 0)),
            wspec((1, D)),
            wspec((D, D_IN)),
            wspec((3, D)),
            wspec((1, D)),
            wspec((SG_GROUPS, SG_CHUNK, SG_CHUNK)),
            wspec((SG_CHUNK, SG_GROUPS)),
            pl.BlockSpec((None, N_MEM, D), lambda b, i: (l * bsz + b, 0, 0)),
            pl.BlockSpec((None, N_MEM, D), lambda b, i: (l * bsz + b, 0, 0)),
        ],
        out_specs=[
            pl.BlockSpec((None, tm, D), lambda b, i: (b, i, 0)),
            pl.BlockSpec((None, 2, D), lambda b, i: (b, 0, 0)),
        ],
        out_shape=[
            jax.ShapeDtypeStruct((bsz, t, D), BF16),
            jax.ShapeDtypeStruct((bsz, 2, D), F32),
        ],
        scratch_shapes=[pltpu.VMEM((tm + prev_rows, D), F32)],
        compiler_params=pltpu.CompilerParams(
            dimension_semantics=("arbitrary", "arbitrary"), vmem_limit_bytes=VMEM_LIMIT),
        name="mixer_prompt",
    )(x, norm_mix_g, w_in, conv_a_w, sg_ln_g, sg_w, sg_b_t, kb, vb)


def _mixer_sample(l, x_tm, norm_mix_g, w_in, conv_a_w, sg_ln_g, sg_w_tiled, sg_b_rows, prev_tm):
    rows = x_tm.shape[0]
    stride = rows // 4
    prev_rows = 2 * stride
    assert stride & (stride - 1) == 0
    wspec = lambda shape: _resident((None,) + shape, lambda i: (l,) + (0,) * len(shape))
    full = lambda r: pl.BlockSpec((r, D), lambda i: (0, 0))
    out_f = jax.ShapeDtypeStruct((rows, D), F32)
    return pl.pallas_call(
        functools.partial(_mixer_kernel, sample=True, rows=rows, prev_rows=prev_rows,
                          stride=stride, chunk=rows),
        grid=(1,),
        in_specs=[
            full(rows),
            wspec((1, D)),
            wspec((D, D_IN)),
            wspec((3, D)),
            wspec((1, D)),
            wspec((SG_GROUPS, rows, rows)),
            wspec((rows, SG_GROUPS)),
            wspec((prev_rows, D)),
        ],
        out_specs=[full(rows), full(rows), full(rows), full(rows), full(prev_rows)],
        out_shape=[out_f, out_f, out_f, out_f, jax.ShapeDtypeStruct((prev_rows, D), F32)],
        scratch_shapes=[pltpu.VMEM((rows + prev_rows, D), F32)],
        compiler_params=pltpu.CompilerParams(
            dimension_semantics=("arbitrary",), vmem_limit_bytes=VMEM_LIMIT),
        name="mixer_sample",
    )(x_tm, norm_mix_g, w_in, conv_a_w, sg_ln_g, sg_w_tiled, sg_b_rows, prev_tm)


def _sample_attn_kernel(q_ref, gm_ref, mixab_ref, k_ref, v_ref, mix_ref):
    nq = HEADS * Q_PAD
    nk = N_MEM * HEADS
    row_head = lax.broadcasted_iota(jnp.int32, (nq, nk), 0) >> (Q_PAD.bit_length() - 1)
    col_head = lax.broadcasted_iota(jnp.int32, (nq, nk), 1) & (HEADS - 1)
    same_head = row_head == col_head
    for b in range(ATT_BB):
        q = q_ref[b]
        q32 = jnp.concatenate([q[:, h * HEAD_DIM:(h + 1) * HEAD_DIM] for h in range(HEADS)], axis=0)
        k2 = k_ref[b].reshape(nk, HEAD_DIM).astype(BF16)
        v2 = v_ref[b].reshape(nk, HEAD_DIM).astype(BF16)
        sc = lax.dot_general(q32.astype(BF16), k2, (((1,), (1,)), ((), ())),
                             preferred_element_type=F32) * (HEAD_DIM ** -0.5)
        sc = jnp.where(same_head, sc, -jnp.inf)
        e = jnp.exp(sc - jnp.max(sc, axis=-1, keepdims=True))
        p = e * (1.0 / jnp.sum(e, axis=-1, keepdims=True))
        o32 = _dot(p.astype(BF16), v2)
        y = jnp.concatenate([o32[h * Q_PAD:(h + 1) * Q_PAD] for h in range(HEADS)], axis=1)
        mix_ref[b] = mixab_ref[b] + gm_ref[b] * y


def _sample_attn(l, q, gm, mixab, cache_k, cache_v):
    bsz = q.shape[0]
    small = pl.BlockSpec((ATT_BB, Q_PAD, D), lambda i: (i, 0, 0))
    big = pl.BlockSpec((None, ATT_BB, N_MEM, HEADS, HEAD_DIM), lambda i: (l, i, 0, 0, 0))
    return pl.pallas_call(
        _sample_attn_kernel,
        grid=(bsz // ATT_BB,),
        in_specs=[small, small, small, big, big],
        out_specs=small,
        out_shape=jax.ShapeDtypeStruct((bsz, Q_PAD, D), F32),
        compiler_params=pltpu.CompilerParams(
            dimension_semantics=("arbitrary",), vmem_limit_bytes=VMEM_LIMIT),
        name="sample_attn",
    )(q, gm, mixab, cache_k, cache_v)


def _ffn_kernel(*refs, sample, final, rows, prev_rows, stride):
    refs = list(refs)
    x_ref, mix_ref, wo_ref, g_ref, wup_ref, cw_ref, cb_ref, wdn_ref = refs[:8]
    refs = refs[8:]
    prev_ref = refs.pop(0) if sample else None
    gfin_ref = refs.pop(0) if final else None
    xo_ref, state_ref = refs[:2]
    refs = refs[2:]
    y_ref = refs.pop(0) if final else None
    h_scr, = refs

    step = pl.program_id(0) if sample else pl.program_id(1)
    last = (pl.num_programs(0) if sample else pl.num_programs(1)) - 1

    @pl.when(step == 0)
    def _():
        if sample:
            h_scr[0:prev_rows, :] = prev_ref[...]
        else:
            h_scr[0:prev_rows, :] = jnp.zeros((prev_rows, D_FF2), F32)

    x1 = x_ref[...] + _dot(mix_ref[...].astype(BF16), wo_ref[...])
    z = _rms(x1, g_ref[...]).astype(BF16)
    h = _dot(z, wup_ref[...])
    hc = _causal_conv(h_scr, cw_ref, h, rows, prev_rows, stride) + cb_ref[...]
    tail = h_scr[rows:rows + prev_rows, :]
    h_scr[0:prev_rows, :] = tail

    @pl.when(step == last)
    def _():
        state_ref[...] = tail if sample else tail[prev_rows - 2:, :]

    act = (jax.nn.silu(hc[:, D_FF:]) * hc[:, :D_FF]).astype(BF16)
    x2 = x1 + _dot(act, wdn_ref[...])
    xo_ref[...] = x2
    if final:
        y_ref[...] = _rms(x2, gfin_ref[...])


def _ffn(l, final, sample, x, mix, w_o, norm_ffn_g, w_up, conv_f_w, conv_f_b, w_down,
         norm_final_g, prev_tm=None):
    if sample:
        rows_total = x.shape[0]
        stride = rows_total // 4
        tm, prev_rows = stride, 2 * stride
        grid = (4,)
        tok = pl.BlockSpec((tm, D), lambda i: (i, 0))
        wmap = lambda n: (lambda i: (l,) + (0,) * n)
        state_spec = pl.BlockSpec((prev_rows, D_FF2), lambda i: (0, 0))
        state_shape = jax.ShapeDtypeStruct((prev_rows, D_FF2), F32)
        x_shape = jax.ShapeDtypeStruct((rows_total, D), F32)
        gfin_spec = pl.BlockSpec((1, D), lambda i: (0, 0))
        sem = ("arbitrary",)
    else:
        bsz, t, _ = x.shape
        stride, tm, prev_rows = 1, TM_FFN, SUBLANES
        grid = (bsz, t // tm)
        tok = pl.BlockSpec((None, tm, D), lambda b, i: (b, i, 0))
        wmap = lambda n: (lambda b, i: (l,) + (0,) * n)
        state_spec = pl.BlockSpec((None, 2, D_FF2), lambda b, i: (b, 0, 0))
        state_shape = jax.ShapeDtypeStruct((bsz, 2, D_FF2), F32)
        x_shape = jax.ShapeDtypeStruct((bsz, t, D), F32)
        gfin_spec = pl.BlockSpec((1, D), lambda b, i: (0, 0))
        sem = ("arbitrary", "arbitrary")
    wspec = lambda shape: _resident((None,) + shape, wmap(len(shape)))
    in_specs = [tok, tok, wspec((D, D)), wspec((1, D)), wspec((D, D_FF2)), wspec((3, D_FF2)),
                wspec((1, D_FF2)), wspec((D_FF, D))]
    args = [x, mix, w_o, norm_ffn_g, w_up, conv_f_w, conv_f_b, w_down]
    if sample:
        in_specs.append(wspec((prev_rows, D_FF2)))
        args.append(prev_tm)
    out_specs = [tok, state_spec]
    out_shape = [x_shape, state_shape]
    if final:
        in_specs.append(gfin_spec)
        args.append(norm_final_g)
        out_specs.append(tok)
        out_shape.append(x_shape)
    return pl.pallas_call(
        functools.partial(_ffn_kernel, sample=sample, final=final, rows=tm,
                          prev_rows=prev_rows, stride=stride),
        grid=grid,
        in_specs=in_specs,
        out_specs=out_specs,
        out_shape=out_shape,
        scratch_shapes=[pltpu.VMEM((tm + prev_rows, D_FF2), F32)],
        compiler_params=pltpu.CompilerParams(dimension_semantics=sem,
                                             vmem_limit_bytes=VMEM_LIMIT),
        name="ffn_sample" if sample else "ffn_prompt",
    )(*args)


def _to_token_major(a):
    return jnp.swapaxes(a, 0, 1).reshape(a.shape[0] * a.shape[1], a.shape[2])


def _to_batch_major(a, t):
    return jnp.swapaxes(a.reshape(t, a.shape[0] // t, a.shape[1]), 0, 1)


def kernel(x_prompt, x_sample, mem_prompt, cache_conv_a, cache_conv_ffn, cache_mem_k, cache_mem_v, norm_mix_g, w_in, conv_a_w, sg_ln_g, sg_w, sg_b, norm_mem_g, w_k, w_v, w_o, norm_ffn_g, w_up, conv_f_w, conv_f_b, w_down, norm_final_g):
    depth = w_in.shape[0]
    bp, seq, _ = x_prompt.shape
    bs, ts, _ = x_sample.shape
    rows_s = bs * ts

    w_in_b = w_in.astype(BF16)
    w_o_b = w_o.astype(BF16)
    w_up_b = w_up.astype(BF16)
    w_down_b = w_down.astype(BF16)
    wkv_b = jnp.concatenate([w_k, w_v], axis=-1).astype(BF16)

    nmg = norm_mix_g[:, None, :]
    lng = sg_ln_g[:, None, :]
    nfg = norm_ffn_g[:, None, :]
    nmemg = norm_mem_g[:, None, :]
    cfb = conv_f_b[:, None, :]
    gfin = norm_final_g[None, :]
    sg_b_t = jnp.swapaxes(sg_b, 1, 2)
    sg_w_s = jnp.repeat(jnp.repeat(sg_w[:, :, :ts, :ts], bs, axis=2), bs, axis=3)
    sg_b_s = jnp.repeat(sg_b_t[:, :ts, :], bs, axis=1)
    prev_a_s = jnp.swapaxes(cache_conv_a, 1, 2).reshape(depth, 2 * bs, D)
    prev_f_s = jnp.swapaxes(cache_conv_ffn, 1, 2).reshape(depth, 2 * bs, D_FF2)

    k_f, v_f, k_b, v_b = _mem_kv(mem_prompt.reshape(bp * N_MEM, D), nmemg, wkv_b)
    k_b = k_b.reshape(depth * bp, N_MEM, D)
    v_b = v_b.reshape(depth * bp, N_MEM, D)

    xp = x_prompt
    xs = _to_token_major(x_sample)
    pa, pf, sa, sf, sv = [], [], [], [], []
    yp = ys = None
    for l in range(depth):
        final = l == depth - 1
        mix, a_state = _mixer_prompt(l, xp, nmg, w_in_b, conv_a_w, lng, sg_w, sg_b_t, k_b, v_b)
        outs = _ffn(l, final, False, xp, mix, w_o_b, nfg, w_up_b, conv_f_w, cfb, w_down_b, gfin)
        xp, f_state = outs[0], outs[1]
        if final:
            yp = outs[2]
        pa.append(a_state)
        pf.append(f_state)
        mixab, gm, q, vn, a_state_s = _mixer_sample(l, xs, nmg, w_in_b, conv_a_w, lng, sg_w_s,
                                                    sg_b_s, prev_a_s)
        pad = lambda a: jnp.pad(_to_batch_major(a, ts), ((0, 0), (0, Q_PAD - ts), (0, 0)))
        mix_s = _sample_attn(l, pad(q), pad(gm), pad(mixab), cache_mem_k, cache_mem_v)
        mix_s = _to_token_major(mix_s[:, :ts, :])
        outs = _ffn(l, final, True, xs, mix_s, w_o_b, nfg, w_up_b, conv_f_w, cfb, w_down_b, gfin,
                    prev_tm=prev_f_s)
        xs, f_state_s = outs[0], outs[1]
        if final:
            ys = outs[2]
        sa.append(_to_batch_major(a_state_s, 2))
        sf.append(_to_batch_major(f_state_s, 2))
        sv.append(_to_batch_major(vn, ts))

    kv_shape = (depth, bp, N_MEM, HEADS, HEAD_DIM)
    return (yp, _to_batch_major(ys, ts),
            jnp.stack(pa), jnp.stack(pf), k_f.reshape(kv_shape), v_f.reshape(kv_shape),
            jnp.stack(sa), jnp.stack(sf), jnp.stack(sv))
```

```python
import functools

import jax
import jax.numpy as jnp
from jax import lax
from jax.experimental import pallas as pl
from jax.experimental.pallas import tpu as pltpu

F32 = jnp.float32
BF16 = jnp.bfloat16

D = 1024
D_IN = 9 * D
D_FF = 2816
D_FF2 = 2 * D_FF
N_MEM = 256
HEADS = 4
HEAD_DIM = D // HEADS
SG_GROUPS = 4
SG_COLS = D // SG_GROUPS
SG_CHUNK = 128
EPS = 1e-6

V7X_VMEM_BYTES = 64 * 1024 * 1024
VMEM_LIMIT = V7X_VMEM_BYTES - 6 * 1024 * 1024
SUBLANES = 8

TM_MIX = 512
NSUB_MIX = 2
TM_FFN = 512
NSUB_FFN = 2
TM_KV = 512
ATT_BB = 4
Q_PAD = SUBLANES


def _rms(x, g):
    y = x * lax.rsqrt(jnp.mean(x * x, axis=-1, keepdims=True) + EPS)
    return y * g


def _dot(a, b):
    return jnp.dot(a, b, preferred_element_type=F32)


def _resident(shape, index_map):
    return pl.BlockSpec(shape, index_map, pipeline_mode=pl.Buffered(1))


def _causal_conv(scr, w_ref, cur, rows, off, stride):
    scr[off:off + rows, :] = cur
    x2 = scr[off - 2 * stride:off - 2 * stride + rows, :]
    x1 = scr[off - stride:off - stride + rows, :]
    return w_ref[0:1, :] * x2 + w_ref[1:2, :] * x1 + w_ref[2:3, :] * cur


def _kv_kernel(mem_ref, g_ref, wkv_ref, k_ref, v_ref, kb_ref, vb_ref):
    m = _rms(mem_ref[...], g_ref[...]).astype(BF16)
    k = _dot(m, wkv_ref[:, :D])
    v = _dot(m, wkv_ref[:, D:])
    for h in range(HEADS):
        hc = slice(h * HEAD_DIM, (h + 1) * HEAD_DIM)
        k_ref[:, h, :] = k[:, hc]
        v_ref[:, h, :] = v[:, hc]
    kb_ref[...] = k.astype(BF16)
    vb_ref[...] = v.astype(BF16)


def _mem_kv(mem2d, norm_mem_g, wkv):
    depth = wkv.shape[0]
    rows = mem2d.shape[0]
    out_f = jax.ShapeDtypeStruct((depth, rows, HEADS, HEAD_DIM), F32)
    out_b = jax.ShapeDtypeStruct((depth, rows, D), BF16)
    oblk = pl.BlockSpec((None, TM_KV, D), lambda l, i: (l, i, 0))
    oblk_f = pl.BlockSpec((None, TM_KV, HEADS, HEAD_DIM), lambda l, i: (l, i, 0, 0))
    return pl.pallas_call(
        _kv_kernel,
        grid=(depth, rows // TM_KV),
        in_specs=[
            pl.BlockSpec((TM_KV, D), lambda l, i: (i, 0)),
            pl.BlockSpec((None, 1, D), lambda l, i: (l, 0, 0)),
            pl.BlockSpec((None, D, 2 * D), lambda l, i: (l, 0, 0)),
        ],
        out_specs=[oblk_f, oblk_f, oblk, oblk],
        out_shape=[out_f, out_f, out_b, out_b],
        compiler_params=pltpu.CompilerParams(
            dimension_semantics=("arbitrary", "arbitrary"), vmem_limit_bytes=VMEM_LIMIT),
        name="mem_kv",
    )(mem2d, norm_mem_g, wkv)


def _mixer_kernel(*refs, sample, rows, nsub, prev_rows, stride, chunk):
    if sample:
        (x_ref, nm_ref, win_ref, caw_ref, lng_ref, sgw_ref, sgb_ref, prev_ref,
         mixab_ref, gm_ref, q_ref, vn_ref, state_ref, a_scr) = refs
    else:
        (x_ref, nm_ref, win_ref, caw_ref, lng_ref, sgw_ref, sgb_ref, kb_ref, vb_ref,
         mix_ref, state_ref, a_scr) = refs

    if sample:
        a_scr[0:prev_rows, :] = prev_ref[...]
    else:
        @pl.when(pl.program_id(1) == 0)
        def _():
            a_scr[0:prev_rows, :] = jnp.zeros((prev_rows, D), F32)

    ri = lax.broadcasted_iota(jnp.int32, (chunk, chunk), 0)
    ci = lax.broadcasted_iota(jnp.int32, (chunk, chunk), 1)
    if sample:
        shift = stride.bit_length() - 1
        keep = ((ri & (stride - 1)) == (ci & (stride - 1))) & ((ci >> shift) <= (ri >> shift))
    else:
        keep = ci <= ri
    sgw = [jnp.where(keep, sgw_ref[g], 0.0).astype(BF16) for g in range(SG_GROUPS)]

    H, C, B_, U, V, Q, GA, GB, GM = range(9)
    proj_groups = [(V, Q, U), (GB, C, H), (B_, GA, GM)]
    sub = rows // nsub

    def start(t):
        rs = slice(t * sub, (t + 1) * sub)
        return {"t": t, "rs": rs, "z": _rms(x_ref[rs, :], nm_ref[...]).astype(BF16)}

    def project(st, group):
        for i in group:
            st[i] = _dot(st["z"], win_ref[:, i * D:(i + 1) * D])

    def spatial_and_scores(st):
        v = st[V]
        vc = v - jnp.mean(v, axis=-1, keepdims=True)
        vn = vc * lax.rsqrt(jnp.mean(vc * vc, axis=-1, keepdims=True) + EPS) * lng_ref[...]
        if sample:
            vn_ref[st["rs"], :] = vn
        vnb = vn.astype(BF16)
        s_rows = []
        for c in range(sub // chunk):
            s_cols = []
            for g in range(SG_GROUPS):
                vg = vnb[c * chunk:(c + 1) * chunk, g * SG_COLS:(g + 1) * SG_COLS]
                s_cols.append(_dot(sgw[g], vg) + sgb_ref[:, g:g + 1])
            s_rows.append(jnp.concatenate(s_cols, axis=1))
        st["s"] = s_rows[0] if len(s_rows) == 1 else jnp.concatenate(s_rows, axis=0)
        if not sample:
            st["sc"] = [
                lax.dot_general(st[Q][:, h * HEAD_DIM:(h + 1) * HEAD_DIM].astype(BF16),
                                kb_ref[:, h * HEAD_DIM:(h + 1) * HEAD_DIM],
                                (((1,), (1,)), ((), ())), preferred_element_type=F32)
                for h in range(HEADS)]

    def attend(st):
        o_cols = []
        for h in range(HEADS):
            sc = st["sc"][h] * (HEAD_DIM ** -0.5)
            e = jnp.exp(sc - jnp.max(sc, axis=-1, keepdims=True))
            p = e * (1.0 / jnp.sum(e, axis=-1, keepdims=True))
            o_cols.append(_dot(p.astype(BF16), vb_ref[:, h * HEAD_DIM:(h + 1) * HEAD_DIM]))
        st["y_m"] = jnp.concatenate(o_cols, axis=1)

    def finish(st):
        rs = st["rs"]
        a = st[C] * st[H]
        conv = _causal_conv(a_scr, caw_ref, a, sub, prev_rows + st["t"] * sub, stride)
        acc = jax.nn.sigmoid(st[GA]) * (st[B_] * conv)
        acc = acc + jax.nn.sigmoid(st[GB]) * (st[U] * st["s"])
        gm = jax.nn.sigmoid(st[GM])
        if sample:
            mixab_ref[rs, :] = acc
            gm_ref[rs, :] = gm
            q_ref[rs, :] = st[Q]
        else:
            mix_ref[rs, :] = (acc + gm * st["y_m"]).astype(BF16)

    tails = [spatial_and_scores] + ([] if sample else [attend]) + [finish]
    prev = None
    for t in range(nsub):
        cur = start(t)
        pending = list(tails) if prev is not None else []
        for group in proj_groups:
            project(cur, group)
            if pending:
                pending.pop(0)(prev)
        for fn in pending:
            fn(prev)
        prev = cur
    for fn in tails:
        fn(prev)

    tail = a_scr[rows:rows + prev_rows, :]
    if sample:
        state_ref[...] = tail
    else:
        a_scr[0:prev_rows, :] = tail

        @pl.when(pl.program_id(1) == pl.num_programs(1) - 1)
        def _():
            state_ref[...] = tail[prev_rows - 2:, :]


def _mixer_prompt(l, x, norm_mix_g, w_in, conv_a_w, sg_ln_g, sg_w, sg_b_t, kb, vb):
    bsz, t, _ = x.shape
    tm = TM_MIX
    prev_rows = SUBLANES
    wspec = lambda shape: _resident((None,) + shape, lambda b, i: (l,) + (0,) * len(shape))
    return pl.pallas_call(
        functools.partial(_mixer_kernel, sample=False, rows=tm, nsub=NSUB_MIX,
                          prev_rows=prev_rows, stride=1, chunk=SG_CHUNK),
        grid=(bsz, t // tm),
        in_specs=[
            pl.BlockSpec((None, tm, D), lambda b, i: (b, i, 0)),
            wspec((1, D)),
            wspec((D, D_IN)),
            wspec((3, D)),
            wspec((1, D)),
            wspec((SG_GROUPS, SG_CHUNK, SG_CHUNK)),
            wspec((SG_CHUNK, SG_GROUPS)),
            pl.BlockSpec((None, N_MEM, D), lambda b, i: (l * bsz + b, 0, 0)),
            pl.BlockSpec((None, N_MEM, D), lambda b, i: (l * bsz + b, 0, 0)),
        ],
        out_specs=[
            pl.BlockSpec((None, tm, D), lambda b, i: (b, i, 0)),
            pl.BlockSpec((None, 2, D), lambda b, i: (b, 0, 0)),
        ],
        out_shape=[
            jax.ShapeDtypeStruct((bsz, t, D), BF16),
            jax.ShapeDtypeStruct((bsz, 2, D), F32),
        ],
        scratch_shapes=[pltpu.VMEM((tm + prev_rows, D), F32)],
        compiler_params=pltpu.CompilerParams(
            dimension_semantics=("arbitrary", "arbitrary"), vmem_limit_bytes=VMEM_LIMIT),
        name="mixer_prompt",
    )(x, norm_mix_g, w_in, conv_a_w, sg_ln_g, sg_w, sg_b_t, kb, vb)


def _mixer_sample(l, x_tm, norm_mix_g, w_in, conv_a_w, sg_ln_g, sg_w_tiled, sg_b_rows, prev_tm):
    rows = x_tm.shape[0]
    stride = rows // 4
    prev_rows = 2 * stride
    assert stride & (stride - 1) == 0
    wspec = lambda shape: _resident((None,) + shape, lambda i: (l,) + (0,) * len(shape))
    full = lambda r: pl.BlockSpec((r, D), lambda i: (0, 0))
    out_f = jax.ShapeDtypeStruct((rows, D), F32)
    return pl.pallas_call(
        functools.partial(_mixer_kernel, sample=True, rows=rows, nsub=1, prev_rows=prev_rows,
                          stride=stride, chunk=rows),
        grid=(1,),
        in_specs=[
            full(rows),
            wspec((1, D)),
            wspec((D, D_IN)),
            wspec((3, D)),
            wspec((1, D)),
            wspec((SG_GROUPS, rows, rows)),
            wspec((rows, SG_GROUPS)),
            wspec((prev_rows, D)),
        ],
        out_specs=[full(rows), full(rows), full(rows), full(rows), full(prev_rows)],
        out_shape=[out_f, out_f, out_f, out_f, jax.ShapeDtypeStruct((prev_rows, D), F32)],
        scratch_shapes=[pltpu.VMEM((rows + prev_rows, D), F32)],
        compiler_params=pltpu.CompilerParams(
            dimension_semantics=("arbitrary",), vmem_limit_bytes=VMEM_LIMIT),
        name="mixer_sample",
    )(x_tm, norm_mix_g, w_in, conv_a_w, sg_ln_g, sg_w_tiled, sg_b_rows, prev_tm)


def _sample_attn_kernel(q_ref, gm_ref, mixab_ref, k_ref, v_ref, mix_ref):
    nq = HEADS * Q_PAD
    nk = N_MEM * HEADS
    row_head = lax.broadcasted_iota(jnp.int32, (nq, nk), 0) >> (Q_PAD.bit_length() - 1)
    col_head = lax.broadcasted_iota(jnp.int32, (nq, nk), 1) & (HEADS - 1)
    same_head = row_head == col_head
    for b in range(ATT_BB):
        q = q_ref[b]
        q32 = jnp.concatenate([q[:, h * HEAD_DIM:(h + 1) * HEAD_DIM] for h in range(HEADS)], axis=0)
        k2 = k_ref[b].reshape(nk, HEAD_DIM).astype(BF16)
        v2 = v_ref[b].reshape(nk, HEAD_DIM).astype(BF16)
        sc = lax.dot_general(q32.astype(BF16), k2, (((1,), (1,)), ((), ())),
                             preferred_element_type=F32) * (HEAD_DIM ** -0.5)
        sc = jnp.where(same_head, sc, -jnp.inf)
        e = jnp.exp(sc - jnp.max(sc, axis=-1, keepdims=True))
        p = e * (1.0 / jnp.sum(e, axis=-1, keepdims=True))
        o32 = _dot(p.astype(BF16), v2)
        y = jnp.concatenate([o32[h * Q_PAD:(h + 1) * Q_PAD] for h in range(HEADS)], axis=1)
        mix_ref[b] = mixab_ref[b] + gm_ref[b] * y


def _sample_attn(l, q, gm, mixab, cache_k, cache_v):
    bsz = q.shape[0]
    small = pl.BlockSpec((ATT_BB, Q_PAD, D), lambda i: (i, 0, 0))
    big = pl.BlockSpec((None, ATT_BB, N_MEM, HEADS, HEAD_DIM), lambda i: (l, i, 0, 0, 0))
    return pl.pallas_call(
        _sample_attn_kernel,
        grid=(bsz // ATT_BB,),
        in_specs=[small, small, small, big, big],
        out_specs=small,
        out_shape=jax.ShapeDtypeStruct((bsz, Q_PAD, D), F32),
        compiler_params=pltpu.CompilerParams(
            dimension_semantics=("arbitrary",), vmem_limit_bytes=VMEM_LIMIT),
        name="sample_attn",
    )(q, gm, mixab, cache_k, cache_v)


def _ffn_kernel(*refs, sample, final, rows, nsub, prev_rows, stride):
    refs = list(refs)
    x_ref, mix_ref, wo_ref, g_ref, wup_ref, cw_ref, cb_ref, wdn_ref = refs[:8]
    refs = refs[8:]
    prev_ref = refs.pop(0) if sample else None
    gfin_ref = refs.pop(0) if final else None
    xo_ref, state_ref = refs[:2]
    refs = refs[2:]
    y_ref = refs.pop(0) if final else None
    h_scr, = refs

    step = pl.program_id(0) if sample else pl.program_id(1)
    last = (pl.num_programs(0) if sample else pl.num_programs(1)) - 1

    @pl.when(step == 0)
    def _():
        if sample:
            h_scr[0:prev_rows, :] = prev_ref[...]
        else:
            h_scr[0:prev_rows, :] = jnp.zeros((prev_rows, D_FF2), F32)

    sub = rows // nsub
    subs = [slice(t * sub, (t + 1) * sub) for t in range(nsub)]
    x1 = [x_ref[rs, :] + _dot(mix_ref[rs, :].astype(BF16), wo_ref[...]) for rs in subs]
    h = [_dot(_rms(x1[t], g_ref[...]).astype(BF16), wup_ref[...]) for t in range(nsub)]
    for t, rs in enumerate(subs):
        hc = _causal_conv(h_scr, cw_ref, h[t], sub, prev_rows + t * sub, stride) + cb_ref[...]
        act = (jax.nn.silu(hc[:, D_FF:]) * hc[:, :D_FF]).astype(BF16)
        x2 = x1[t] + _dot(act, wdn_ref[...])
        xo_ref[rs, :] = x2
        if final:
            y_ref[rs, :] = _rms(x2, gfin_ref[...])

    tail = h_scr[rows:rows + prev_rows, :]
    h_scr[0:prev_rows, :] = tail

    @pl.when(step == last)
    def _():
        state_ref[...] = tail if sample else tail[prev_rows - 2:, :]


def _ffn(l, final, sample, x, mix, w_o, norm_ffn_g, w_up, conv_f_w, conv_f_b, w_down,
         norm_final_g, prev_tm=None):
    if sample:
        rows_total = x.shape[0]
        stride = rows_total // 4
        tm, prev_rows = stride, 2 * stride
        grid = (4,)
        tok = pl.BlockSpec((tm, D), lambda i: (i, 0))
        wmap = lambda n: (lambda i: (l,) + (0,) * n)
        state_spec = pl.BlockSpec((prev_rows, D_FF2), lambda i: (0, 0))
        state_shape = jax.ShapeDtypeStruct((prev_rows, D_FF2), F32)
        x_shape = jax.ShapeDtypeStruct((rows_total, D), F32)
        gfin_spec = pl.BlockSpec((1, D), lambda i: (0, 0))
        sem = ("arbitrary",)
    else:
        bsz, t, _ = x.shape
        stride, tm, prev_rows = 1, TM_FFN, SUBLANES
        grid = (bsz, t // tm)
        tok = pl.BlockSpec((None, tm, D), lambda b, i: (b, i, 0))
        wmap = lambda n: (lambda b, i: (l,) + (0,) * n)
        state_spec = pl.BlockSpec((None, 2, D_FF2), lambda b, i: (b, 0, 0))
        state_shape = jax.ShapeDtypeStruct((bsz, 2, D_FF2), F32)
        x_shape = jax.ShapeDtypeStruct((bsz, t, D), F32)
        gfin_spec = pl.BlockSpec((1, D), lambda b, i: (0, 0))
        sem = ("arbitrary", "arbitrary")
    wspec = lambda shape: _resident((None,) + shape, wmap(len(shape)))
    in_specs = [tok, tok, wspec((D, D)), wspec((1, D)), wspec((D, D_FF2)), wspec((3, D_FF2)),
                wspec((1, D_FF2)), wspec((D_FF, D))]
    args = [x, mix, w_o, norm_ffn_g, w_up, conv_f_w, conv_f_b, w_down]
    if sample:
        in_specs.append(wspec((prev_rows, D_FF2)))
        args.append(prev_tm)
    out_specs = [tok, state_spec]
    out_shape = [x_shape, state_shape]
    if final:
        in_specs.append(gfin_spec)
        args.append(norm_final_g)
        out_specs.append(tok)
        out_shape.append(x_shape)
    return pl.pallas_call(
        functools.partial(_ffn_kernel, sample=sample, final=final, rows=tm,
                          nsub=1 if sample else NSUB_FFN, prev_rows=prev_rows, stride=stride),
        grid=grid,
        in_specs=in_specs,
        out_specs=out_specs,
        out_shape=out_shape,
        scratch_shapes=[pltpu.VMEM((tm + prev_rows, D_FF2), F32)],
        compiler_params=pltpu.CompilerParams(dimension_semantics=sem,
                                             vmem_limit_bytes=VMEM_LIMIT),
        name="ffn_sample" if sample else "ffn_prompt",
    )(*args)


def _to_token_major(a):
    return jnp.swapaxes(a, 0, 1).reshape(a.shape[0] * a.shape[1], a.shape[2])


def _to_batch_major(a, t):
    return jnp.swapaxes(a.reshape(t, a.shape[0] // t, a.shape[1]), 0, 1)


def kernel(x_prompt, x_sample, mem_prompt, cache_conv_a, cache_conv_ffn, cache_mem_k, cache_mem_v, norm_mix_g, w_in, conv_a_w, sg_ln_g, sg_w, sg_b, norm_mem_g, w_k, w_v, w_o, norm_ffn_g, w_up, conv_f_w, conv_f_b, w_down, norm_final_g):
    depth = w_in.shape[0]
    bp, seq, _ = x_prompt.shape
    bs, ts, _ = x_sample.shape
    rows_s = bs * ts

    w_in_b = w_in.astype(BF16)
    w_o_b = w_o.astype(BF16)
    w_up_b = w_up.astype(BF16)
    w_down_b = w_down.astype(BF16)
    wkv_b = jnp.concatenate([w_k, w_v], axis=-1).astype(BF16)

    nmg = norm_mix_g[:, None, :]
    lng = sg_ln_g[:, None, :]
    nfg = norm_ffn_g[:, None, :]
    nmemg = norm_mem_g[:, None, :]
    cfb = conv_f_b[:, None, :]
    gfin = norm_final_g[None, :]
    sg_b_t = jnp.swapaxes(sg_b, 1, 2)
    sg_w_s = jnp.repeat(jnp.repeat(sg_w[:, :, :ts, :ts], bs, axis=2), bs, axis=3)
    sg_b_s = jnp.repeat(sg_b_t[:, :ts, :], bs, axis=1)
    prev_a_s = jnp.swapaxes(cache_conv_a, 1, 2).reshape(depth, 2 * bs, D)
    prev_f_s = jnp.swapaxes(cache_conv_ffn, 1, 2).reshape(depth, 2 * bs, D_FF2)

    k_f, v_f, k_b, v_b = _mem_kv(mem_prompt.reshape(bp * N_MEM, D), nmemg, wkv_b)
    k_b = k_b.reshape(depth * bp, N_MEM, D)
    v_b = v_b.reshape(depth * bp, N_MEM, D)

    xp = x_prompt
    xs = _to_token_major(x_sample)
    pa, pf, sa, sf, sv = [], [], [], [], []
    yp = ys = None
    for l in range(depth):
        final = l == depth - 1
        mix, a_state = _mixer_prompt(l, xp, nmg, w_in_b, conv_a_w, lng, sg_w, sg_b_t, k_b, v_b)
        outs = _ffn(l, final, False, xp, mix, w_o_b, nfg, w_up_b, conv_f_w, cfb, w_down_b, gfin)
        xp, f_state = outs[0], outs[1]
        if final:
            yp = outs[2]
        pa.append(a_state)
        pf.append(f_state)
        mixab, gm, q, vn, a_state_s = _mixer_sample(l, xs, nmg, w_in_b, conv_a_w, lng, sg_w_s,
                                                    sg_b_s, prev_a_s)
        pad = lambda a: jnp.pad(_to_batch_major(a, ts), ((0, 0), (0, Q_PAD - ts), (0, 0)))
        mix_s = _sample_attn(l, pad(q), pad(gm), pad(mixab), cache_mem_k, cache_mem_v)
        mix_s = _to_token_major(mix_s[:, :ts, :])
        outs = _ffn(l, final, True, xs, mix_s, w_o_b, nfg, w_up_b, conv_f_w, cfb, w_down_b, gfin,
                    prev_tm=prev_f_s)
        xs, f_state_s = outs[0], outs[1]
        if final:
            ys = outs[2]
        sa.append(_to_batch_major(a_state_s, 2))
        sf.append(_to_batch_major(f_state_s, 2))
        sv.append(_to_batch_major(vn, ts))

    kv_shape = (depth, bp, N_MEM, HEADS, HEAD_DIM)
    return (yp, _to_batch_major(ys, ts),
            jnp.stack(pa), jnp.stack(pf), k_f.reshape(kv_shape), v_f.reshape(kv_shape),
            jnp.stack(sa), jnp.stack(sf), jnp.stack(sv))
```

```python
import functools

import jax
import jax.numpy as jnp
from jax import lax
from jax.experimental import pallas as pl
from jax.experimental.pallas import tpu as pltpu

F32 = jnp.float32
BF16 = jnp.bfloat16

D = 1024
D_IN = 9 * D
D_FF = 2816
D_FF2 = 2 * D_FF
N_MEM = 256
HEADS = 4
HEAD_DIM = D // HEADS
SG_GROUPS = 4
SG_COLS = D // SG_GROUPS
SG_CHUNK = 128
EPS = 1e-6
H, C, B_, U, V, Q, GA, GB, GM = range(9)
PROJ_GROUPS = [(V, Q, U), (GB, C, H), (B_, GA, GM)]

V7X_VMEM_BYTES = 64 * 1024 * 1024
VMEM_LIMIT = V7X_VMEM_BYTES - 6 * 1024 * 1024
SUBLANES = 8

TM_MIX = 512
NSUB_MIX = 2
TM_FFN = 512
NSUB_FFN = 2
TM_KV = 512
ATT_BB = 4
MASKED = -1e30


def _rms(x, g):
    y = x * lax.rsqrt(jnp.mean(x * x, axis=-1, keepdims=True) + EPS)
    return y * g


def _layernorm(x, g):
    xc = x - jnp.mean(x, axis=-1, keepdims=True)
    return xc * lax.rsqrt(jnp.mean(xc * xc, axis=-1, keepdims=True) + EPS) * g


def _dot(a, b):
    return jnp.dot(a, b, preferred_element_type=F32)


def _resident(shape, index_map):
    return pl.BlockSpec(shape, index_map, pipeline_mode=pl.Buffered(1))


def _causal_conv(scr, w_ref, cur, rows, off):
    scr[off:off + rows, :] = cur
    x2 = scr[off - 2:off - 2 + rows, :]
    x1 = scr[off - 1:off - 1 + rows, :]
    return w_ref[0:1, :] * x2 + w_ref[1:2, :] * x1 + w_ref[2:3, :] * cur


def _silu_gate(hc):
    return (jax.nn.silu(hc[:, D_FF:]) * hc[:, :D_FF]).astype(BF16)


def _kv_kernel(mem_ref, g_ref, wkv_ref, k_ref, v_ref, kb_ref, vb_ref):
    m = _rms(mem_ref[...], g_ref[...]).astype(BF16)
    k = _dot(m, wkv_ref[:, :D])
    v = _dot(m, wkv_ref[:, D:])
    for h in range(HEADS):
        hc = slice(h * HEAD_DIM, (h + 1) * HEAD_DIM)
        k_ref[:, h, :] = k[:, hc]
        v_ref[:, h, :] = v[:, hc]
    kb_ref[...] = k.astype(BF16)
    vb_ref[...] = v.astype(BF16)


def _mem_kv(mem2d, norm_mem_g, wkv):
    depth = wkv.shape[0]
    rows = mem2d.shape[0]
    out_f = jax.ShapeDtypeStruct((depth, rows, HEADS, HEAD_DIM), F32)
    out_b = jax.ShapeDtypeStruct((depth, rows, D), BF16)
    oblk = pl.BlockSpec((None, TM_KV, D), lambda l, i: (l, i, 0))
    oblk_f = pl.BlockSpec((None, TM_KV, HEADS, HEAD_DIM), lambda l, i: (l, i, 0, 0))
    return pl.pallas_call(
        _kv_kernel,
        grid=(depth, rows // TM_KV),
        in_specs=[
            pl.BlockSpec((TM_KV, D), lambda l, i: (i, 0)),
            pl.BlockSpec((None, 1, D), lambda l, i: (l, 0, 0)),
            pl.BlockSpec((None, D, 2 * D), lambda l, i: (l, 0, 0)),
        ],
        out_specs=[oblk_f, oblk_f, oblk, oblk],
        out_shape=[out_f, out_f, out_b, out_b],
        compiler_params=pltpu.CompilerParams(
            dimension_semantics=("arbitrary", "arbitrary"), vmem_limit_bytes=VMEM_LIMIT),
        name="mem_kv",
    )(mem2d, norm_mem_g, wkv)


def _mixer_kernel(x_ref, nm_ref, win_ref, caw_ref, lng_ref, sgw_ref, sgb_ref, kb_ref, vb_ref,
                  mix_ref, state_ref, a_scr, *, rows, nsub, prev_rows):
    @pl.when(pl.program_id(1) == 0)
    def _():
        a_scr[0:prev_rows, :] = jnp.zeros((prev_rows, D), F32)

    ri = lax.broadcasted_iota(jnp.int32, (SG_CHUNK, SG_CHUNK), 0)
    ci = lax.broadcasted_iota(jnp.int32, (SG_CHUNK, SG_CHUNK), 1)
    sgw = [jnp.where(ci <= ri, sgw_ref[g], 0.0).astype(BF16) for g in range(SG_GROUPS)]
    sub = rows // nsub

    def start(t):
        rs = slice(t * sub, (t + 1) * sub)
        return {"t": t, "rs": rs, "z": _rms(x_ref[rs, :], nm_ref[...]).astype(BF16)}

    def project(st, group):
        for i in group:
            st[i] = _dot(st["z"], win_ref[:, i * D:(i + 1) * D])

    def spatial_and_scores(st):
        vnb = _layernorm(st[V], lng_ref[...]).astype(BF16)
        s_rows = []
        for c in range(sub // SG_CHUNK):
            s_cols = []
            for g in range(SG_GROUPS):
                vg = vnb[c * SG_CHUNK:(c + 1) * SG_CHUNK, g * SG_COLS:(g + 1) * SG_COLS]
                s_cols.append(_dot(sgw[g], vg) + sgb_ref[:, g:g + 1])
            s_rows.append(jnp.concatenate(s_cols, axis=1))
        st["s"] = jnp.concatenate(s_rows, axis=0)
        st["sc"] = [
            lax.dot_general(st[Q][:, h * HEAD_DIM:(h + 1) * HEAD_DIM].astype(BF16),
                            kb_ref[:, h * HEAD_DIM:(h + 1) * HEAD_DIM],
                            (((1,), (1,)), ((), ())), preferred_element_type=F32)
            for h in range(HEADS)]

    def attend(st):
        o_cols = []
        for h in range(HEADS):
            sc = st["sc"][h] * (HEAD_DIM ** -0.5)
            e = jnp.exp(sc - jnp.max(sc, axis=-1, keepdims=True))
            p = e * (1.0 / jnp.sum(e, axis=-1, keepdims=True))
            o_cols.append(_dot(p.astype(BF16), vb_ref[:, h * HEAD_DIM:(h + 1) * HEAD_DIM]))
        st["y_m"] = jnp.concatenate(o_cols, axis=1)

    def finish(st):
        a = st[C] * st[H]
        conv = _causal_conv(a_scr, caw_ref, a, sub, prev_rows + st["t"] * sub)
        acc = jax.nn.sigmoid(st[GA]) * (st[B_] * conv)
        acc = acc + jax.nn.sigmoid(st[GB]) * (st[U] * st["s"])
        acc = acc + jax.nn.sigmoid(st[GM]) * st["y_m"]
        mix_ref[st["rs"], :] = acc.astype(BF16)

    tails = [spatial_and_scores, attend, finish]
    prev = None
    for t in range(nsub):
        cur = start(t)
        pending = list(tails) if prev is not None else []
        for group in PROJ_GROUPS:
            project(cur, group)
            if pending:
                pending.pop(0)(prev)
        for fn in pending:
            fn(prev)
        prev = cur
    for fn in tails:
        fn(prev)

    tail = a_scr[rows:rows + prev_rows, :]
    a_scr[0:prev_rows, :] = tail

    @pl.when(pl.program_id(1) == pl.num_programs(1) - 1)
    def _():
        state_ref[...] = tail[prev_rows - 2:, :]


def _mixer_prompt(l, x, norm_mix_g, w_in, conv_a_w, sg_ln_g, sg_w, sg_b_t, kb, vb):
    bsz, t, _ = x.shape
    tm = TM_MIX
    prev_rows = SUBLANES
    wspec = lambda shape: _resident((None,) + shape, lambda b, i: (l,) + (0,) * len(shape))
    return pl.pallas_call(
        functools.partial(_mixer_kernel, rows=tm, nsub=NSUB_MIX, prev_rows=prev_rows),
        grid=(bsz, t // tm),
        in_specs=[
            pl.BlockSpec((None, tm, D), lambda b, i: (b, i, 0)),
            wspec((1, D)),
            wspec((D, D_IN)),
            wspec((3, D)),
            wspec((1, D)),
            wspec((SG_GROUPS, SG_CHUNK, SG_CHUNK)),
            wspec((SG_CHUNK, SG_GROUPS)),
            pl.BlockSpec((None, N_MEM, D), lambda b, i: (l * bsz + b, 0, 0)),
            pl.BlockSpec((None, N_MEM, D), lambda b, i: (l * bsz + b, 0, 0)),
        ],
        out_specs=[
            pl.BlockSpec((None, tm, D), lambda b, i: (b, i, 0)),
            pl.BlockSpec((None, 2, D), lambda b, i: (b, 0, 0)),
        ],
        out_shape=[
            jax.ShapeDtypeStruct((bsz, t, D), BF16),
            jax.ShapeDtypeStruct((bsz, 2, D), F32),
        ],
        scratch_shapes=[pltpu.VMEM((tm + prev_rows, D), F32)],
        compiler_params=pltpu.CompilerParams(
            dimension_semantics=("arbitrary", "arbitrary"), vmem_limit_bytes=VMEM_LIMIT),
        name="mixer_prompt",
    )(x, norm_mix_g, w_in, conv_a_w, sg_ln_g, sg_w, sg_b_t, kb, vb)


def _ffn_kernel(*refs, final, rows, nsub, prev_rows):
    if final:
        (x_ref, mix_ref, wo_ref, g_ref, wup_ref, cw_ref, cb_ref, wdn_ref, gfin_ref,
         xo_ref, state_ref, y_ref, h_scr) = refs
    else:
        (x_ref, mix_ref, wo_ref, g_ref, wup_ref, cw_ref, cb_ref, wdn_ref,
         xo_ref, state_ref, h_scr) = refs

    @pl.when(pl.program_id(1) == 0)
    def _():
        h_scr[0:prev_rows, :] = jnp.zeros((prev_rows, D_FF2), F32)

    sub = rows // nsub
    subs = [slice(t * sub, (t + 1) * sub) for t in range(nsub)]
    x1 = [x_ref[rs, :] + _dot(mix_ref[rs, :], wo_ref[...]) for rs in subs]
    h = [_dot(_rms(x1[t], g_ref[...]).astype(BF16), wup_ref[...]) for t in range(nsub)]
    for t, rs in enumerate(subs):
        hc = _causal_conv(h_scr, cw_ref, h[t], sub, prev_rows + t * sub) + cb_ref[...]
        x2 = x1[t] + _dot(_silu_gate(hc), wdn_ref[...])
        xo_ref[rs, :] = x2
        if final:
            y_ref[rs, :] = _rms(x2, gfin_ref[...])

    tail = h_scr[rows:rows + prev_rows, :]
    h_scr[0:prev_rows, :] = tail

    @pl.when(pl.program_id(1) == pl.num_programs(1) - 1)
    def _():
        state_ref[...] = tail[prev_rows - 2:, :]


def _ffn_prompt(l, final, x, mix, w_o, norm_ffn_g, w_up, conv_f_w, conv_f_b, w_down, norm_final_g):
    bsz, t, _ = x.shape
    tm, prev_rows = TM_FFN, SUBLANES
    tok = pl.BlockSpec((None, tm, D), lambda b, i: (b, i, 0))
    wspec = lambda shape: _resident((None,) + shape, lambda b, i: (l,) + (0,) * len(shape))
    x_shape = jax.ShapeDtypeStruct((bsz, t, D), F32)
    in_specs = [tok, tok, wspec((D, D)), wspec((1, D)), wspec((D, D_FF2)), wspec((3, D_FF2)),
                wspec((1, D_FF2)), wspec((D_FF, D))]
    args = [x, mix, w_o, norm_ffn_g, w_up, conv_f_w, conv_f_b, w_down]
    out_specs = [tok, pl.BlockSpec((None, 2, D_FF2), lambda b, i: (b, 0, 0))]
    out_shape = [x_shape, jax.ShapeDtypeStruct((bsz, 2, D_FF2), F32)]
    if final:
        in_specs.append(_resident((1, D), lambda b, i: (0, 0)))
        args.append(norm_final_g)
        out_specs.append(tok)
        out_shape.append(x_shape)
    return pl.pallas_call(
        functools.partial(_ffn_kernel, final=final, rows=tm, nsub=NSUB_FFN, prev_rows=prev_rows),
        grid=(bsz, t // tm),
        in_specs=in_specs,
        out_specs=out_specs,
        out_shape=out_shape,
        scratch_shapes=[pltpu.VMEM((tm + prev_rows, D_FF2), F32)],
        compiler_params=pltpu.CompilerParams(
            dimension_semantics=("arbitrary", "arbitrary"), vmem_limit_bytes=VMEM_LIMIT),
        name="ffn_prompt",
    )(*args)


def _put(scr, x):
    scr[SUBLANES:SUBLANES + x.shape[0], :] = x


def _get(scr, rows, k):
    return scr[SUBLANES - k:SUBLANES - k + rows, :]


def _zero_margins(scr, rows):
    scr[0:SUBLANES, :] = jnp.zeros((SUBLANES, scr.shape[1]), F32)
    scr[SUBLANES + rows:, :] = jnp.zeros((SUBLANES, scr.shape[1]), F32)


def _sample_conv(scr, w_ref, cur, hist, steps):
    rows = cur.shape[0]
    t = lax.broadcasted_iota(jnp.int32, (rows, 1), 0) & (steps - 1)
    _put(scr, hist)
    hist_next = _get(scr, rows, -1)
    _put(scr, cur)
    x1 = jnp.where(t == 0, hist_next, _get(scr, rows, 1))
    x2 = jnp.where(t < 2, hist, _get(scr, rows, 2))
    new_hist = _get(scr, rows, 2 - steps)
    return w_ref[0:1, :] * x2 + w_ref[1:2, :] * x1 + w_ref[2:3, :] * cur, new_hist


def _mixer_sample_kernel(x_ref, nm_ref, win_ref, caw_ref, lng_ref, sgc_ref, sgb_ref, hist_ref,
                         mixab_ref, gm_ref, q_ref, vn_ref, state_ref, scr, *, steps):
    rows = x_ref.shape[0]
    _zero_margins(scr, rows)
    z = _rms(x_ref[...], nm_ref[...]).astype(BF16)
    p = {}
    for group in PROJ_GROUPS:
        for i in group:
            p[i] = _dot(z, win_ref[:, i * D:(i + 1) * D])

    vn = _layernorm(p[V], lng_ref[...])
    vn_ref[...] = vn
    tiles = lambda a: a.reshape(rows // SUBLANES, SUBLANES, D)
    _put(scr, vn)
    s3 = sgc_ref[0][None] * tiles(vn) + sgb_ref[...][None]
    for k in range(1, steps):
        s3 = s3 + sgc_ref[k][None] * tiles(_get(scr, rows, k))
    s = s3.reshape(rows, D)

    conv, state = _sample_conv(scr, caw_ref, p[C] * p[H], hist_ref[...], steps)
    state_ref[...] = state
    acc = jax.nn.sigmoid(p[GA]) * (p[B_] * conv)
    mixab_ref[...] = acc + jax.nn.sigmoid(p[GB]) * (p[U] * s)
    gm_ref[...] = jax.nn.sigmoid(p[GM])
    q_ref[...] = p[Q]


def _mixer_sample(l, steps, xs, norm_mix_g, w_in, conv_a_w, sg_ln_g, sgc, sgb, hist_a):
    rows = xs.shape[0]
    tm = rows // 2
    wspec = lambda shape: _resident((None,) + shape, lambda i: (l,) + (0,) * len(shape))
    tok = pl.BlockSpec((tm, D), lambda i: (i, 0))
    out_f = jax.ShapeDtypeStruct((rows, D), F32)
    return pl.pallas_call(
        functools.partial(_mixer_sample_kernel, steps=steps),
        grid=(rows // tm,),
        in_specs=[tok, wspec((1, D)), wspec((D, D_IN)), wspec((3, D)), wspec((1, D)),
                  wspec((steps, SUBLANES, D)), wspec((SUBLANES, D)),
                  pl.BlockSpec((None, tm, D), lambda i: (l, i, 0))],
        out_specs=[tok] * 5,
        out_shape=[out_f] * 5,
        scratch_shapes=[pltpu.VMEM((tm + 2 * SUBLANES, D), F32)],
        compiler_params=pltpu.CompilerParams(
            dimension_semantics=("arbitrary",), vmem_limit_bytes=VMEM_LIMIT),
        name="mixer_sample",
    )(xs, norm_mix_g, w_in, conv_a_w, sg_ln_g, sgc, sgb, hist_a)


def _sample_attn_kernel(q_ref, gm_ref, mixab_ref, k_ref, v_ref, mix_ref, *, steps):
    nb = q_ref.shape[0]
    nq = HEADS * nb
    nk = N_MEM * HEADS
    row = lax.broadcasted_iota(jnp.int32, (nq, nk), 0)
    col = lax.broadcasted_iota(jnp.int32, (nq, nk), 1)
    same_head = (row >> (nb.bit_length() - 1)) == (col & (HEADS - 1))
    row_b = (row & (nb - 1)) >> (steps.bit_length() - 1)
    out_b = (lax.broadcasted_iota(jnp.int32, (nq, 1), 0) & (nb - 1)) >> (steps.bit_length() - 1)
    q = q_ref[...]
    q_all = jnp.concatenate([q[:, h * HEAD_DIM:(h + 1) * HEAD_DIM] for h in range(HEADS)],
                            axis=0).astype(BF16)
    y_all = jnp.zeros((nq, HEAD_DIM), F32)
    for b in range(ATT_BB):
        k2 = k_ref[b].reshape(nk, HEAD_DIM).astype(BF16)
        v2 = v_ref[b].reshape(nk, HEAD_DIM).astype(BF16)
        sc = lax.dot_general(q_all, k2, (((1,), (1,)), ((), ())),
                             preferred_element_type=F32) * (HEAD_DIM ** -0.5)
        sc = jnp.where(same_head & (row_b == b), sc, MASKED)
        e = jnp.exp(sc - jnp.max(sc, axis=-1, keepdims=True))
        p = e * (1.0 / jnp.sum(e, axis=-1, keepdims=True))
        y_all = jnp.where(out_b == b, _dot(p.astype(BF16), v2), y_all)
    y = jnp.concatenate([y_all[h * nb:(h + 1) * nb] for h in range(HEADS)], axis=1)
    mix_ref[...] = mixab_ref[...] + gm_ref[...] * y


def _sample_attn(l, steps, q, gm, mixab, cache_k, cache_v):
    rows = q.shape[0]
    nb = ATT_BB * steps
    assert nb & (nb - 1) == 0 and steps & (steps - 1) == 0
    small = pl.BlockSpec((nb, D), lambda i: (i, 0))
    big = pl.BlockSpec((None, ATT_BB, N_MEM, HEADS, HEAD_DIM), lambda i: (l, i, 0, 0, 0))
    return pl.pallas_call(
        functools.partial(_sample_attn_kernel, steps=steps),
        grid=(rows // nb,),
        in_specs=[small, small, small, big, big],
        out_specs=small,
        out_shape=jax.ShapeDtypeStruct((rows, D), F32),
        compiler_params=pltpu.CompilerParams(
            dimension_semantics=("arbitrary",), vmem_limit_bytes=VMEM_LIMIT),
        name="sample_attn",
    )(q, gm, mixab, cache_k, cache_v)


def _ffn_sample_kernel(*refs, final, steps):
    if final:
        (x_ref, mix_ref, wo_ref, g_ref, wup_ref, cw_ref, cb_ref, wdn_ref, hist_ref, gfin_ref,
         xo_ref, state_ref, y_ref, scr) = refs
    else:
        (x_ref, mix_ref, wo_ref, g_ref, wup_ref, cw_ref, cb_ref, wdn_ref, hist_ref,
         xo_ref, state_ref, scr) = refs
    rows = x_ref.shape[0]
    _zero_margins(scr, rows)
    x1 = x_ref[...] + _dot(mix_ref[...].astype(BF16), wo_ref[...])
    h = _dot(_rms(x1, g_ref[...]).astype(BF16), wup_ref[...])
    conv, state = _sample_conv(scr, cw_ref, h, hist_ref[...], steps)
    state_ref[...] = state
    x2 = x1 + _dot(_silu_gate(conv + cb_ref[...]), wdn_ref[...])
    xo_ref[...] = x2
    if final:
        y_ref[...] = _rms(x2, gfin_ref[...])


def _ffn_sample(l, final, steps, xs, mix, w_o, norm_ffn_g, w_up, conv_f_w, conv_f_b, w_down,
                hist_f, norm_final_g):
    rows = xs.shape[0]
    tm = rows // 4
    tok = pl.BlockSpec((tm, D), lambda i: (i, 0))
    wide = pl.BlockSpec((tm, D_FF2), lambda i: (i, 0))
    wspec = lambda shape: _resident((None,) + shape, lambda i: (l,) + (0,) * len(shape))
    x_shape = jax.ShapeDtypeStruct((rows, D), F32)
    in_specs = [tok, tok, wspec((D, D)), wspec((1, D)), wspec((D, D_FF2)), wspec((3, D_FF2)),
                wspec((1, D_FF2)), wspec((D_FF, D)),
                pl.BlockSpec((None, tm, D_FF2), lambda i: (l, i, 0))]
    args = [xs, mix, w_o, norm_ffn_g, w_up, conv_f_w, conv_f_b, w_down, hist_f]
    out_specs = [tok, wide]
    out_shape = [x_shape, jax.ShapeDtypeStruct((rows, D_FF2), F32)]
    if final:
        in_specs.append(_resident((1, D), lambda i: (0, 0)))
        args.append(norm_final_g)
        out_specs.append(tok)
        out_shape.append(x_shape)
    return pl.pallas_call(
        functools.partial(_ffn_sample_kernel, final=final, steps=steps),
        grid=(rows // tm,),
        in_specs=in_specs,
        out_specs=out_specs,
        out_shape=out_shape,
        scratch_shapes=[pltpu.VMEM((tm + 2 * SUBLANES, D_FF2), F32)],
        compiler_params=pltpu.CompilerParams(
            dimension_semantics=("arbitrary",), vmem_limit_bytes=VMEM_LIMIT),
        name="ffn_sample",
    )(*args)


def _sample_gating_tables(sg_w, sg_b, steps):
    t = jnp.arange(SUBLANES) % steps
    src = t[None, :] - jnp.arange(steps)[:, None]
    coef = sg_w[:, :, t[None, :], jnp.maximum(src, 0)]
    coef = jnp.where(src >= 0, coef, 0.0)
    coef = jnp.repeat(jnp.moveaxis(coef, 1, 3), SG_COLS, axis=3)
    bias = jnp.repeat(jnp.moveaxis(sg_b[:, :, t], 1, 2), SG_COLS, axis=2)
    return coef, bias


def kernel(x_prompt, x_sample, mem_prompt, cache_conv_a, cache_conv_ffn, cache_mem_k, cache_mem_v, norm_mix_g, w_in, conv_a_w, sg_ln_g, sg_w, sg_b, norm_mem_g, w_k, w_v, w_o, norm_ffn_g, w_up, conv_f_w, conv_f_b, w_down, norm_final_g):
    depth = w_in.shape[0]
    bp = x_prompt.shape[0]
    bs, ts, _ = x_sample.shape
    assert SUBLANES % ts == 0 and ts >= 2

    w_in_b = w_in.astype(BF16)
    w_o_b = w_o.astype(BF16)
    w_up_b = w_up.astype(BF16)
    w_down_b = w_down.astype(BF16)
    wkv_b = jnp.concatenate([w_k, w_v], axis=-1).astype(BF16)

    nmg = norm_mix_g[:, None, :]
    lng = sg_ln_g[:, None, :]
    nfg = norm_ffn_g[:, None, :]
    nmemg = norm_mem_g[:, None, :]
    cfb = conv_f_b[:, None, :]
    gfin = norm_final_g[None, :]
    sg_b_t = jnp.swapaxes(sg_b, 1, 2)
    sgc, sgbias = _sample_gating_tables(sg_w, sg_b, ts)
    pad_hist = lambda c: jnp.pad(c, ((0, 0), (0, 0), (0, ts - 2), (0, 0))).reshape(
        depth, bs * ts, c.shape[-1])
    hist_a = pad_hist(cache_conv_a)
    hist_f = pad_hist(cache_conv_ffn)
    xs = x_sample.reshape(bs * ts, D)
    new_hist = lambda a: a.reshape(bs, ts, a.shape[-1])[:, :2]

    k_f, v_f, k_b, v_b = _mem_kv(mem_prompt.reshape(bp * N_MEM, D), nmemg, wkv_b)
    k_b = k_b.reshape(depth * bp, N_MEM, D)
    v_b = v_b.reshape(depth * bp, N_MEM, D)

    xp = x_prompt
    pa, pf, sa, sf, sv = [], [], [], [], []
    yp = ys = None
    for l in range(depth):
        final = l == depth - 1
        mix, a_state = _mixer_prompt(l, xp, nmg, w_in_b, conv_a_w, lng, sg_w, sg_b_t, k_b, v_b)
        outs = _ffn_prompt(l, final, xp, mix, w_o_b, nfg, w_up_b, conv_f_w, cfb, w_down_b, gfin)
        xp, f_state = outs[0], outs[1]
        if final:
            yp = outs[2]
        pa.append(a_state)
        pf.append(f_state)
        mixab, gm, q, vn, a_state_s = _mixer_sample(l, ts, xs, nmg, w_in_b, conv_a_w, lng, sgc,
                                                    sgbias, hist_a)
        mix_s = _sample_attn(l, ts, q, gm, mixab, cache_mem_k, cache_mem_v)
        outs = _ffn_sample(l, final, ts, xs, mix_s, w_o_b, nfg, w_up_b, conv_f_w, cfb, w_down_b,
                           hist_f, gfin)
        xs, f_state_s = outs[0], outs[1]
        if final:
            ys = outs[2]
        sa.append(new_hist(a_state_s))
        sf.append(new_hist(f_state_s))
        sv.append(vn.reshape(bs, ts, D))

    kv_shape = (depth, bp, N_MEM, HEADS, HEAD_DIM)
    return (yp, ys.reshape(bs, ts, D),
            jnp.stack(pa), jnp.stack(pf), k_f.reshape(kv_shape), v_f.reshape(kv_shape),
            jnp.stack(sa), jnp.stack(sf), jnp.stack(sv))
```

```python
import functools

import jax
import jax.numpy as jnp
from jax import lax
from jax.experimental import pallas as pl
from jax.experimental.pallas import tpu as pltpu

F32 = jnp.float32
BF16 = jnp.bfloat16

D = 1024
D_IN = 9 * D
D_FF = 2816
D_FF2 = 2 * D_FF
N_MEM = 256
HEADS = 4
HEAD_DIM = D // HEADS
SG_GROUPS = 4
SG_COLS = D // SG_GROUPS
SG_CHUNK = 128
EPS = 1e-6
H, C, B_, U, V, Q, GA, GB, GM = range(9)
PROJ_GROUPS = [(V, Q, U), (GB, C, H), (B_, GA, GM)]

V7X_VMEM_BYTES = 64 * 1024 * 1024
VMEM_LIMIT = V7X_VMEM_BYTES - 6 * 1024 * 1024
SUBLANES = 8

TM_MIX = 512
NSUB_MIX = 2
TM_FFN = 512
NSUB_FFN = 2
TM_KV = 512
ATT_BB = 4
MASKED = -1e30


def _rms(x, g):
    y = x * lax.rsqrt(jnp.mean(x * x, axis=-1, keepdims=True) + EPS)
    return y * g


def _layernorm(x, g):
    xc = x - jnp.mean(x, axis=-1, keepdims=True)
    return xc * lax.rsqrt(jnp.mean(xc * xc, axis=-1, keepdims=True) + EPS) * g


def _dot(a, b):
    return jnp.dot(a, b, preferred_element_type=F32)


def _resident(shape, index_map):
    return pl.BlockSpec(shape, index_map, pipeline_mode=pl.Buffered(1))


def _causal_conv(scr, w_ref, cur, rows, off):
    scr[off:off + rows, :] = cur
    x2 = scr[off - 2:off - 2 + rows, :]
    x1 = scr[off - 1:off - 1 + rows, :]
    return w_ref[0:1, :] * x2 + w_ref[1:2, :] * x1 + w_ref[2:3, :] * cur


def _silu_gate(hc):
    return (jax.nn.silu(hc[:, D_FF:]) * hc[:, :D_FF]).astype(BF16)


def _kv_kernel(mem_ref, g_ref, wkv_ref, k_ref, v_ref, kb_ref, vb_ref):
    m = _rms(mem_ref[...], g_ref[...]).astype(BF16)
    k = _dot(m, wkv_ref[:, :D])
    v = _dot(m, wkv_ref[:, D:])
    for h in range(HEADS):
        hc = slice(h * HEAD_DIM, (h + 1) * HEAD_DIM)
        k_ref[:, h, :] = k[:, hc]
        v_ref[:, h, :] = v[:, hc]
    kb_ref[...] = k.astype(BF16)
    vb_ref[...] = v.astype(BF16)


def _mem_kv(mem2d, norm_mem_g, wkv):
    depth = wkv.shape[0]
    rows = mem2d.shape[0]
    out_f = jax.ShapeDtypeStruct((depth, rows, HEADS, HEAD_DIM), F32)
    out_b = jax.ShapeDtypeStruct((depth, rows, D), BF16)
    oblk = pl.BlockSpec((None, TM_KV, D), lambda l, i: (l, i, 0))
    oblk_f = pl.BlockSpec((None, TM_KV, HEADS, HEAD_DIM), lambda l, i: (l, i, 0, 0))
    return pl.pallas_call(
        _kv_kernel,
        grid=(depth, rows // TM_KV),
        in_specs=[
            pl.BlockSpec((TM_KV, D), lambda l, i: (i, 0)),
            pl.BlockSpec((None, 1, D), lambda l, i: (l, 0, 0)),
            pl.BlockSpec((None, D, 2 * D), lambda l, i: (l, 0, 0)),
        ],
        out_specs=[oblk_f, oblk_f, oblk, oblk],
        out_shape=[out_f, out_f, out_b, out_b],
        compiler_params=pltpu.CompilerParams(
            dimension_semantics=("arbitrary", "arbitrary"), vmem_limit_bytes=VMEM_LIMIT),
        name="mem_kv",
    )(mem2d, norm_mem_g, wkv)


def _mixer_kernel(x_ref, nm_ref, win_ref, caw_ref, lng_ref, sgw_ref, sgb_ref, kb_ref, vb_ref,
                  mix_ref, state_ref, a_scr, *, rows, nsub, prev_rows):
    @pl.when(pl.program_id(1) == 0)
    def _():
        a_scr[0:prev_rows, :] = jnp.zeros((prev_rows, D), F32)

    ri = lax.broadcasted_iota(jnp.int32, (SG_CHUNK, SG_CHUNK), 0)
    ci = lax.broadcasted_iota(jnp.int32, (SG_CHUNK, SG_CHUNK), 1)
    sgw = [jnp.where(ci <= ri, sgw_ref[g], 0.0).astype(BF16) for g in range(SG_GROUPS)]
    sub = rows // nsub

    def start(t):
        rs = slice(t * sub, (t + 1) * sub)
        return {"t": t, "rs": rs, "z": _rms(x_ref[rs, :], nm_ref[...]).astype(BF16)}

    def project(st, group):
        for i in group:
            st[i] = _dot(st["z"], win_ref[:, i * D:(i + 1) * D])

    def spatial_and_scores(st):
        vnb = _layernorm(st[V], lng_ref[...]).astype(BF16)
        s_rows = []
        for c in range(sub // SG_CHUNK):
            s_cols = []
            for g in range(SG_GROUPS):
                vg = vnb[c * SG_CHUNK:(c + 1) * SG_CHUNK, g * SG_COLS:(g + 1) * SG_COLS]
                s_cols.append(_dot(sgw[g], vg) + sgb_ref[:, g:g + 1])
            s_rows.append(jnp.concatenate(s_cols, axis=1))
        st["s"] = jnp.concatenate(s_rows, axis=0)
        st["sc"] = [
            lax.dot_general(st[Q][:, h * HEAD_DIM:(h + 1) * HEAD_DIM].astype(BF16),
                            kb_ref[:, h * HEAD_DIM:(h + 1) * HEAD_DIM],
                            (((1,), (1,)), ((), ())), preferred_element_type=F32)
            for h in range(HEADS)]

    def attend(st):
        o_cols = []
        for h in range(HEADS):
            sc = st["sc"][h] * (HEAD_DIM ** -0.5)
            e = jnp.exp(sc - jnp.max(sc, axis=-1, keepdims=True))
            p = e * (1.0 / jnp.sum(e, axis=-1, keepdims=True))
            o_cols.append(_dot(p.astype(BF16), vb_ref[:, h * HEAD_DIM:(h + 1) * HEAD_DIM]))
        st["y_m"] = jnp.concatenate(o_cols, axis=1)

    def finish(st):
        a = st[C] * st[H]
        conv = _causal_conv(a_scr, caw_ref, a, sub, prev_rows + st["t"] * sub)
        acc = jax.nn.sigmoid(st[GA]) * (st[B_] * conv)
        acc = acc + jax.nn.sigmoid(st[GB]) * (st[U] * st["s"])
        acc = acc + jax.nn.sigmoid(st[GM]) * st["y_m"]
        mix_ref[st["rs"], :] = acc.astype(BF16)

    tails = [spatial_and_scores, attend, finish]
    prev = None
    for t in range(nsub):
        cur = start(t)
        pending = list(tails) if prev is not None else []
        for group in PROJ_GROUPS:
            project(cur, group)
            if pending:
                pending.pop(0)(prev)
        for fn in pending:
            fn(prev)
        prev = cur
    for fn in tails:
        fn(prev)

    tail = a_scr[rows:rows + prev_rows, :]
    a_scr[0:prev_rows, :] = tail

    @pl.when(pl.program_id(1) == pl.num_programs(1) - 1)
    def _():
        state_ref[...] = tail[prev_rows - 2:, :]


def _mixer_prompt(l, x, norm_mix_g, w_in, conv_a_w, sg_ln_g, sg_w, sg_b_t, kb, vb):
    bsz, t, _ = x.shape
    tm = TM_MIX
    prev_rows = SUBLANES
    wspec = lambda shape: _resident((None,) + shape, lambda b, i: (l,) + (0,) * len(shape))
    return pl.pallas_call(
        functools.partial(_mixer_kernel, rows=tm, nsub=NSUB_MIX, prev_rows=prev_rows),
        grid=(bsz, t // tm),
        in_specs=[
            pl.BlockSpec((None, tm, D), lambda b, i: (b, i, 0)),
            wspec((1, D)),
            wspec((D, D_IN)),
            wspec((3, D)),
            wspec((1, D)),
            wspec((SG_GROUPS, SG_CHUNK, SG_CHUNK)),
            wspec((SG_CHUNK, SG_GROUPS)),
            pl.BlockSpec((None, N_MEM, D), lambda b, i: (l * bsz + b, 0, 0)),
            pl.BlockSpec((None, N_MEM, D), lambda b, i: (l * bsz + b, 0, 0)),
        ],
        out_specs=[
            pl.BlockSpec((None, tm, D), lambda b, i: (b, i, 0)),
            pl.BlockSpec((None, 2, D), lambda b, i: (b, 0, 0)),
        ],
        out_shape=[
            jax.ShapeDtypeStruct((bsz, t, D), BF16),
            jax.ShapeDtypeStruct((bsz, 2, D), F32),
        ],
        scratch_shapes=[pltpu.VMEM((tm + prev_rows, D), F32)],
        compiler_params=pltpu.CompilerParams(
            dimension_semantics=("arbitrary", "arbitrary"), vmem_limit_bytes=VMEM_LIMIT),
        name="mixer_prompt",
    )(x, norm_mix_g, w_in, conv_a_w, sg_ln_g, sg_w, sg_b_t, kb, vb)


def _ffn_kernel(*refs, final, rows, nsub, prev_rows):
    if final:
        (x_ref, mix_ref, wo_ref, g_ref, wup_ref, cw_ref, cb_ref, wdn_ref, gfin_ref,
         xo_ref, state_ref, y_ref, h_scr) = refs
    else:
        (x_ref, mix_ref, wo_ref, g_ref, wup_ref, cw_ref, cb_ref, wdn_ref,
         xo_ref, state_ref, h_scr) = refs

    @pl.when(pl.program_id(1) == 0)
    def _():
        h_scr[0:prev_rows, :] = jnp.zeros((prev_rows, D_FF2), F32)

    sub = rows // nsub
    subs = [slice(t * sub, (t + 1) * sub) for t in range(nsub)]
    x1 = [x_ref[rs, :] + _dot(mix_ref[rs, :], wo_ref[...]) for rs in subs]
    h = [_dot(_rms(x1[t], g_ref[...]).astype(BF16), wup_ref[...]) for t in range(nsub)]
    for t, rs in enumerate(subs):
        hc = _causal_conv(h_scr, cw_ref, h[t], sub, prev_rows + t * sub) + cb_ref[...]
        x2 = x1[t] + _dot(_silu_gate(hc), wdn_ref[...])
        xo_ref[rs, :] = x2
        if final:
            y_ref[rs, :] = _rms(x2, gfin_ref[...])

    tail = h_scr[rows:rows + prev_rows, :]
    h_scr[0:prev_rows, :] = tail

    @pl.when(pl.program_id(1) == pl.num_programs(1) - 1)
    def _():
        state_ref[...] = tail[prev_rows - 2:, :]


def _ffn_prompt(l, final, x, mix, w_o, norm_ffn_g, w_up, conv_f_w, conv_f_b, w_down, norm_final_g):
    bsz, t, _ = x.shape
    tm, prev_rows = TM_FFN, SUBLANES
    tok = pl.BlockSpec((None, tm, D), lambda b, i: (b, i, 0))
    wspec = lambda shape: _resident((None,) + shape, lambda b, i: (l,) + (0,) * len(shape))
    x_shape = jax.ShapeDtypeStruct((bsz, t, D), F32)
    in_specs = [tok, tok, wspec((D, D)), wspec((1, D)), wspec((D, D_FF2)), wspec((3, D_FF2)),
                wspec((1, D_FF2)), wspec((D_FF, D))]
    args = [x, mix, w_o, norm_ffn_g, w_up, conv_f_w, conv_f_b, w_down]
    out_specs = [tok, pl.BlockSpec((None, 2, D_FF2), lambda b, i: (b, 0, 0))]
    out_shape = [x_shape, jax.ShapeDtypeStruct((bsz, 2, D_FF2), F32)]
    if final:
        in_specs.append(_resident((1, D), lambda b, i: (0, 0)))
        args.append(norm_final_g)
        out_specs.append(tok)
        out_shape.append(x_shape)
    return pl.pallas_call(
        functools.partial(_ffn_kernel, final=final, rows=tm, nsub=NSUB_FFN, prev_rows=prev_rows),
        grid=(bsz, t // tm),
        in_specs=in_specs,
        out_specs=out_specs,
        out_shape=out_shape,
        scratch_shapes=[pltpu.VMEM((tm + prev_rows, D_FF2), F32)],
        compiler_params=pltpu.CompilerParams(
            dimension_semantics=("arbitrary", "arbitrary"), vmem_limit_bytes=VMEM_LIMIT),
        name="ffn_prompt",
    )(*args)


SAMPLE_STEPS = 4


def _put(scr, x):
    scr[SUBLANES:SUBLANES + x.shape[0], :] = x


def _get(scr, rows, k):
    return scr[SUBLANES - k:SUBLANES - k + rows, :]


def _zero_margins(scr, rows):
    scr[0:SUBLANES, :] = jnp.zeros((SUBLANES, scr.shape[1]), F32)
    scr[SUBLANES + rows:, :] = jnp.zeros((SUBLANES, scr.shape[1]), F32)


def _sample_conv(scr, w_ref, cur0, cur1, hist):
    rows = cur0.shape[0]
    first = (lax.broadcasted_iota(jnp.int32, (rows, 1), 0) & 1) == 0
    _put(scr, hist)
    hist_up = _get(scr, rows, -1)
    _put(scr, cur0)
    cur0_dn = _get(scr, rows, 1)
    cur0_up = _get(scr, rows, -1)
    _put(scr, cur1)
    cur1_dn = _get(scr, rows, 1)
    w0, w1, w2 = w_ref[0:1, :], w_ref[1:2, :], w_ref[2:3, :]
    c0 = w0 * hist + w1 * jnp.where(first, hist_up, cur0_dn) + w2 * cur0
    c1 = w0 * cur0 + w1 * jnp.where(first, cur0_up, cur1_dn) + w2 * cur1
    return c0, c1


def _mixer_sample_kernel(x_ref, nm_ref, win_ref, caw_ref, lng_ref, sgc_ref, sgb_ref, hist_ref,
                         mixab_ref, gm_ref, q_ref, vn_ref, state_ref, scr):
    rows = x_ref.shape[1]
    halves = lambda a: (a[:rows], a[rows:])
    _zero_margins(scr, rows)
    z = _rms(x_ref[...].reshape(2 * rows, D), nm_ref[...]).astype(BF16)
    p = {}
    for group in PROJ_GROUPS:
        for i in group:
            p[i] = _dot(z, win_ref[:, i * D:(i + 1) * D])

    vn0, vn1 = halves(_layernorm(p[V], lng_ref[...]))
    vn_ref[0] = vn0
    vn_ref[1] = vn1
    tiles = lambda a: a.reshape(rows // SUBLANES, SUBLANES, D)
    coef = lambda k: sgc_ref[k][None]
    _put(scr, vn0)
    vn0_dn, vn0_up = tiles(_get(scr, rows, 1)), tiles(_get(scr, rows, -1))
    _put(scr, vn1)
    vn1_dn = tiles(_get(scr, rows, 1))
    s0 = coef(0) * tiles(vn0) + coef(1) * vn0_dn + sgb_ref[0][None]
    s1 = (coef(2) * tiles(vn0) + coef(3) * vn0_up + coef(4) * vn0_dn
          + coef(5) * tiles(vn1) + coef(6) * vn1_dn + sgb_ref[1][None])
    s = jnp.concatenate([s0.reshape(rows, D), s1.reshape(rows, D)], axis=0)

    a0, a1 = halves(p[C] * p[H])
    hist = hist_ref[...].reshape(rows, D)
    conv = jnp.concatenate(_sample_conv(scr, caw_ref, a0, a1, hist), axis=0)
    state_ref[...] = a1.reshape(state_ref.shape)
    acc = jax.nn.sigmoid(p[GA]) * (p[B_] * conv)
    acc = acc + jax.nn.sigmoid(p[GB]) * (p[U] * s)
    mixab_ref[...] = acc.reshape(2, rows, D)
    gm_ref[...] = jax.nn.sigmoid(p[GM]).reshape(2, rows, D)
    q_ref[...] = p[Q].reshape(2, rows, D)


def _mixer_sample(l, xs, norm_mix_g, w_in, conv_a_w, sg_ln_g, sgc, sgb, cache_a):
    rows = xs.shape[1]
    tm = rows // 2
    wspec = lambda shape: _resident((None,) + shape, lambda i: (l,) + (0,) * len(shape))
    tok = pl.BlockSpec((2, tm, D), lambda i: (0, i, 0))
    hist = pl.BlockSpec((None, tm // 2, 2, D), lambda i: (l, i, 0, 0))
    out_f = jax.ShapeDtypeStruct((2, rows, D), F32)
    return pl.pallas_call(
        _mixer_sample_kernel,
        grid=(rows // tm,),
        in_specs=[tok, wspec((1, D)), wspec((D, D_IN)), wspec((3, D)), wspec((1, D)),
                  wspec((7, SUBLANES, D)), wspec((2, SUBLANES, D)), hist],
        out_specs=[tok] * 4 + [pl.BlockSpec((tm // 2, 2, D), lambda i: (i, 0, 0))],
        out_shape=[out_f] * 4 + [jax.ShapeDtypeStruct((rows // 2, 2, D), F32)],
        scratch_shapes=[pltpu.VMEM((tm + 2 * SUBLANES, D), F32)],
        compiler_params=pltpu.CompilerParams(
            dimension_semantics=("arbitrary",), vmem_limit_bytes=VMEM_LIMIT),
        name="mixer_sample",
    )(xs, norm_mix_g, w_in, conv_a_w, sg_ln_g, sgc, sgb, cache_a)


def _sample_attn_kernel(q_ref, gm_ref, mixab_ref, k_ref, v_ref, mix_ref):
    hb = q_ref.shape[1]
    nb = 2 * hb
    nq = HEADS * nb
    nk = N_MEM * HEADS
    row = lax.broadcasted_iota(jnp.int32, (nq, nk), 0)
    col = lax.broadcasted_iota(jnp.int32, (nq, nk), 1)
    same_head = (row >> (nb.bit_length() - 1)) == (col & (HEADS - 1))
    row_b = (row & (hb - 1)) >> 1
    out_b = (lax.broadcasted_iota(jnp.int32, (nq, 1), 0) & (hb - 1)) >> 1
    q = q_ref[...].reshape(nb, D)
    q_all = jnp.concatenate([q[:, h * HEAD_DIM:(h + 1) * HEAD_DIM] for h in range(HEADS)],
                            axis=0).astype(BF16)
    y_all = jnp.zeros((nq, HEAD_DIM), F32)
    for b in range(ATT_BB):
        k2 = k_ref[b].reshape(nk, HEAD_DIM).astype(BF16)
        v2 = v_ref[b].reshape(nk, HEAD_DIM).astype(BF16)
        sc = lax.dot_general(q_all, k2, (((1,), (1,)), ((), ())),
                             preferred_element_type=F32) * (HEAD_DIM ** -0.5)
        sc = jnp.where(same_head & (row_b == b), sc, MASKED)
        e = jnp.exp(sc - jnp.max(sc, axis=-1, keepdims=True))
        p = e * (1.0 / jnp.sum(e, axis=-1, keepdims=True))
        y_all = jnp.where(out_b == b, _dot(p.astype(BF16), v2), y_all)
    y = jnp.concatenate([y_all[h * nb:(h + 1) * nb] for h in range(HEADS)], axis=1)
    mix_ref[...] = mixab_ref[...] + gm_ref[...] * y.reshape(2, hb, D)


def _sample_attn(l, q, gm, mixab, cache_k, cache_v):
    rows = q.shape[1]
    hb = 2 * ATT_BB
    assert hb & (hb - 1) == 0
    small = pl.BlockSpec((2, hb, D), lambda i: (0, i, 0))
    big = pl.BlockSpec((None, ATT_BB, N_MEM, HEADS, HEAD_DIM), lambda i: (l, i, 0, 0, 0))
    return pl.pallas_call(
        _sample_attn_kernel,
        grid=(rows // hb,),
        in_specs=[small, small, small, big, big],
        out_specs=small,
        out_shape=jax.ShapeDtypeStruct((2, rows, D), F32),
        compiler_params=pltpu.CompilerParams(
            dimension_semantics=("arbitrary",), vmem_limit_bytes=VMEM_LIMIT),
        name="sample_attn",
    )(q, gm, mixab, cache_k, cache_v)


def _ffn_sample_kernel(*refs, final):
    if final:
        (x_ref, mix_ref, wo_ref, g_ref, wup_ref, cw_ref, cb_ref, wdn_ref, hist_ref, gfin_ref,
         xo_ref, state_ref, y_ref, scr) = refs
    else:
        (x_ref, mix_ref, wo_ref, g_ref, wup_ref, cw_ref, cb_ref, wdn_ref, hist_ref,
         xo_ref, state_ref, scr) = refs
    rows = x_ref.shape[1]
    _zero_margins(scr, rows)
    x1 = x_ref[...].reshape(2 * rows, D) + _dot(
        mix_ref[...].reshape(2 * rows, D).astype(BF16), wo_ref[...])
    h = _dot(_rms(x1, g_ref[...]).astype(BF16), wup_ref[...])
    h0, h1 = h[:rows], h[rows:]
    hist = hist_ref[...].reshape(rows, D_FF2)
    conv = jnp.concatenate(_sample_conv(scr, cw_ref, h0, h1, hist), axis=0)
    state_ref[...] = h1.reshape(state_ref.shape)
    x2 = x1 + _dot(_silu_gate(conv + cb_ref[...]), wdn_ref[...])
    xo_ref[...] = x2.reshape(2, rows, D)
    if final:
        y_ref[...] = _rms(x2, gfin_ref[...]).reshape(2, rows, D)


def _ffn_sample(l, final, xs, mix, w_o, norm_ffn_g, w_up, conv_f_w, conv_f_b, w_down, cache_f,
                norm_final_g):
    rows = xs.shape[1]
    tm = rows // 2
    tok = pl.BlockSpec((2, tm, D), lambda i: (0, i, 0))
    wspec = lambda shape: _resident((None,) + shape, lambda i: (l,) + (0,) * len(shape))
    x_shape = jax.ShapeDtypeStruct((2, rows, D), F32)
    in_specs = [tok, tok, wspec((D, D)), wspec((1, D)), wspec((D, D_FF2)), wspec((3, D_FF2)),
                wspec((1, D_FF2)), wspec((D_FF, D)),
                pl.BlockSpec((None, tm // 2, 2, D_FF2), lambda i: (l, i, 0, 0))]
    args = [xs, mix, w_o, norm_ffn_g, w_up, conv_f_w, conv_f_b, w_down, cache_f]
    out_specs = [tok, pl.BlockSpec((tm // 2, 2, D_FF2), lambda i: (i, 0, 0))]
    out_shape = [x_shape, jax.ShapeDtypeStruct((rows // 2, 2, D_FF2), F32)]
    if final:
        in_specs.append(_resident((1, D), lambda i: (0, 0)))
        args.append(norm_final_g)
        out_specs.append(tok)
        out_shape.append(x_shape)
    return pl.pallas_call(
        functools.partial(_ffn_sample_kernel, final=final),
        grid=(rows // tm,),
        in_specs=in_specs,
        out_specs=out_specs,
        out_shape=out_shape,
        scratch_shapes=[pltpu.VMEM((tm + 2 * SUBLANES, D_FF2), F32)],
        compiler_params=pltpu.CompilerParams(
            dimension_semantics=("arbitrary",), vmem_limit_bytes=VMEM_LIMIT),
        name="ffn_sample",
    )(*args)


def _sample_gating_tables(sg_w, sg_b):
    w = sg_w[:, :, :SAMPLE_STEPS, :SAMPLE_STEPS]
    zero = jnp.zeros_like(w[..., 0, 0])
    by_parity = [(w[..., 0, 0], w[..., 1, 1]), (zero, w[..., 1, 0]), (w[..., 2, 0], w[..., 3, 1]),
                 (w[..., 2, 1], zero), (zero, w[..., 3, 0]), (w[..., 2, 2], w[..., 3, 3]),
                 (zero, w[..., 3, 2])]

    def table(pairs):
        t = jnp.stack([jnp.stack(p, axis=1) for p in pairs], axis=1)
        t = jnp.tile(t, (1, 1, SUBLANES // 2, 1))
        return jnp.repeat(t, SG_COLS, axis=3)

    b = sg_b[:, :, :SAMPLE_STEPS]
    return table(by_parity), table([(b[..., 0], b[..., 1]), (b[..., 2], b[..., 3])])


def kernel(x_prompt, x_sample, mem_prompt, cache_conv_a, cache_conv_ffn, cache_mem_k, cache_mem_v, norm_mix_g, w_in, conv_a_w, sg_ln_g, sg_w, sg_b, norm_mem_g, w_k, w_v, w_o, norm_ffn_g, w_up, conv_f_w, conv_f_b, w_down, norm_final_g):
    depth = w_in.shape[0]
    bp = x_prompt.shape[0]
    bs, ts, _ = x_sample.shape
    assert ts == SAMPLE_STEPS

    w_in_b = w_in.astype(BF16)
    w_o_b = w_o.astype(BF16)
    w_up_b = w_up.astype(BF16)
    w_down_b = w_down.astype(BF16)
    wkv_b = jnp.concatenate([w_k, w_v], axis=-1).astype(BF16)

    nmg = norm_mix_g[:, None, :]
    lng = sg_ln_g[:, None, :]
    nfg = norm_ffn_g[:, None, :]
    nmemg = norm_mem_g[:, None, :]
    cfb = conv_f_b[:, None, :]
    gfin = norm_final_g[None, :]
    sg_b_t = jnp.swapaxes(sg_b, 1, 2)
    sgc, sgbias = _sample_gating_tables(sg_w, sg_b)
    to_halves = lambda a: jnp.swapaxes(a.reshape(bs, 2, 2, D), 0, 1).reshape(2, bs * 2, D)
    from_halves = lambda a: jnp.swapaxes(a.reshape(2, bs, 2, D), 0, 1).reshape(bs, ts, D)
    xs = to_halves(x_sample)

    k_f, v_f, k_b, v_b = _mem_kv(mem_prompt.reshape(bp * N_MEM, D), nmemg, wkv_b)
    k_b = k_b.reshape(depth * bp, N_MEM, D)
    v_b = v_b.reshape(depth * bp, N_MEM, D)

    xp = x_prompt
    pa, pf, sa, sf, sv = [], [], [], [], []
    yp = ys = None
    for l in range(depth):
        final = l == depth - 1
        mix, a_state = _mixer_prompt(l, xp, nmg, w_in_b, conv_a_w, lng, sg_w, sg_b_t, k_b, v_b)
        outs = _ffn_prompt(l, final, xp, mix, w_o_b, nfg, w_up_b, conv_f_w, cfb, w_down_b, gfin)
        xp, f_state = outs[0], outs[1]
        if final:
            yp = outs[2]
        pa.append(a_state)
        pf.append(f_state)
        mixab, gm, q, vn, a_state_s = _mixer_sample(l, xs, nmg, w_in_b, conv_a_w, lng, sgc,
                                                    sgbias, cache_conv_a)
        mix_s = _sample_attn(l, q, gm, mixab, cache_mem_k, cache_mem_v)
        outs = _ffn_sample(l, final, xs, mix_s, w_o_b, nfg, w_up_b, conv_f_w, cfb, w_down_b,
                           cache_conv_ffn, gfin)
        xs, f_state_s = outs[0], outs[1]
        if final:
            ys = outs[2]
        sa.append(a_state_s)
        sf.append(f_state_s)
        sv.append(from_halves(vn))

    kv_shape = (depth, bp, N_MEM, HEADS, HEAD_DIM)
    return (yp, from_halves(ys),
            jnp.stack(pa), jnp.stack(pf), k_f.reshape(kv_shape), v_f.reshape(kv_shape),
            jnp.stack(sa), jnp.stack(sf), jnp.stack(sv))
```

```python
import functools

import jax
import jax.numpy as jnp
from jax import lax
from jax.experimental import pallas as pl
from jax.experimental.pallas import tpu as pltpu

F32 = jnp.float32
BF16 = jnp.bfloat16

D = 1024
D_IN = 9 * D
D_FF = 2816
D_FF2 = 2 * D_FF
N_MEM = 256
HEADS = 4
HEAD_DIM = D // HEADS
SG_GROUPS = 4
SG_COLS = D // SG_GROUPS
SG_CHUNK = 128
EPS = 1e-6
H, C, B_, U, V, Q, GA, GB, GM = range(9)
PROJ_GROUPS = [(V, Q, U), (GB, C, H), (B_, GA, GM)]

V7X_VMEM_BYTES = 64 * 1024 * 1024
VMEM_LIMIT = V7X_VMEM_BYTES - 4 * 1024 * 1024
SUBLANES = 8

SUBS_MIX = (256, 256)
SUBS_FFN = (256, 256)
TM_KV = 512
ATT_BB = 4
MASKED = -1e30


def _rms(x, g):
    y = x * lax.rsqrt(jnp.mean(x * x, axis=-1, keepdims=True) + EPS)
    return y * g


def _layernorm(x, g):
    xc = x - jnp.mean(x, axis=-1, keepdims=True)
    return xc * lax.rsqrt(jnp.mean(xc * xc, axis=-1, keepdims=True) + EPS) * g


def _dot(a, b):
    return jnp.dot(a, b, preferred_element_type=F32)


def _resident(shape, index_map):
    return pl.BlockSpec(shape, index_map, pipeline_mode=pl.Buffered(1))


def _causal_conv(w_ref, hist, cur):
    n = hist.shape[0]
    ext = jnp.concatenate([hist, cur], axis=0)
    x1 = pltpu.roll(ext, 1, axis=0)[n:, :]
    x2 = pltpu.roll(ext, 2, axis=0)[n:, :]
    return w_ref[0:1, :] * x2 + w_ref[1:2, :] * x1 + w_ref[2:3, :] * cur


def _silu_gate(hc):
    return (jax.nn.silu(hc[:, D_FF:]) * hc[:, :D_FF]).astype(BF16)


def _kv_kernel(mem_ref, g_ref, wkv_ref, k_ref, v_ref, kb_ref, vb_ref):
    m = _rms(mem_ref[...], g_ref[...]).astype(BF16)
    k = _dot(m, wkv_ref[:, :D])
    v = _dot(m, wkv_ref[:, D:])
    for h in range(HEADS):
        hc = slice(h * HEAD_DIM, (h + 1) * HEAD_DIM)
        k_ref[:, h, :] = k[:, hc]
        v_ref[:, h, :] = v[:, hc]
    kb_ref[...] = k.astype(BF16)
    vb_ref[...] = v.astype(BF16)


def _mem_kv(mem2d, norm_mem_g, wkv):
    depth = wkv.shape[0]
    rows = mem2d.shape[0]
    out_f = jax.ShapeDtypeStruct((depth, rows, HEADS, HEAD_DIM), F32)
    out_b = jax.ShapeDtypeStruct((depth, rows, D), BF16)
    oblk = pl.BlockSpec((None, TM_KV, D), lambda l, i: (l, i, 0))
    oblk_f = pl.BlockSpec((None, TM_KV, HEADS, HEAD_DIM), lambda l, i: (l, i, 0, 0))
    return pl.pallas_call(
        _kv_kernel,
        grid=(depth, rows // TM_KV),
        in_specs=[
            pl.BlockSpec((TM_KV, D), lambda l, i: (i, 0)),
            pl.BlockSpec((None, 1, D), lambda l, i: (l, 0, 0)),
            pl.BlockSpec((None, D, 2 * D), lambda l, i: (l, 0, 0)),
        ],
        out_specs=[oblk_f, oblk_f, oblk, oblk],
        out_shape=[out_f, out_f, out_b, out_b],
        compiler_params=pltpu.CompilerParams(
            dimension_semantics=("arbitrary", "arbitrary"), vmem_limit_bytes=VMEM_LIMIT),
        name="mem_kv",
    )(mem2d, norm_mem_g, wkv)


def _mixer_kernel(x_ref, nm_ref, win_ref, caw_ref, lng_ref, sgw_ref, sgb_ref, kb_ref, vb_ref,
                  mix_ref, state_ref, a_scr, *, subs, prev_rows):
    @pl.when(pl.program_id(1) == 0)
    def _():
        a_scr[...] = jnp.zeros((prev_rows, D), F32)

    ri = lax.broadcasted_iota(jnp.int32, (SG_CHUNK, SG_CHUNK), 0)
    ci = lax.broadcasted_iota(jnp.int32, (SG_CHUNK, SG_CHUNK), 1)
    sgw = [jnp.where(ci <= ri, sgw_ref[g], 0.0).astype(BF16) for g in range(SG_GROUPS)]
    offs = [sum(subs[:t]) for t in range(len(subs))]
    hist = [a_scr[...]]

    def start(t):
        rs = slice(offs[t], offs[t] + subs[t])
        return {"t": t, "rs": rs, "z": _rms(x_ref[rs, :], nm_ref[...]).astype(BF16)}

    def project(st, group):
        for i in group:
            st[i] = _dot(st["z"], win_ref[:, i * D:(i + 1) * D])

    def spatial_and_scores(st):
        vnb = _layernorm(st[V], lng_ref[...]).astype(BF16)
        s_rows = []
        for c in range(subs[st["t"]] // SG_CHUNK):
            s_cols = []
            for g in range(SG_GROUPS):
                vg = vnb[c * SG_CHUNK:(c + 1) * SG_CHUNK, g * SG_COLS:(g + 1) * SG_COLS]
                s_cols.append(_dot(sgw[g], vg) + sgb_ref[:, g:g + 1])
            s_rows.append(jnp.concatenate(s_cols, axis=1))
        st["s"] = jnp.concatenate(s_rows, axis=0)
        st["sc"] = [
            lax.dot_general(st[Q][:, h * HEAD_DIM:(h + 1) * HEAD_DIM].astype(BF16),
                            kb_ref[:, h * HEAD_DIM:(h + 1) * HEAD_DIM],
                            (((1,), (1,)), ((), ())), preferred_element_type=F32)
            for h in range(HEADS)]

    def attend(st):
        o_cols = []
        for h in range(HEADS):
            sc = st["sc"][h] * (HEAD_DIM ** -0.5)
            e = jnp.exp(sc - jnp.max(sc, axis=-1, keepdims=True))
            p = e * (1.0 / jnp.sum(e, axis=-1, keepdims=True))
            o_cols.append(_dot(p.astype(BF16), vb_ref[:, h * HEAD_DIM:(h + 1) * HEAD_DIM]))
        st["y_m"] = jnp.concatenate(o_cols, axis=1)

    def gate(st):
        a = st.pop(C) * st.pop(H)
        conv = _causal_conv(caw_ref, hist[0], a)
        hist[0] = a[subs[st["t"]] - prev_rows:, :]
        st["acc"] = jax.nn.sigmoid(st.pop(GA)) * (st.pop(B_) * conv)
        st["ub"] = jax.nn.sigmoid(st.pop(GB)) * st.pop(U)
        st["gm"] = jax.nn.sigmoid(st.pop(GM))

    def finish(st):
        acc = st["acc"] + st["ub"] * st["s"] + st["gm"] * st["y_m"]
        mix_ref[st["rs"], :] = acc.astype(BF16)

    tails = [spatial_and_scores, attend, finish]
    prev = None
    for t in range(len(subs)):
        cur = start(t)
        pending = list(tails) if prev is not None else []
        for group in PROJ_GROUPS:
            project(cur, group)
            if pending:
                pending.pop(0)(prev)
        for fn in pending:
            fn(prev)
        gate(cur)
        prev = cur
    for fn in tails:
        fn(prev)

    a_scr[...] = hist[0]

    @pl.when(pl.program_id(1) == pl.num_programs(1) - 1)
    def _():
        state_ref[...] = hist[0][prev_rows - 2:, :]


def _mixer_prompt(l, x, norm_mix_g, w_in, conv_a_w, sg_ln_g, sg_w, sg_b_t, kb, vb):
    bsz, t, _ = x.shape
    tm = sum(SUBS_MIX)
    prev_rows = SUBLANES
    wspec = lambda shape: _resident((None,) + shape, lambda b, i: (l,) + (0,) * len(shape))
    return pl.pallas_call(
        functools.partial(_mixer_kernel, subs=SUBS_MIX, prev_rows=prev_rows),
        grid=(bsz, t // tm),
        in_specs=[
            pl.BlockSpec((None, tm, D), lambda b, i: (b, i, 0)),
            wspec((1, D)),
            wspec((D, D_IN)),
            wspec((3, D)),
            wspec((1, D)),
            wspec((SG_GROUPS, SG_CHUNK, SG_CHUNK)),
            wspec((SG_CHUNK, SG_GROUPS)),
            pl.BlockSpec((None, N_MEM, D), lambda b, i: (l * bsz + b, 0, 0)),
            pl.BlockSpec((None, N_MEM, D), lambda b, i: (l * bsz + b, 0, 0)),
        ],
        out_specs=[
            pl.BlockSpec((None, tm, D), lambda b, i: (b, i, 0)),
            pl.BlockSpec((None, 2, D), lambda b, i: (b, 0, 0)),
        ],
        out_shape=[
            jax.ShapeDtypeStruct((bsz, t, D), BF16),
            jax.ShapeDtypeStruct((bsz, 2, D), F32),
        ],
        scratch_shapes=[pltpu.VMEM((prev_rows, D), F32)],
        compiler_params=pltpu.CompilerParams(
            dimension_semantics=("arbitrary", "arbitrary"), vmem_limit_bytes=VMEM_LIMIT),
        name="mixer_prompt",
    )(x, norm_mix_g, w_in, conv_a_w, sg_ln_g, sg_w, sg_b_t, kb, vb)


def _attn_scores(q_ref, k_ref):
    hb = q_ref.shape[1]
    nb = 2 * hb
    nq = HEADS * nb
    nk = N_MEM * HEADS
    row = lax.broadcasted_iota(jnp.int32, (nq, nk), 0)
    col = lax.broadcasted_iota(jnp.int32, (nq, nk), 1)
    same_head = (row >> (nb.bit_length() - 1)) == (col & (HEADS - 1))
    row_b = (row & (hb - 1)) >> 1
    q = q_ref[...].reshape(nb, D)
    q_all = jnp.concatenate([q[:, h * HEAD_DIM:(h + 1) * HEAD_DIM] for h in range(HEADS)],
                            axis=0).astype(BF16)
    scores = []
    for b in range(ATT_BB):
        k2 = k_ref[b].reshape(nk, HEAD_DIM).astype(BF16)
        sc = lax.dot_general(q_all, k2, (((1,), (1,)), ((), ())),
                             preferred_element_type=F32) * (HEAD_DIM ** -0.5)
        scores.append(jnp.where(same_head & (row_b == b), sc, MASKED))
    return scores


def _attn_values(scores, v_ref, hb):
    nb = 2 * hb
    nq = HEADS * nb
    out_b = (lax.broadcasted_iota(jnp.int32, (nq, 1), 0) & (hb - 1)) >> 1
    y_all = jnp.zeros((nq, HEAD_DIM), F32)
    for b, sc in enumerate(scores):
        v2 = v_ref[b].reshape(N_MEM * HEADS, HEAD_DIM).astype(BF16)
        e = jnp.exp(sc - jnp.max(sc, axis=-1, keepdims=True))
        p = e * (1.0 / jnp.sum(e, axis=-1, keepdims=True))
        y_all = jnp.where(out_b == b, _dot(p.astype(BF16), v2), y_all)
    y = jnp.concatenate([y_all[h * nb:(h + 1) * nb] for h in range(HEADS)], axis=1)
    return y.reshape(2, hb, D)


def _ffn_kernel(*refs, final, subs, prev_rows):
    (x_ref, mix_ref, wo_ref, g_ref, wup_ref, cw_ref, cb_ref, wdn_ref,
     sq_ref, sgm_ref, smixab_ref, sk_ref, sv_ref) = refs[:13]
    if final:
        gfin_ref, xo_ref, state_ref, smix_ref, y_ref, h_scr = refs[13:]
    else:
        xo_ref, state_ref, smix_ref, h_scr = refs[13:]

    @pl.when(pl.program_id(1) == 0)
    def _():
        h_scr[...] = jnp.zeros((prev_rows, D_FF2), F32)

    offs = [sum(subs[:t]) for t in range(len(subs))]
    slices = [slice(o, o + n) for o, n in zip(offs, subs)]
    x1 = [x_ref[rs, :] + _dot(mix_ref[rs, :], wo_ref[...]) for rs in slices]
    scores = _attn_scores(sq_ref, sk_ref)
    h = []
    for t in range(len(subs)):
        h.append(_dot(_rms(x1[t], g_ref[...]).astype(BF16), wup_ref[...]))
        if t == 0:
            smix_ref[...] = smixab_ref[...] + sgm_ref[...] * _attn_values(
                scores, sv_ref, sq_ref.shape[1])
    hist = h_scr[...]
    for t, rs in enumerate(slices):
        hc = _causal_conv(cw_ref, hist, h[t]) + cb_ref[...]
        hist = h[t][subs[t] - prev_rows:, :]
        x2 = x1[t] + _dot(_silu_gate(hc), wdn_ref[...])
        xo_ref[rs, :] = x2
        if final:
            y_ref[rs, :] = _rms(x2, gfin_ref[...])
    h_scr[...] = hist

    @pl.when(pl.program_id(1) == pl.num_programs(1) - 1)
    def _():
        state_ref[...] = hist[prev_rows - 2:, :]


def _ffn_prompt(l, final, x, mix, w_o, norm_ffn_g, w_up, conv_f_w, conv_f_b, w_down, norm_final_g,
                sq, sgm, smixab, cache_k, cache_v):
    bsz, t, _ = x.shape
    tm, prev_rows = sum(SUBS_FFN), SUBLANES
    n_t = t // tm
    hb = 2 * ATT_BB
    assert hb & (hb - 1) == 0 and sq.shape[1] == hb * bsz * n_t
    tok = pl.BlockSpec((None, tm, D), lambda b, i: (b, i, 0))
    wspec = lambda shape: _resident((None,) + shape, lambda b, i: (l,) + (0,) * len(shape))
    small = pl.BlockSpec((2, hb, D), lambda b, i: (0, b * n_t + i, 0))
    big = pl.BlockSpec((None, ATT_BB, N_MEM, HEADS, HEAD_DIM),
                       lambda b, i: (l, b * n_t + i, 0, 0, 0))
    x_shape = jax.ShapeDtypeStruct((bsz, t, D), F32)
    in_specs = [tok, tok, wspec((D, D)), wspec((1, D)), wspec((D, D_FF2)), wspec((3, D_FF2)),
                wspec((1, D_FF2)), wspec((D_FF, D)), small, small, small, big, big]
    args = [x, mix, w_o, norm_ffn_g, w_up, conv_f_w, conv_f_b, w_down, sq, sgm, smixab,
            cache_k, cache_v]
    out_specs = [tok, pl.BlockSpec((None, 2, D_FF2), lambda b, i: (b, 0, 0)), small]
    out_shape = [x_shape, jax.ShapeDtypeStruct((bsz, 2, D_FF2), F32),
                 jax.ShapeDtypeStruct(sq.shape, F32)]
    if final:
        in_specs.append(_resident((1, D), lambda b, i: (0, 0)))
        args.append(norm_final_g)
        out_specs.append(tok)
        out_shape.append(x_shape)
    return pl.pallas_call(
        functools.partial(_ffn_kernel, final=final, subs=SUBS_FFN, prev_rows=prev_rows),
        grid=(bsz, t // tm),
        in_specs=in_specs,
        out_specs=out_specs,
        out_shape=out_shape,
        scratch_shapes=[pltpu.VMEM((prev_rows, D_FF2), F32)],
        compiler_params=pltpu.CompilerParams(
            dimension_semantics=("arbitrary", "arbitrary"), vmem_limit_bytes=VMEM_LIMIT),
        name="ffn_prompt",
    )(*args)


SAMPLE_STEPS = 4


def _put(scr, x):
    scr[SUBLANES:SUBLANES + x.shape[0], :] = x


def _get(scr, rows, k):
    return scr[SUBLANES - k:SUBLANES - k + rows, :]


def _zero_margins(scr, rows):
    scr[0:SUBLANES, :] = jnp.zeros((SUBLANES, scr.shape[1]), F32)
    scr[SUBLANES + rows:, :] = jnp.zeros((SUBLANES, scr.shape[1]), F32)


def _sample_conv(scr, w_ref, cur0, cur1, hist):
    rows = cur0.shape[0]
    first = (lax.broadcasted_iota(jnp.int32, (rows, 1), 0) & 1) == 0
    _put(scr, hist)
    hist_up = _get(scr, rows, -1)
    _put(scr, cur0)
    cur0_dn = _get(scr, rows, 1)
    cur0_up = _get(scr, rows, -1)
    _put(scr, cur1)
    cur1_dn = _get(scr, rows, 1)
    w0, w1, w2 = w_ref[0:1, :], w_ref[1:2, :], w_ref[2:3, :]
    c0 = w0 * hist + w1 * jnp.where(first, hist_up, cur0_dn) + w2 * cur0
    c1 = w0 * cur0 + w1 * jnp.where(first, cur0_up, cur1_dn) + w2 * cur1
    return c0, c1


def _mixer_sample_kernel(x_ref, nm_ref, win_ref, caw_ref, lng_ref, sgc_ref, sgb_ref, hist_ref,
                         mixab_ref, gm_ref, q_ref, vn_ref, state_ref, scr):
    rows = x_ref.shape[1]
    halves = lambda a: (a[:rows], a[rows:])
    _zero_margins(scr, rows)
    z = _rms(x_ref[...].reshape(2 * rows, D), nm_ref[...]).astype(BF16)
    p = {}
    for group in PROJ_GROUPS:
        for i in group:
            p[i] = _dot(z, win_ref[:, i * D:(i + 1) * D])

    vn0, vn1 = halves(_layernorm(p[V], lng_ref[...]))
    vn_ref[0] = vn0
    vn_ref[1] = vn1
    tiles = lambda a: a.reshape(rows // SUBLANES, SUBLANES, D)
    coef = lambda k: sgc_ref[k][None]
    _put(scr, vn0)
    vn0_dn, vn0_up = tiles(_get(scr, rows, 1)), tiles(_get(scr, rows, -1))
    _put(scr, vn1)
    vn1_dn = tiles(_get(scr, rows, 1))
    s0 = coef(0) * tiles(vn0) + coef(1) * vn0_dn + sgb_ref[0][None]
    s1 = (coef(2) * tiles(vn0) + coef(3) * vn0_up + coef(4) * vn0_dn
          + coef(5) * tiles(vn1) + coef(6) * vn1_dn + sgb_ref[1][None])
    s = jnp.concatenate([s0.reshape(rows, D), s1.reshape(rows, D)], axis=0)

    a0, a1 = halves(p[C] * p[H])
    hist = hist_ref[...].reshape(rows, D)
    conv = jnp.concatenate(_sample_conv(scr, caw_ref, a0, a1, hist), axis=0)
    state_ref[...] = a1.reshape(state_ref.shape)
    acc = jax.nn.sigmoid(p[GA]) * (p[B_] * conv)
    acc = acc + jax.nn.sigmoid(p[GB]) * (p[U] * s)
    mixab_ref[...] = acc.reshape(2, rows, D)
    gm_ref[...] = jax.nn.sigmoid(p[GM]).reshape(2, rows, D)
    q_ref[...] = p[Q].reshape(2, rows, D)


def _mixer_sample(l, xs, norm_mix_g, w_in, conv_a_w, sg_ln_g, sgc, sgb, cache_a):
    rows = xs.shape[1]
    tm = rows // 2
    wspec = lambda shape: _resident((None,) + shape, lambda i: (l,) + (0,) * len(shape))
    tok = pl.BlockSpec((2, tm, D), lambda i: (0, i, 0))
    hist = pl.BlockSpec((None, tm // 2, 2, D), lambda i: (l, i, 0, 0))
    out_f = jax.ShapeDtypeStruct((2, rows, D), F32)
    return pl.pallas_call(
        _mixer_sample_kernel,
        grid=(rows // tm,),
        in_specs=[tok, wspec((1, D)), wspec((D, D_IN)), wspec((3, D)), wspec((1, D)),
                  wspec((7, SUBLANES, D)), wspec((2, SUBLANES, D)), hist],
        out_specs=[tok] * 4 + [pl.BlockSpec((tm // 2, 2, D), lambda i: (i, 0, 0))],
        out_shape=[out_f] * 4 + [jax.ShapeDtypeStruct((rows // 2, 2, D), F32)],
        scratch_shapes=[pltpu.VMEM((tm + 2 * SUBLANES, D), F32)],
        compiler_params=pltpu.CompilerParams(
            dimension_semantics=("arbitrary",), vmem_limit_bytes=VMEM_LIMIT),
        name="mixer_sample",
    )(xs, norm_mix_g, w_in, conv_a_w, sg_ln_g, sgc, sgb, cache_a)


def _ffn_sample_kernel(*refs, final):
    if final:
        (x_ref, mix_ref, wo_ref, g_ref, wup_ref, cw_ref, cb_ref, wdn_ref, hist_ref, gfin_ref,
         xo_ref, state_ref, y_ref, scr) = refs
    else:
        (x_ref, mix_ref, wo_ref, g_ref, wup_ref, cw_ref, cb_ref, wdn_ref, hist_ref,
         xo_ref, state_ref, scr) = refs
    rows = x_ref.shape[1]
    _zero_margins(scr, rows)
    x1 = x_ref[...].reshape(2 * rows, D) + _dot(
        mix_ref[...].reshape(2 * rows, D).astype(BF16), wo_ref[...])
    h = _dot(_rms(x1, g_ref[...]).astype(BF16), wup_ref[...])
    h0, h1 = h[:rows], h[rows:]
    hist = hist_ref[...].reshape(rows, D_FF2)
    conv = jnp.concatenate(_sample_conv(scr, cw_ref, h0, h1, hist), axis=0)
    state_ref[...] = h1.reshape(state_ref.shape)
    x2 = x1 + _dot(_silu_gate(conv + cb_ref[...]), wdn_ref[...])
    xo_ref[...] = x2.reshape(2, rows, D)
    if final:
        y_ref[...] = _rms(x2, gfin_ref[...]).reshape(2, rows, D)


def _ffn_sample(l, final, xs, mix, w_o, norm_ffn_g, w_up, conv_f_w, conv_f_b, w_down, cache_f,
                norm_final_g):
    rows = xs.shape[1]
    tm = rows // 2
    tok = pl.BlockSpec((2, tm, D), lambda i: (0, i, 0))
    wspec = lambda shape: _resident((None,) + shape, lambda i: (l,) + (0,) * len(shape))
    x_shape = jax.ShapeDtypeStruct((2, rows, D), F32)
    in_specs = [tok, tok, wspec((D, D)), wspec((1, D)), wspec((D, D_FF2)), wspec((3, D_FF2)),
                wspec((1, D_FF2)), wspec((D_FF, D)),
                pl.BlockSpec((None, tm // 2, 2, D_FF2), lambda i: (l, i, 0, 0))]
    args = [xs, mix, w_o, norm_ffn_g, w_up, conv_f_w, conv_f_b, w_down, cache_f]
    out_specs = [tok, pl.BlockSpec((tm // 2, 2, D_FF2), lambda i: (i, 0, 0))]
    out_shape = [x_shape, jax.ShapeDtypeStruct((rows // 2, 2, D_FF2), F32)]
    if final:
        in_specs.append(_resident((1, D), lambda i: (0, 0)))
        args.append(norm_final_g)
        out_specs.append(tok)
        out_shape.append(x_shape)
    return pl.pallas_call(
        functools.partial(_ffn_sample_kernel, final=final),
        grid=(rows // tm,),
        in_specs=in_specs,
        out_specs=out_specs,
        out_shape=out_shape,
        scratch_shapes=[pltpu.VMEM((tm + 2 * SUBLANES, D_FF2), F32)],
        compiler_params=pltpu.CompilerParams(
            dimension_semantics=("arbitrary",), vmem_limit_bytes=VMEM_LIMIT),
        name="ffn_sample",
    )(*args)


def _sample_gating_tables(sg_w, sg_b):
    w = sg_w[:, :, :SAMPLE_STEPS, :SAMPLE_STEPS]
    zero = jnp.zeros_like(w[..., 0, 0])
    by_parity = [(w[..., 0, 0], w[..., 1, 1]), (zero, w[..., 1, 0]), (w[..., 2, 0], w[..., 3, 1]),
                 (w[..., 2, 1], zero), (zero, w[..., 3, 0]), (w[..., 2, 2], w[..., 3, 3]),
                 (zero, w[..., 3, 2])]

    def table(pairs):
        t = jnp.stack([jnp.stack(p, axis=1) for p in pairs], axis=1)
        t = jnp.tile(t, (1, 1, SUBLANES // 2, 1))
        return jnp.repeat(t, SG_COLS, axis=3)

    b = sg_b[:, :, :SAMPLE_STEPS]
    return table(by_parity), table([(b[..., 0], b[..., 1]), (b[..., 2], b[..., 3])])


def kernel(x_prompt, x_sample, mem_prompt, cache_conv_a, cache_conv_ffn, cache_mem_k, cache_mem_v, norm_mix_g, w_in, conv_a_w, sg_ln_g, sg_w, sg_b, norm_mem_g, w_k, w_v, w_o, norm_ffn_g, w_up, conv_f_w, conv_f_b, w_down, norm_final_g):
    depth = w_in.shape[0]
    bp = x_prompt.shape[0]
    bs, ts, _ = x_sample.shape
    assert ts == SAMPLE_STEPS

    w_in_b = w_in.astype(BF16)
    w_o_b = w_o.astype(BF16)
    w_up_b = w_up.astype(BF16)
    w_down_b = w_down.astype(BF16)
    wkv_b = jnp.concatenate([w_k, w_v], axis=-1).astype(BF16)

    nmg = norm_mix_g[:, None, :]
    lng = sg_ln_g[:, None, :]
    nfg = norm_ffn_g[:, None, :]
    nmemg = norm_mem_g[:, None, :]
    cfb = conv_f_b[:, None, :]
    gfin = norm_final_g[None, :]
    sg_b_t = jnp.swapaxes(sg_b, 1, 2)
    sgc, sgbias = _sample_gating_tables(sg_w, sg_b)
    to_halves = lambda a: jnp.swapaxes(a.reshape(bs, 2, 2, D), 0, 1).reshape(2, bs * 2, D)
    from_halves = lambda a: jnp.swapaxes(a.reshape(2, bs, 2, D), 0, 1).reshape(bs, ts, D)
    xs = to_halves(x_sample)

    k_f, v_f, k_b, v_b = _mem_kv(mem_prompt.reshape(bp * N_MEM, D), nmemg, wkv_b)
    k_b = k_b.reshape(depth * bp, N_MEM, D)
    v_b = v_b.reshape(depth * bp, N_MEM, D)

    xp = x_prompt
    pa, pf, sa, sf, sv = [], [], [], [], []
    yp = ys = None
    for l in range(depth):
        final = l == depth - 1
        mix, a_state = _mixer_prompt(l, xp, nmg, w_in_b, conv_a_w, lng, sg_w, sg_b_t, k_b, v_b)
        mixab, gm, q, vn, a_state_s = _mixer_sample(l, xs, nmg, w_in_b, conv_a_w, lng, sgc,
                                                    sgbias, cache_conv_a)
        outs = _ffn_prompt(l, final, xp, mix, w_o_b, nfg, w_up_b, conv_f_w, cfb, w_down_b, gfin,
                           q, gm, mixab, cache_mem_k, cache_mem_v)
        xp, f_state, mix_s = outs[0], outs[1], outs[2]
        if final:
            yp = outs[3]
        pa.append(a_state)
        pf.append(f_state)
        outs = _ffn_sample(l, final, xs, mix_s, w_o_b, nfg, w_up_b, conv_f_w, cfb, w_down_b,
                           cache_conv_ffn, gfin)
        xs, f_state_s = outs[0], outs[1]
        if final:
            ys = outs[2]
        sa.append(a_state_s)
        sf.append(f_state_s)
        sv.append(from_halves(vn))

    kv_shape = (depth, bp, N_MEM, HEADS, HEAD_DIM)
    return (yp, from_halves(ys),
            jnp.stack(pa), jnp.stack(pf), k_f.reshape(kv_shape), v_f.reshape(kv_shape),
            jnp.stack(sa), jnp.stack(sf), jnp.stack(sv))
```

```python
import functools

import jax
import jax.numpy as jnp
from jax import lax
from jax.experimental import pallas as pl
from jax.experimental.pallas import tpu as pltpu

F32 = jnp.float32
BF16 = jnp.bfloat16

D = 1024
D_IN = 9 * D
D_FF = 2816
D_FF2 = 2 * D_FF
N_MEM = 256
HEADS = 4
HEAD_DIM = D // HEADS
SG_GROUPS = 4
SG_COLS = D // SG_GROUPS
SG_CHUNK = 128
EPS = 1e-6
H, C, B_, U, V, Q, GA, GB, GM = range(9)
PROJ_GROUPS = [(V, Q, U), (GB, C, H), (B_, GA, GM)]

V7X_VMEM_BYTES = 64 * 1024 * 1024
VMEM_LIMIT = V7X_VMEM_BYTES - 4 * 1024 * 1024
SUBLANES = 8

SUBS_MIX = (256, 256, 256, 256)
SUBS_FFN = (256, 256)
TM_KV = 512
ATT_BB = 4
MASKED = -1e30


def _rms(x, g):
    y = x * lax.rsqrt(jnp.mean(x * x, axis=-1, keepdims=True) + EPS)
    return y * g


def _layernorm(x, g):
    xc = x - jnp.mean(x, axis=-1, keepdims=True)
    return xc * lax.rsqrt(jnp.mean(xc * xc, axis=-1, keepdims=True) + EPS) * g


def _dot(a, b):
    return jnp.dot(a, b, preferred_element_type=F32)


def _resident(shape, index_map):
    return pl.BlockSpec(shape, index_map, pipeline_mode=pl.Buffered(1))


def _causal_conv(w_ref, hist, cur):
    n = hist.shape[0]
    ext = jnp.concatenate([hist, cur], axis=0)
    x1 = pltpu.roll(ext, 1, axis=0)[n:, :]
    x2 = pltpu.roll(ext, 2, axis=0)[n:, :]
    return w_ref[0:1, :] * x2 + w_ref[1:2, :] * x1 + w_ref[2:3, :] * cur


def _silu_gate(hc):
    return (jax.nn.silu(hc[:, D_FF:]) * hc[:, :D_FF]).astype(BF16)


def _kv_kernel(mem_ref, g_ref, wkv_ref, k_ref, v_ref, kb_ref, vb_ref):
    m = _rms(mem_ref[...], g_ref[...]).astype(BF16)
    k = _dot(m, wkv_ref[:, :D])
    v = _dot(m, wkv_ref[:, D:])
    for h in range(HEADS):
        hc = slice(h * HEAD_DIM, (h + 1) * HEAD_DIM)
        k_ref[:, h, :] = k[:, hc]
        v_ref[:, h, :] = v[:, hc]
    kb_ref[...] = k.astype(BF16)
    vb_ref[...] = v.astype(BF16)


def _mem_kv(mem2d, norm_mem_g, wkv):
    depth = wkv.shape[0]
    rows = mem2d.shape[0]
    out_f = jax.ShapeDtypeStruct((depth, rows, HEADS, HEAD_DIM), F32)
    out_b = jax.ShapeDtypeStruct((depth, rows, D), BF16)
    oblk = pl.BlockSpec((None, TM_KV, D), lambda l, i: (l, i, 0))
    oblk_f = pl.BlockSpec((None, TM_KV, HEADS, HEAD_DIM), lambda l, i: (l, i, 0, 0))
    return pl.pallas_call(
        _kv_kernel,
        grid=(depth, rows // TM_KV),
        in_specs=[
            pl.BlockSpec((TM_KV, D), lambda l, i: (i, 0)),
            pl.BlockSpec((None, 1, D), lambda l, i: (l, 0, 0)),
            pl.BlockSpec((None, D, 2 * D), lambda l, i: (l, 0, 0)),
        ],
        out_specs=[oblk_f, oblk_f, oblk, oblk],
        out_shape=[out_f, out_f, out_b, out_b],
        compiler_params=pltpu.CompilerParams(
            dimension_semantics=("arbitrary", "arbitrary"), vmem_limit_bytes=VMEM_LIMIT),
        name="mem_kv",
    )(mem2d, norm_mem_g, wkv)


def _mixer_kernel(x_ref, nm_ref, win_ref, caw_ref, lng_ref, sgw_ref, sgb_ref, kb_ref, vb_ref,
                  mix_ref, state_ref, a_scr, *, subs, prev_rows):
    @pl.when(pl.program_id(1) == 0)
    def _():
        a_scr[...] = jnp.zeros((prev_rows, D), F32)

    ri = lax.broadcasted_iota(jnp.int32, (SG_CHUNK, SG_CHUNK), 0)
    ci = lax.broadcasted_iota(jnp.int32, (SG_CHUNK, SG_CHUNK), 1)
    sgw = [jnp.where(ci <= ri, sgw_ref[g], 0.0).astype(BF16) for g in range(SG_GROUPS)]
    offs = [sum(subs[:t]) for t in range(len(subs))]
    hist = [a_scr[...]]

    for t in range(len(subs)):
        rs = slice(offs[t], offs[t] + subs[t])
        z = _rms(x_ref[rs, :], nm_ref[...]).astype(BF16)
        proj = lambda i: _dot(z, win_ref[:, i * D:(i + 1) * D])

        v, q, u = proj(V), proj(Q), proj(U)
        vnb = _layernorm(v, lng_ref[...]).astype(BF16)
        s_rows = []
        for c in range(subs[t] // SG_CHUNK):
            s_cols = []
            for g in range(SG_GROUPS):
                vg = vnb[c * SG_CHUNK:(c + 1) * SG_CHUNK, g * SG_COLS:(g + 1) * SG_COLS]
                s_cols.append(_dot(sgw[g], vg) + sgb_ref[:, g:g + 1])
            s_rows.append(jnp.concatenate(s_cols, axis=1))
        s = jnp.concatenate(s_rows, axis=0)
        scores = [
            lax.dot_general(q[:, h * HEAD_DIM:(h + 1) * HEAD_DIM].astype(BF16),
                            kb_ref[:, h * HEAD_DIM:(h + 1) * HEAD_DIM],
                            (((1,), (1,)), ((), ())), preferred_element_type=F32)
            for h in range(HEADS)]

        gb, c_, h_ = proj(GB), proj(C), proj(H)
        ubs = jax.nn.sigmoid(gb) * u * s
        a = c_ * h_
        conv = _causal_conv(caw_ref, hist[0], a)
        hist[0] = a[subs[t] - prev_rows:, :]
        o_cols = []
        for h in range(HEADS):
            sc = scores[h] * (HEAD_DIM ** -0.5)
            e = jnp.exp(sc - jnp.max(sc, axis=-1, keepdims=True))
            p = e * (1.0 / jnp.sum(e, axis=-1, keepdims=True))
            o_cols.append(_dot(p.astype(BF16), vb_ref[:, h * HEAD_DIM:(h + 1) * HEAD_DIM]))
        y_m = jnp.concatenate(o_cols, axis=1)

        b_, ga, gm = proj(B_), proj(GA), proj(GM)
        acc = jax.nn.sigmoid(ga) * (b_ * conv) + ubs + jax.nn.sigmoid(gm) * y_m
        mix_ref[rs, :] = acc.astype(BF16)

    a_scr[...] = hist[0]

    @pl.when(pl.program_id(1) == pl.num_programs(1) - 1)
    def _():
        state_ref[...] = hist[0][prev_rows - 2:, :]


def _mixer_prompt(l, x, norm_mix_g, w_in, conv_a_w, sg_ln_g, sg_w, sg_b_t, kb, vb):
    bsz, t, _ = x.shape
    tm = sum(SUBS_MIX)
    prev_rows = SUBLANES
    wspec = lambda shape: _resident((None,) + shape, lambda b, i: (l,) + (0,) * len(shape))
    return pl.pallas_call(
        functools.partial(_mixer_kernel, subs=SUBS_MIX, prev_rows=prev_rows),
        grid=(bsz, t // tm),
        in_specs=[
            pl.BlockSpec((None, tm, D), lambda b, i: (b, i, 0)),
            wspec((1, D)),
            wspec((D, D_IN)),
            wspec((3, D)),
            wspec((1, D)),
            wspec((SG_GROUPS, SG_CHUNK, SG_CHUNK)),
            wspec((SG_CHUNK, SG_GROUPS)),
            pl.BlockSpec((None, N_MEM, D), lambda b, i: (l * bsz + b, 0, 0)),
            pl.BlockSpec((None, N_MEM, D), lambda b, i: (l * bsz + b, 0, 0)),
        ],
        out_specs=[
            pl.BlockSpec((None, tm, D), lambda b, i: (b, i, 0)),
            pl.BlockSpec((None, 2, D), lambda b, i: (b, 0, 0)),
        ],
        out_shape=[
            jax.ShapeDtypeStruct((bsz, t, D), BF16),
            jax.ShapeDtypeStruct((bsz, 2, D), F32),
        ],
        scratch_shapes=[pltpu.VMEM((prev_rows, D), F32)],
        compiler_params=pltpu.CompilerParams(
            dimension_semantics=("arbitrary", "arbitrary"), vmem_limit_bytes=VMEM_LIMIT),
        name="mixer_prompt",
    )(x, norm_mix_g, w_in, conv_a_w, sg_ln_g, sg_w, sg_b_t, kb, vb)


def _attn_scores(q_ref, k_ref):
    hb = q_ref.shape[1]
    nb = 2 * hb
    nq = HEADS * nb
    nk = N_MEM * HEADS
    row = lax.broadcasted_iota(jnp.int32, (nq, nk), 0)
    col = lax.broadcasted_iota(jnp.int32, (nq, nk), 1)
    same_head = (row >> (nb.bit_length() - 1)) == (col & (HEADS - 1))
    row_b = (row & (hb - 1)) >> 1
    q = q_ref[...].reshape(nb, D)
    q_all = jnp.concatenate([q[:, h * HEAD_DIM:(h + 1) * HEAD_DIM] for h in range(HEADS)],
                            axis=0).astype(BF16)
    scores = []
    for b in range(ATT_BB):
        k2 = k_ref[b].reshape(nk, HEAD_DIM).astype(BF16)
        sc = lax.dot_general(q_all, k2, (((1,), (1,)), ((), ())),
                             preferred_element_type=F32) * (HEAD_DIM ** -0.5)
        scores.append(jnp.where(same_head & (row_b == b), sc, MASKED))
    return scores


def _attn_values(scores, v_ref, hb):
    nb = 2 * hb
    nq = HEADS * nb
    out_b = (lax.broadcasted_iota(jnp.int32, (nq, 1), 0) & (hb - 1)) >> 1
    y_all = jnp.zeros((nq, HEAD_DIM), F32)
    for b, sc in enumerate(scores):
        v2 = v_ref[b].reshape(N_MEM * HEADS, HEAD_DIM).astype(BF16)
        e = jnp.exp(sc - jnp.max(sc, axis=-1, keepdims=True))
        p = e * (1.0 / jnp.sum(e, axis=-1, keepdims=True))
        y_all = jnp.where(out_b == b, _dot(p.astype(BF16), v2), y_all)
    y = jnp.concatenate([y_all[h * nb:(h + 1) * nb] for h in range(HEADS)], axis=1)
    return y.reshape(2, hb, D)


def _ffn_kernel(*refs, final, subs, prev_rows):
    (x_ref, mix_ref, wo_ref, g_ref, wup_ref, cw_ref, cb_ref, wdn_ref,
     sq_ref, sgm_ref, smixab_ref, sk_ref, sv_ref) = refs[:13]
    if final:
        gfin_ref, xo_ref, state_ref, smix_ref, y_ref, h_scr = refs[13:]
    else:
        xo_ref, state_ref, smix_ref, h_scr = refs[13:]

    @pl.when(pl.program_id(1) == 0)
    def _():
        h_scr[...] = jnp.zeros((prev_rows, D_FF2), F32)

    offs = [sum(subs[:t]) for t in range(len(subs))]
    slices = [slice(o, o + n) for o, n in zip(offs, subs)]
    x1 = [x_ref[rs, :] + _dot(mix_ref[rs, :], wo_ref[...]) for rs in slices]
    scores = _attn_scores(sq_ref, sk_ref)
    h = []
    for t in range(len(subs)):
        h.append(_dot(_rms(x1[t], g_ref[...]).astype(BF16), wup_ref[...]))
        if t == 0:
            smix_ref[...] = smixab_ref[...] + sgm_ref[...] * _attn_values(
                scores, sv_ref, sq_ref.shape[1])
    hist = h_scr[...]
    for t, rs in enumerate(slices):
        hc = _causal_conv(cw_ref, hist, h[t]) + cb_ref[...]
        hist = h[t][subs[t] - prev_rows:, :]
        x2 = x1[t] + _dot(_silu_gate(hc), wdn_ref[...])
        xo_ref[rs, :] = x2
        if final:
            y_ref[rs, :] = _rms(x2, gfin_ref[...])
    h_scr[...] = hist

    @pl.when(pl.program_id(1) == pl.num_programs(1) - 1)
    def _():
        state_ref[...] = hist[prev_rows - 2:, :]


def _ffn_prompt(l, final, x, mix, w_o, norm_ffn_g, w_up, conv_f_w, conv_f_b, w_down, norm_final_g,
                sq, sgm, smixab, cache_k, cache_v):
    bsz, t, _ = x.shape
    tm, prev_rows = sum(SUBS_FFN), SUBLANES
    n_t = t // tm
    hb = 2 * ATT_BB
    assert hb & (hb - 1) == 0 and sq.shape[1] == hb * bsz * n_t
    tok = pl.BlockSpec((None, tm, D), lambda b, i: (b, i, 0))
    wspec = lambda shape: _resident((None,) + shape, lambda b, i: (l,) + (0,) * len(shape))
    small = pl.BlockSpec((2, hb, D), lambda b, i: (0, b * n_t + i, 0))
    big = pl.BlockSpec((None, ATT_BB, N_MEM, HEADS, HEAD_DIM),
                       lambda b, i: (l, b * n_t + i, 0, 0, 0))
    x_shape = jax.ShapeDtypeStruct((bsz, t, D), F32)
    in_specs = [tok, tok, wspec((D, D)), wspec((1, D)), wspec((D, D_FF2)), wspec((3, D_FF2)),
                wspec((1, D_FF2)), wspec((D_FF, D)), small, small, small, big, big]
    args = [x, mix, w_o, norm_ffn_g, w_up, conv_f_w, conv_f_b, w_down, sq, sgm, smixab,
            cache_k, cache_v]
    out_specs = [tok, pl.BlockSpec((None, 2, D_FF2), lambda b, i: (b, 0, 0)), small]
    out_shape = [x_shape, jax.ShapeDtypeStruct((bsz, 2, D_FF2), F32),
                 jax.ShapeDtypeStruct(sq.shape, F32)]
    if final:
        in_specs.append(_resident((1, D), lambda b, i: (0, 0)))
        args.append(norm_final_g)
        out_specs.append(tok)
        out_shape.append(x_shape)
    return pl.pallas_call(
        functools.partial(_ffn_kernel, final=final, subs=SUBS_FFN, prev_rows=prev_rows),
        grid=(bsz, t // tm),
        in_specs=in_specs,
        out_specs=out_specs,
        out_shape=out_shape,
        scratch_shapes=[pltpu.VMEM((prev_rows, D_FF2), F32)],
        compiler_params=pltpu.CompilerParams(
            dimension_semantics=("arbitrary", "arbitrary"), vmem_limit_bytes=VMEM_LIMIT),
        name="ffn_prompt",
    )(*args)


SAMPLE_STEPS = 4


def _put(scr, x):
    scr[SUBLANES:SUBLANES + x.shape[0], :] = x


def _get(scr, rows, k):
    return scr[SUBLANES - k:SUBLANES - k + rows, :]


def _zero_margins(scr, rows):
    scr[0:SUBLANES, :] = jnp.zeros((SUBLANES, scr.shape[1]), F32)
    scr[SUBLANES + rows:, :] = jnp.zeros((SUBLANES, scr.shape[1]), F32)


def _sample_conv(scr, w_ref, cur0, cur1, hist):
    rows = cur0.shape[0]
    first = (lax.broadcasted_iota(jnp.int32, (rows, 1), 0) & 1) == 0
    _put(scr, hist)
    hist_up = _get(scr, rows, -1)
    _put(scr, cur0)
    cur0_dn = _get(scr, rows, 1)
    cur0_up = _get(scr, rows, -1)
    _put(scr, cur1)
    cur1_dn = _get(scr, rows, 1)
    w0, w1, w2 = w_ref[0:1, :], w_ref[1:2, :], w_ref[2:3, :]
    c0 = w0 * hist + w1 * jnp.where(first, hist_up, cur0_dn) + w2 * cur0
    c1 = w0 * cur0 + w1 * jnp.where(first, cur0_up, cur1_dn) + w2 * cur1
    return c0, c1


def _mixer_sample_kernel(x_ref, nm_ref, win_ref, caw_ref, lng_ref, sgc_ref, sgb_ref, hist_ref,
                         mixab_ref, gm_ref, q_ref, vn_ref, state_ref, scr):
    rows = x_ref.shape[1]
    halves = lambda a: (a[:rows], a[rows:])
    _zero_margins(scr, rows)
    z = _rms(x_ref[...].reshape(2 * rows, D), nm_ref[...]).astype(BF16)
    p = {}
    for group in PROJ_GROUPS:
        for i in group:
            p[i] = _dot(z, win_ref[:, i * D:(i + 1) * D])

    vn0, vn1 = halves(_layernorm(p[V], lng_ref[...]))
    vn_ref[0] = vn0
    vn_ref[1] = vn1
    tiles = lambda a: a.reshape(rows // SUBLANES, SUBLANES, D)
    coef = lambda k: sgc_ref[k][None]
    _put(scr, vn0)
    vn0_dn, vn0_up = tiles(_get(scr, rows, 1)), tiles(_get(scr, rows, -1))
    _put(scr, vn1)
    vn1_dn = tiles(_get(scr, rows, 1))
    s0 = coef(0) * tiles(vn0) + coef(1) * vn0_dn + sgb_ref[0][None]
    s1 = (coef(2) * tiles(vn0) + coef(3) * vn0_up + coef(4) * vn0_dn
          + coef(5) * tiles(vn1) + coef(6) * vn1_dn + sgb_ref[1][None])
    s = jnp.concatenate([s0.reshape(rows, D), s1.reshape(rows, D)], axis=0)

    a0, a1 = halves(p[C] * p[H])
    hist = hist_ref[...].reshape(rows, D)
    conv = jnp.concatenate(_sample_conv(scr, caw_ref, a0, a1, hist), axis=0)
    state_ref[...] = a1.reshape(state_ref.shape)
    acc = jax.nn.sigmoid(p[GA]) * (p[B_] * conv)
    acc = acc + jax.nn.sigmoid(p[GB]) * (p[U] * s)
    mixab_ref[...] = acc.reshape(2, rows, D)
    gm_ref[...] = jax.nn.sigmoid(p[GM]).reshape(2, rows, D)
    q_ref[...] = p[Q].reshape(2, rows, D)


def _mixer_sample(l, xs, norm_mix_g, w_in, conv_a_w, sg_ln_g, sgc, sgb, cache_a):
    rows = xs.shape[1]
    tm = rows // 2
    wspec = lambda shape: _resident((None,) + shape, lambda i: (l,) + (0,) * len(shape))
    tok = pl.BlockSpec((2, tm, D), lambda i: (0, i, 0))
    hist = pl.BlockSpec((None, tm // 2, 2, D), lambda i: (l, i, 0, 0))
    out_f = jax.ShapeDtypeStruct((2, rows, D), F32)
    return pl.pallas_call(
        _mixer_sample_kernel,
        grid=(rows // tm,),
        in_specs=[tok, wspec((1, D)), wspec((D, D_IN)), wspec((3, D)), wspec((1, D)),
                  wspec((7, SUBLANES, D)), wspec((2, SUBLANES, D)), hist],
        out_specs=[tok] * 4 + [pl.BlockSpec((tm // 2, 2, D), lambda i: (i, 0, 0))],
        out_shape=[out_f] * 4 + [jax.ShapeDtypeStruct((rows // 2, 2, D), F32)],
        scratch_shapes=[pltpu.VMEM((tm + 2 * SUBLANES, D), F32)],
        compiler_params=pltpu.CompilerParams(
            dimension_semantics=("arbitrary",), vmem_limit_bytes=VMEM_LIMIT),
        name="mixer_sample",
    )(xs, norm_mix_g, w_in, conv_a_w, sg_ln_g, sgc, sgb, cache_a)


def _ffn_sample_kernel(*refs, final):
    if final:
        (x_ref, mix_ref, wo_ref, g_ref, wup_ref, cw_ref, cb_ref, wdn_ref, hist_ref, gfin_ref,
         xo_ref, state_ref, y_ref, scr) = refs
    else:
        (x_ref, mix_ref, wo_ref, g_ref, wup_ref, cw_ref, cb_ref, wdn_ref, hist_ref,
         xo_ref, state_ref, scr) = refs
    rows = x_ref.shape[1]
    _zero_margins(scr, rows)
    x1 = x_ref[...].reshape(2 * rows, D) + _dot(
        mix_ref[...].reshape(2 * rows, D).astype(BF16), wo_ref[...])
    h = _dot(_rms(x1, g_ref[...]).astype(BF16), wup_ref[...])
    h0, h1 = h[:rows], h[rows:]
    hist = hist_ref[...].reshape(rows, D_FF2)
    conv = jnp.concatenate(_sample_conv(scr, cw_ref, h0, h1, hist), axis=0)
    state_ref[...] = h1.reshape(state_ref.shape)
    x2 = x1 + _dot(_silu_gate(conv + cb_ref[...]), wdn_ref[...])
    xo_ref[...] = x2.reshape(2, rows, D)
    if final:
        y_ref[...] = _rms(x2, gfin_ref[...]).reshape(2, rows, D)


def _ffn_sample(l, final, xs, mix, w_o, norm_ffn_g, w_up, conv_f_w, conv_f_b, w_down, cache_f,
                norm_final_g):
    rows = xs.shape[1]
    tm = rows // 2
    tok = pl.BlockSpec((2, tm, D), lambda i: (0, i, 0))
    wspec = lambda shape: _resident((None,) + shape, lambda i: (l,) + (0,) * len(shape))
    x_shape = jax.ShapeDtypeStruct((2, rows, D), F32)
    in_specs = [tok, tok, wspec((D, D)), wspec((1, D)), wspec((D, D_FF2)), wspec((3, D_FF2)),
                wspec((1, D_FF2)), wspec((D_FF, D)),
                pl.BlockSpec((None, tm // 2, 2, D_FF2), lambda i: (l, i, 0, 0))]
    args = [xs, mix, w_o, norm_ffn_g, w_up, conv_f_w, conv_f_b, w_down, cache_f]
    out_specs = [tok, pl.BlockSpec((tm // 2, 2, D_FF2), lambda i: (i, 0, 0))]
    out_shape = [x_shape, jax.ShapeDtypeStruct((rows // 2, 2, D_FF2), F32)]
    if final:
        in_specs.append(_resident((1, D), lambda i: (0, 0)))
        args.append(norm_final_g)
        out_specs.append(tok)
        out_shape.append(x_shape)
    return pl.pallas_call(
        functools.partial(_ffn_sample_kernel, final=final),
        grid=(rows // tm,),
        in_specs=in_specs,
        out_specs=out_specs,
        out_shape=out_shape,
        scratch_shapes=[pltpu.VMEM((tm + 2 * SUBLANES, D_FF2), F32)],
        compiler_params=pltpu.CompilerParams(
            dimension_semantics=("arbitrary",), vmem_limit_bytes=VMEM_LIMIT),
        name="ffn_sample",
    )(*args)


def _sample_gating_tables(sg_w, sg_b):
    w = sg_w[:, :, :SAMPLE_STEPS, :SAMPLE_STEPS]
    zero = jnp.zeros_like(w[..., 0, 0])
    by_parity = [(w[..., 0, 0], w[..., 1, 1]), (zero, w[..., 1, 0]), (w[..., 2, 0], w[..., 3, 1]),
                 (w[..., 2, 1], zero), (zero, w[..., 3, 0]), (w[..., 2, 2], w[..., 3, 3]),
                 (zero, w[..., 3, 2])]

    def table(pairs):
        t = jnp.stack([jnp.stack(p, axis=1) for p in pairs], axis=1)
        t = jnp.tile(t, (1, 1, SUBLANES // 2, 1))
        return jnp.repeat(t, SG_COLS, axis=3)

    b = sg_b[:, :, :SAMPLE_STEPS]
    return table(by_parity), table([(b[..., 0], b[..., 1]), (b[..., 2], b[..., 3])])


def kernel(x_prompt, x_sample, mem_prompt, cache_conv_a, cache_conv_ffn, cache_mem_k, cache_mem_v, norm_mix_g, w_in, conv_a_w, sg_ln_g, sg_w, sg_b, norm_mem_g, w_k, w_v, w_o, norm_ffn_g, w_up, conv_f_w, conv_f_b, w_down, norm_final_g):
    depth = w_in.shape[0]
    bp = x_prompt.shape[0]
    bs, ts, _ = x_sample.shape
    assert ts == SAMPLE_STEPS

    w_in_b = w_in.astype(BF16)
    w_o_b = w_o.astype(BF16)
    w_up_b = w_up.astype(BF16)
    w_down_b = w_down.astype(BF16)
    wkv_b = jnp.concatenate([w_k, w_v], axis=-1).astype(BF16)

    nmg = norm_mix_g[:, None, :]
    lng = sg_ln_g[:, None, :]
    nfg = norm_ffn_g[:, None, :]
    nmemg = norm_mem_g[:, None, :]
    cfb = conv_f_b[:, None, :]
    gfin = norm_final_g[None, :]
    sg_b_t = jnp.swapaxes(sg_b, 1, 2)
    sgc, sgbias = _sample_gating_tables(sg_w, sg_b)
    to_halves = lambda a: jnp.swapaxes(a.reshape(bs, 2, 2, D), 0, 1).reshape(2, bs * 2, D)
    from_halves = lambda a: jnp.swapaxes(a.reshape(2, bs, 2, D), 0, 1).reshape(bs, ts, D)
    xs = to_halves(x_sample)

    k_f, v_f, k_b, v_b = _mem_kv(mem_prompt.reshape(bp * N_MEM, D), nmemg, wkv_b)
    k_b = k_b.reshape(depth * bp, N_MEM, D)
    v_b = v_b.reshape(depth * bp, N_MEM, D)

    xp = x_prompt
    pa, pf, sa, sf, sv = [], [], [], [], []
    yp = ys = None
    for l in range(depth):
        final = l == depth - 1
        mix, a_state = _mixer_prompt(l, xp, nmg, w_in_b, conv_a_w, lng, sg_w, sg_b_t, k_b, v_b)
        mixab, gm, q, vn, a_state_s = _mixer_sample(l, xs, nmg, w_in_b, conv_a_w, lng, sgc,
                                                    sgbias, cache_conv_a)
        outs = _ffn_prompt(l, final, xp, mix, w_o_b, nfg, w_up_b, conv_f_w, cfb, w_down_b, gfin,
                           q, gm, mixab, cache_mem_k, cache_mem_v)
        xp, f_state, mix_s = outs[0], outs[1], outs[2]
        if final:
            yp = outs[3]
        pa.append(a_state)
        pf.append(f_state)
        outs = _ffn_sample(l, final, xs, mix_s, w_o_b, nfg, w_up_b, conv_f_w, cfb, w_down_b,
                           cache_conv_ffn, gfin)
        xs, f_state_s = outs[0], outs[1]
        if final:
            ys = outs[2]
        sa.append(a_state_s)
        sf.append(f_state_s)
        sv.append(from_halves(vn))

    kv_shape = (depth, bp, N_MEM, HEADS, HEAD_DIM)
    return (yp, from_halves(ys),
            jnp.stack(pa), jnp.stack(pf), k_f.reshape(kv_shape), v_f.reshape(kv_shape),
            jnp.stack(sa), jnp.stack(sf), jnp.stack(sv))
```

```python
import functools

import jax
import jax.numpy as jnp
from jax import lax
from jax.experimental import pallas as pl
from jax.experimental.pallas import tpu as pltpu

F32 = jnp.float32
BF16 = jnp.bfloat16

D = 1024
D_IN = 9 * D
D_FF = 2816
D_FF2 = 2 * D_FF
N_MEM = 256
HEADS = 4
HEAD_DIM = D // HEADS
SG_GROUPS = 4
SG_COLS = D // SG_GROUPS
SG_CHUNK = 128
EPS = 1e-6
H, C, B_, U, V, Q, GA, GB, GM = range(9)
PROJ_GROUPS = [(V, Q, U), (GB, C, H), (B_, GA, GM)]

V7X_VMEM_BYTES = 64 * 1024 * 1024
VMEM_LIMIT = V7X_VMEM_BYTES - 4 * 1024 * 1024
SUBLANES = 8

SUBS_MIX = (256, 256)
SUBS_FFN = (256, 256, 256, 256)
TM_KV = 512
ATT_BB = 4
MASKED = -1e30


def _rms(x, g):
    y = x * lax.rsqrt(jnp.mean(x * x, axis=-1, keepdims=True) + EPS)
    return y * g


def _layernorm(x, g):
    xc = x - jnp.mean(x, axis=-1, keepdims=True)
    return xc * lax.rsqrt(jnp.mean(xc * xc, axis=-1, keepdims=True) + EPS) * g


def _dot(a, b):
    return jnp.dot(a, b, preferred_element_type=F32)


def _resident(shape, index_map):
    return pl.BlockSpec(shape, index_map, pipeline_mode=pl.Buffered(1))


def _causal_conv(w_ref, hist, cur):
    n = hist.shape[0]
    ext = jnp.concatenate([hist, cur], axis=0)
    x1 = pltpu.roll(ext, 1, axis=0)[n:, :]
    x2 = pltpu.roll(ext, 2, axis=0)[n:, :]
    return w_ref[0:1, :] * x2 + w_ref[1:2, :] * x1 + w_ref[2:3, :] * cur


def _silu_gate(hc):
    return (jax.nn.silu(hc[:, D_FF:]) * hc[:, :D_FF]).astype(BF16)


def _kv_kernel(mem_ref, g_ref, wkv_ref, k_ref, v_ref, kb_ref, vb_ref):
    m = _rms(mem_ref[...], g_ref[...]).astype(BF16)
    k = _dot(m, wkv_ref[:, :D])
    v = _dot(m, wkv_ref[:, D:])
    for h in range(HEADS):
        hc = slice(h * HEAD_DIM, (h + 1) * HEAD_DIM)
        k_ref[:, h, :] = k[:, hc]
        v_ref[:, h, :] = v[:, hc]
    kb_ref[...] = k.astype(BF16)
    vb_ref[...] = v.astype(BF16)


def _mem_kv(mem2d, norm_mem_g, wkv):
    depth = wkv.shape[0]
    rows = mem2d.shape[0]
    out_f = jax.ShapeDtypeStruct((depth, rows, HEADS, HEAD_DIM), F32)
    out_b = jax.ShapeDtypeStruct((depth, rows, D), BF16)
    oblk = pl.BlockSpec((None, TM_KV, D), lambda l, i: (l, i, 0))
    oblk_f = pl.BlockSpec((None, TM_KV, HEADS, HEAD_DIM), lambda l, i: (l, i, 0, 0))
    return pl.pallas_call(
        _kv_kernel,
        grid=(depth, rows // TM_KV),
        in_specs=[
            pl.BlockSpec((TM_KV, D), lambda l, i: (i, 0)),
            pl.BlockSpec((None, 1, D), lambda l, i: (l, 0, 0)),
            pl.BlockSpec((None, D, 2 * D), lambda l, i: (l, 0, 0)),
        ],
        out_specs=[oblk_f, oblk_f, oblk, oblk],
        out_shape=[out_f, out_f, out_b, out_b],
        compiler_params=pltpu.CompilerParams(
            dimension_semantics=("arbitrary", "arbitrary"), vmem_limit_bytes=VMEM_LIMIT),
        name="mem_kv",
    )(mem2d, norm_mem_g, wkv)


def _attn_scores(q_ref, k_ref):
    hb = q_ref.shape[1]
    nb = 2 * hb
    nq = HEADS * nb
    nk = N_MEM * HEADS
    row = lax.broadcasted_iota(jnp.int32, (nq, nk), 0)
    col = lax.broadcasted_iota(jnp.int32, (nq, nk), 1)
    same_head = (row >> (nb.bit_length() - 1)) == (col & (HEADS - 1))
    row_b = (row & (hb - 1)) >> 1
    q = q_ref[...].reshape(nb, D)
    q_all = jnp.concatenate([q[:, h * HEAD_DIM:(h + 1) * HEAD_DIM] for h in range(HEADS)],
                            axis=0).astype(BF16)
    scores = []
    for b in range(ATT_BB):
        k2 = k_ref[b].reshape(nk, HEAD_DIM).astype(BF16)
        sc = lax.dot_general(q_all, k2, (((1,), (1,)), ((), ())),
                             preferred_element_type=F32) * (HEAD_DIM ** -0.5)
        scores.append(jnp.where(same_head & (row_b == b), sc, MASKED))
    return scores


def _attn_values(scores, v_ref, hb):
    nb = 2 * hb
    nq = HEADS * nb
    out_b = (lax.broadcasted_iota(jnp.int32, (nq, 1), 0) & (hb - 1)) >> 1
    y_all = jnp.zeros((nq, HEAD_DIM), F32)
    for b, sc in enumerate(scores):
        v2 = v_ref[b].reshape(N_MEM * HEADS, HEAD_DIM).astype(BF16)
        e = jnp.exp(sc - jnp.max(sc, axis=-1, keepdims=True))
        p = e * (1.0 / jnp.sum(e, axis=-1, keepdims=True))
        y_all = jnp.where(out_b == b, _dot(p.astype(BF16), v2), y_all)
    y = jnp.concatenate([y_all[h * nb:(h + 1) * nb] for h in range(HEADS)], axis=1)
    return y.reshape(2, hb, D)


def _mixer_kernel(x_ref, nm_ref, win_ref, caw_ref, lng_ref, sgw_ref, sgb_ref, kb_ref, vb_ref,
                  sq_ref, sgm_ref, smixab_ref, sk_ref, sv_ref,
                  mix_ref, state_ref, smix_ref, a_scr, *, subs, prev_rows):
    @pl.when(pl.program_id(1) == 0)
    def _():
        a_scr[...] = jnp.zeros((prev_rows, D), F32)

    ri = lax.broadcasted_iota(jnp.int32, (SG_CHUNK, SG_CHUNK), 0)
    ci = lax.broadcasted_iota(jnp.int32, (SG_CHUNK, SG_CHUNK), 1)
    sgw = [jnp.where(ci <= ri, sgw_ref[g], 0.0).astype(BF16) for g in range(SG_GROUPS)]
    offs = [sum(subs[:t]) for t in range(len(subs))]
    hist = [a_scr[...]]

    for t in range(len(subs)):
        rs = slice(offs[t], offs[t] + subs[t])
        z = _rms(x_ref[rs, :], nm_ref[...]).astype(BF16)
        proj = lambda i: _dot(z, win_ref[:, i * D:(i + 1) * D])

        v, q, u = proj(V), proj(Q), proj(U)
        vnb = _layernorm(v, lng_ref[...]).astype(BF16)
        s_rows = []
        for c in range(subs[t] // SG_CHUNK):
            s_cols = []
            for g in range(SG_GROUPS):
                vg = vnb[c * SG_CHUNK:(c + 1) * SG_CHUNK, g * SG_COLS:(g + 1) * SG_COLS]
                s_cols.append(_dot(sgw[g], vg) + sgb_ref[:, g:g + 1])
            s_rows.append(jnp.concatenate(s_cols, axis=1))
        s = jnp.concatenate(s_rows, axis=0)
        scores = [
            lax.dot_general(q[:, h * HEAD_DIM:(h + 1) * HEAD_DIM].astype(BF16),
                            kb_ref[:, h * HEAD_DIM:(h + 1) * HEAD_DIM],
                            (((1,), (1,)), ((), ())), preferred_element_type=F32)
            for h in range(HEADS)]

        if t == 1:
            smix_ref[...] = smixab_ref[...] + sgm_ref[...] * _attn_values(
                sample_scores, sv_ref, sq_ref.shape[1])
        gb, c_, h_ = proj(GB), proj(C), proj(H)
        if t == 0:
            sample_scores = _attn_scores(sq_ref, sk_ref)
        ubs = jax.nn.sigmoid(gb) * u * s
        a = c_ * h_
        conv = _causal_conv(caw_ref, hist[0], a)
        hist[0] = a[subs[t] - prev_rows:, :]
        o_cols = []
        for h in range(HEADS):
            sc = scores[h] * (HEAD_DIM ** -0.5)
            e = jnp.exp(sc - jnp.max(sc, axis=-1, keepdims=True))
            p = e * (1.0 / jnp.sum(e, axis=-1, keepdims=True))
            o_cols.append(_dot(p.astype(BF16), vb_ref[:, h * HEAD_DIM:(h + 1) * HEAD_DIM]))
        y_m = jnp.concatenate(o_cols, axis=1)

        b_, ga, gm = proj(B_), proj(GA), proj(GM)
        acc = jax.nn.sigmoid(ga) * (b_ * conv) + ubs + jax.nn.sigmoid(gm) * y_m
        mix_ref[rs, :] = acc.astype(BF16)

    a_scr[...] = hist[0]

    @pl.when(pl.program_id(1) == pl.num_programs(1) - 1)
    def _():
        state_ref[...] = hist[0][prev_rows - 2:, :]


def _mixer_prompt(l, x, norm_mix_g, w_in, conv_a_w, sg_ln_g, sg_w, sg_b_t, kb, vb,
                  sq, sgm, smixab, cache_k, cache_v):
    bsz, t, _ = x.shape
    tm = sum(SUBS_MIX)
    assert len(SUBS_MIX) >= 2
    n_t = t // tm
    prev_rows = SUBLANES
    hb = 2 * ATT_BB
    assert hb & (hb - 1) == 0 and sq.shape[1] == hb * bsz * n_t
    wspec = lambda shape: _resident((None,) + shape, lambda b, i: (l,) + (0,) * len(shape))
    small = pl.BlockSpec((2, hb, D), lambda b, i: (0, b * n_t + i, 0))
    big = pl.BlockSpec((None, ATT_BB, N_MEM, HEADS, HEAD_DIM),
                       lambda b, i: (l, b * n_t + i, 0, 0, 0))
    return pl.pallas_call(
        functools.partial(_mixer_kernel, subs=SUBS_MIX, prev_rows=prev_rows),
        grid=(bsz, t // tm),
        in_specs=[
            pl.BlockSpec((None, tm, D), lambda b, i: (b, i, 0)),
            wspec((1, D)),
            wspec((D, D_IN)),
            wspec((3, D)),
            wspec((1, D)),
            wspec((SG_GROUPS, SG_CHUNK, SG_CHUNK)),
            wspec((SG_CHUNK, SG_GROUPS)),
            pl.BlockSpec((None, N_MEM, D), lambda b, i: (l * bsz + b, 0, 0)),
            pl.BlockSpec((None, N_MEM, D), lambda b, i: (l * bsz + b, 0, 0)),
            small, small, small, big, big,
        ],
        out_specs=[
            pl.BlockSpec((None, tm, D), lambda b, i: (b, i, 0)),
            pl.BlockSpec((None, 2, D), lambda b, i: (b, 0, 0)),
            small,
        ],
        out_shape=[
            jax.ShapeDtypeStruct((bsz, t, D), BF16),
            jax.ShapeDtypeStruct((bsz, 2, D), F32),
            jax.ShapeDtypeStruct(sq.shape, F32),
        ],
        scratch_shapes=[pltpu.VMEM((prev_rows, D), F32)],
        compiler_params=pltpu.CompilerParams(
            dimension_semantics=("arbitrary", "arbitrary"), vmem_limit_bytes=VMEM_LIMIT),
        name="mixer_prompt",
    )(x, norm_mix_g, w_in, conv_a_w, sg_ln_g, sg_w, sg_b_t, kb, vb, sq, sgm, smixab, cache_k, cache_v)


def _ffn_kernel(*refs, final, subs, prev_rows):
    if final:
        (x_ref, mix_ref, wo_ref, g_ref, wup_ref, cw_ref, cb_ref, wdn_ref, gfin_ref,
         out_ref, state_ref, h_scr) = refs
    else:
        (x_ref, mix_ref, wo_ref, g_ref, wup_ref, cw_ref, cb_ref, wdn_ref,
         out_ref, state_ref, h_scr) = refs

    @pl.when(pl.program_id(1) == 0)
    def _():
        h_scr[...] = jnp.zeros((prev_rows, D_FF2), F32)

    offs = [sum(subs[:t]) for t in range(len(subs))]
    slices = [slice(o, o + n) for o, n in zip(offs, subs)]
    x1 = [x_ref[rs, :] + _dot(mix_ref[rs, :], wo_ref[...]) for rs in slices]
    h = [_dot(_rms(x1[t], g_ref[...]).astype(BF16), wup_ref[...]) for t in range(len(subs))]
    hist = h_scr[...]
    for t, rs in enumerate(slices):
        hc = _causal_conv(cw_ref, hist, h[t]) + cb_ref[...]
        hist = h[t][subs[t] - prev_rows:, :]
        x2 = x1[t] + _dot(_silu_gate(hc), wdn_ref[...])
        out_ref[rs, :] = _rms(x2, gfin_ref[...]) if final else x2
    h_scr[...] = hist

    @pl.when(pl.program_id(1) == pl.num_programs(1) - 1)
    def _():
        state_ref[...] = hist[prev_rows - 2:, :]


def _ffn_prompt(l, final, x, mix, w_o, norm_ffn_g, w_up, conv_f_w, conv_f_b, w_down, norm_final_g):
    bsz, t, _ = x.shape
    tm, prev_rows = sum(SUBS_FFN), SUBLANES
    tok = pl.BlockSpec((None, tm, D), lambda b, i: (b, i, 0))
    wspec = lambda shape: _resident((None,) + shape, lambda b, i: (l,) + (0,) * len(shape))
    x_shape = jax.ShapeDtypeStruct((bsz, t, D), F32)
    in_specs = [tok, tok, wspec((D, D)), wspec((1, D)), wspec((D, D_FF2)), wspec((3, D_FF2)),
                wspec((1, D_FF2)), wspec((D_FF, D))]
    args = [x, mix, w_o, norm_ffn_g, w_up, conv_f_w, conv_f_b, w_down]
    out_specs = [tok, pl.BlockSpec((None, 2, D_FF2), lambda b, i: (b, 0, 0))]
    out_shape = [x_shape, jax.ShapeDtypeStruct((bsz, 2, D_FF2), F32)]
    if final:
        in_specs.append(_resident((1, D), lambda b, i: (0, 0)))
        args.append(norm_final_g)
    return pl.pallas_call(
        functools.partial(_ffn_kernel, final=final, subs=SUBS_FFN, prev_rows=prev_rows),
        grid=(bsz, t // tm),
        in_specs=in_specs,
        out_specs=out_specs,
        out_shape=out_shape,
        scratch_shapes=[pltpu.VMEM((prev_rows, D_FF2), F32)],
        compiler_params=pltpu.CompilerParams(
            dimension_semantics=("arbitrary", "arbitrary"), vmem_limit_bytes=VMEM_LIMIT),
        name="ffn_prompt",
    )(*args)


SAMPLE_STEPS = 4


def _put(scr, x):
    scr[SUBLANES:SUBLANES + x.shape[0], :] = x


def _get(scr, rows, k):
    return scr[SUBLANES - k:SUBLANES - k + rows, :]


def _zero_margins(scr, rows):
    scr[0:SUBLANES, :] = jnp.zeros((SUBLANES, scr.shape[1]), F32)
    scr[SUBLANES + rows:, :] = jnp.zeros((SUBLANES, scr.shape[1]), F32)


def _sample_conv(scr, w_ref, cur0, cur1, hist):
    rows = cur0.shape[0]
    first = (lax.broadcasted_iota(jnp.int32, (rows, 1), 0) & 1) == 0
    _put(scr, hist)
    hist_up = _get(scr, rows, -1)
    _put(scr, cur0)
    cur0_dn = _get(scr, rows, 1)
    cur0_up = _get(scr, rows, -1)
    _put(scr, cur1)
    cur1_dn = _get(scr, rows, 1)
    w0, w1, w2 = w_ref[0:1, :], w_ref[1:2, :], w_ref[2:3, :]
    c0 = w0 * hist + w1 * jnp.where(first, hist_up, cur0_dn) + w2 * cur0
    c1 = w0 * cur0 + w1 * jnp.where(first, cur0_up, cur1_dn) + w2 * cur1
    return c0, c1


def _mixer_sample_kernel(x_ref, nm_ref, win_ref, caw_ref, lng_ref, sgc_ref, sgb_ref, hist_ref,
                         mixab_ref, gm_ref, q_ref, vn_ref, state_ref, scr):
    rows = x_ref.shape[1]
    halves = lambda a: (a[:rows], a[rows:])
    _zero_margins(scr, rows)
    z = _rms(x_ref[...].reshape(2 * rows, D), nm_ref[...]).astype(BF16)
    p = {}
    for group in PROJ_GROUPS:
        for i in group:
            p[i] = _dot(z, win_ref[:, i * D:(i + 1) * D])

    vn0, vn1 = halves(_layernorm(p[V], lng_ref[...]))
    vn_ref[0] = vn0
    vn_ref[1] = vn1
    tiles = lambda a: a.reshape(rows // SUBLANES, SUBLANES, D)
    coef = lambda k: sgc_ref[k][None]
    _put(scr, vn0)
    vn0_dn, vn0_up = tiles(_get(scr, rows, 1)), tiles(_get(scr, rows, -1))
    _put(scr, vn1)
    vn1_dn = tiles(_get(scr, rows, 1))
    s0 = coef(0) * tiles(vn0) + coef(1) * vn0_dn + sgb_ref[0][None]
    s1 = (coef(2) * tiles(vn0) + coef(3) * vn0_up + coef(4) * vn0_dn
          + coef(5) * tiles(vn1) + coef(6) * vn1_dn + sgb_ref[1][None])
    s = jnp.concatenate([s0.reshape(rows, D), s1.reshape(rows, D)], axis=0)

    a0, a1 = halves(p[C] * p[H])
    hist = hist_ref[...].reshape(rows, D)
    conv = jnp.concatenate(_sample_conv(scr, caw_ref, a0, a1, hist), axis=0)
    state_ref[...] = a1.reshape(state_ref.shape)
    acc = jax.nn.sigmoid(p[GA]) * (p[B_] * conv)
    acc = acc + jax.nn.sigmoid(p[GB]) * (p[U] * s)
    mixab_ref[...] = acc.reshape(2, rows, D)
    gm_ref[...] = jax.nn.sigmoid(p[GM]).reshape(2, rows, D)
    q_ref[...] = p[Q].reshape(2, rows, D)


def _mixer_sample(l, xs, norm_mix_g, w_in, conv_a_w, sg_ln_g, sgc, sgb, cache_a):
    rows = xs.shape[1]
    tm = rows // 2
    wspec = lambda shape: _resident((None,) + shape, lambda i: (l,) + (0,) * len(shape))
    tok = pl.BlockSpec((2, tm, D), lambda i: (0, i, 0))
    hist = pl.BlockSpec((None, tm // 2, 2, D), lambda i: (l, i, 0, 0))
    out_f = jax.ShapeDtypeStruct((2, rows, D), F32)
    return pl.pallas_call(
        _mixer_sample_kernel,
        grid=(rows // tm,),
        in_specs=[tok, wspec((1, D)), wspec((D, D_IN)), wspec((3, D)), wspec((1, D)),
                  wspec((7, SUBLANES, D)), wspec((2, SUBLANES, D)), hist],
        out_specs=[tok] * 4 + [pl.BlockSpec((tm // 2, 2, D), lambda i: (i, 0, 0))],
        out_shape=[out_f] * 4 + [jax.ShapeDtypeStruct((rows // 2, 2, D), F32)],
        scratch_shapes=[pltpu.VMEM((tm + 2 * SUBLANES, D), F32)],
        compiler_params=pltpu.CompilerParams(
            dimension_semantics=("arbitrary",), vmem_limit_bytes=VMEM_LIMIT),
        name="mixer_sample",
    )(xs, norm_mix_g, w_in, conv_a_w, sg_ln_g, sgc, sgb, cache_a)


def _ffn_sample_kernel(*refs, final):
    if final:
        (x_ref, mix_ref, wo_ref, g_ref, wup_ref, cw_ref, cb_ref, wdn_ref, hist_ref, gfin_ref,
         xo_ref, state_ref, y_ref, scr) = refs
    else:
        (x_ref, mix_ref, wo_ref, g_ref, wup_ref, cw_ref, cb_ref, wdn_ref, hist_ref,
         xo_ref, state_ref, scr) = refs
    rows = x_ref.shape[1]
    _zero_margins(scr, rows)
    x1 = x_ref[...].reshape(2 * rows, D) + _dot(
        mix_ref[...].reshape(2 * rows, D).astype(BF16), wo_ref[...])
    h = _dot(_rms(x1, g_ref[...]).astype(BF16), wup_ref[...])
    h0, h1 = h[:rows], h[rows:]
    hist = hist_ref[...].reshape(rows, D_FF2)
    conv = jnp.concatenate(_sample_conv(scr, cw_ref, h0, h1, hist), axis=0)
    state_ref[...] = h1.reshape(state_ref.shape)
    x2 = x1 + _dot(_silu_gate(conv + cb_ref[...]), wdn_ref[...])
    xo_ref[...] = x2.reshape(2, rows, D)
    if final:
        y_ref[...] = _rms(x2, gfin_ref[...]).reshape(2, rows, D)


def _ffn_sample(l, final, xs, mix, w_o, norm_ffn_g, w_up, conv_f_w, conv_f_b, w_down, cache_f,
                norm_final_g):
    rows = xs.shape[1]
    tm = rows // 2
    tok = pl.BlockSpec((2, tm, D), lambda i: (0, i, 0))
    wspec = lambda shape: _resident((None,) + shape, lambda i: (l,) + (0,) * len(shape))
    x_shape = jax.ShapeDtypeStruct((2, rows, D), F32)
    in_specs = [tok, tok, wspec((D, D)), wspec((1, D)), wspec((D, D_FF2)), wspec((3, D_FF2)),
                wspec((1, D_FF2)), wspec((D_FF, D)),
                pl.BlockSpec((None, tm // 2, 2, D_FF2), lambda i: (l, i, 0, 0))]
    args = [xs, mix, w_o, norm_ffn_g, w_up, conv_f_w, conv_f_b, w_down, cache_f]
    out_specs = [tok, pl.BlockSpec((tm // 2, 2, D_FF2), lambda i: (i, 0, 0))]
    out_shape = [x_shape, jax.ShapeDtypeStruct((rows // 2, 2, D_FF2), F32)]
    if final:
        in_specs.append(_resident((1, D), lambda i: (0, 0)))
        args.append(norm_final_g)
        out_specs.append(tok)
        out_shape.append(x_shape)
    return pl.pallas_call(
        functools.partial(_ffn_sample_kernel, final=final),
        grid=(rows // tm,),
        in_specs=in_specs,
        out_specs=out_specs,
        out_shape=out_shape,
        scratch_shapes=[pltpu.VMEM((tm + 2 * SUBLANES, D_FF2), F32)],
        compiler_params=pltpu.CompilerParams(
            dimension_semantics=("arbitrary",), vmem_limit_bytes=VMEM_LIMIT),
        name="ffn_sample",
    )(*args)


def _sample_gating_tables(sg_w, sg_b):
    w = sg_w[:, :, :SAMPLE_STEPS, :SAMPLE_STEPS]
    zero = jnp.zeros_like(w[..., 0, 0])
    by_parity = [(w[..., 0, 0], w[..., 1, 1]), (zero, w[..., 1, 0]), (w[..., 2, 0], w[..., 3, 1]),
                 (w[..., 2, 1], zero), (zero, w[..., 3, 0]), (w[..., 2, 2], w[..., 3, 3]),
                 (zero, w[..., 3, 2])]

    def table(pairs):
        t = jnp.stack([jnp.stack(p, axis=1) for p in pairs], axis=1)
        t = jnp.tile(t, (1, 1, SUBLANES // 2, 1))
        return jnp.repeat(t, SG_COLS, axis=3)

    b = sg_b[:, :, :SAMPLE_STEPS]
    return table(by_parity), table([(b[..., 0], b[..., 1]), (b[..., 2], b[..., 3])])


def kernel(x_prompt, x_sample, mem_prompt, cache_conv_a, cache_conv_ffn, cache_mem_k, cache_mem_v, norm_mix_g, w_in, conv_a_w, sg_ln_g, sg_w, sg_b, norm_mem_g, w_k, w_v, w_o, norm_ffn_g, w_up, conv_f_w, conv_f_b, w_down, norm_final_g):
    depth = w_in.shape[0]
    bp = x_prompt.shape[0]
    bs, ts, _ = x_sample.shape
    assert ts == SAMPLE_STEPS

    w_in_b = w_in.astype(BF16)
    w_o_b = w_o.astype(BF16)
    w_up_b = w_up.astype(BF16)
    w_down_b = w_down.astype(BF16)
    wkv_b = jnp.concatenate([w_k, w_v], axis=-1).astype(BF16)

    nmg = norm_mix_g[:, None, :]
    lng = sg_ln_g[:, None, :]
    nfg = norm_ffn_g[:, None, :]
    nmemg = norm_mem_g[:, None, :]
    cfb = conv_f_b[:, None, :]
    gfin = norm_final_g[None, :]
    sg_b_t = jnp.swapaxes(sg_b, 1, 2)
    sgc, sgbias = _sample_gating_tables(sg_w, sg_b)
    to_halves = lambda a: jnp.swapaxes(a.reshape(bs, 2, 2, D), 0, 1).reshape(2, bs * 2, D)
    from_halves = lambda a: jnp.swapaxes(a.reshape(2, bs, 2, D), 0, 1).reshape(bs, ts, D)
    xs = to_halves(x_sample)

    k_f, v_f, k_b, v_b = _mem_kv(mem_prompt.reshape(bp * N_MEM, D), nmemg, wkv_b)
    k_b = k_b.reshape(depth * bp, N_MEM, D)
    v_b = v_b.reshape(depth * bp, N_MEM, D)

    xp = x_prompt
    pa, pf, sa, sf, sv = [], [], [], [], []
    ys = None
    for l in range(depth):
        final = l == depth - 1
        mixab, gm, q, vn, a_state_s = _mixer_sample(l, xs, nmg, w_in_b, conv_a_w, lng, sgc,
                                                    sgbias, cache_conv_a)
        mix, a_state, mix_s = _mixer_prompt(l, xp, nmg, w_in_b, conv_a_w, lng, sg_w, sg_b_t,
                                            k_b, v_b, q, gm, mixab, cache_mem_k, cache_mem_v)
        xp, f_state = _ffn_prompt(l, final, xp, mix, w_o_b, nfg, w_up_b, conv_f_w, cfb, w_down_b,
                                  gfin)
        pa.append(a_state)
        pf.append(f_state)
        outs = _ffn_sample(l, final, xs, mix_s, w_o_b, nfg, w_up_b, conv_f_w, cfb, w_down_b,
                           cache_conv_ffn, gfin)
        xs, f_state_s = outs[0], outs[1]
        if final:
            ys = outs[2]
        sa.append(a_state_s)
        sf.append(f_state_s)
        sv.append(from_halves(vn))

    kv_shape = (depth, bp, N_MEM, HEADS, HEAD_DIM)
    return (xp, from_halves(ys),
            jnp.stack(pa), jnp.stack(pf), k_f.reshape(kv_shape), v_f.reshape(kv_shape),
            jnp.stack(sa), jnp.stack(sf), jnp.stack(sv))
```

```python
import functools

import jax
import jax.numpy as jnp
from jax import lax
from jax.experimental import pallas as pl
from jax.experimental.pallas import tpu as pltpu

F32 = jnp.float32
BF16 = jnp.bfloat16

D = 1024
D_IN = 9 * D
D_FF = 2816
D_FF2 = 2 * D_FF
N_MEM = 256
HEADS = 4
HEAD_DIM = D // HEADS
SG_GROUPS = 4
SG_COLS = D // SG_GROUPS
SG_CHUNK = 128
EPS = 1e-6
H, C, B_, U, V, Q, GA, GB, GM = range(9)
PROJ_GROUPS = [(V, Q, U), (GB, C, H), (B_, GA, GM)]

V7X_VMEM_BYTES = 64 * 1024 * 1024
VMEM_LIMIT = V7X_VMEM_BYTES - 4 * 1024 * 1024
SUBLANES = 8

SUBS_MIX = (256, 256)
SUBS_FFN = (256, 256, 256, 256)
TM_KV = 512
ATT_BB = 4
MASKED = -1e30


def _rms(x, g):
    y = x * lax.rsqrt(jnp.mean(x * x, axis=-1, keepdims=True) + EPS)
    return y * g


def _layernorm(x, g):
    xc = x - jnp.mean(x, axis=-1, keepdims=True)
    return xc * lax.rsqrt(jnp.mean(xc * xc, axis=-1, keepdims=True) + EPS) * g


def _dot(a, b):
    return jnp.dot(a, b, preferred_element_type=F32)


def _resident(shape, index_map):
    return pl.BlockSpec(shape, index_map, pipeline_mode=pl.Buffered(1))


def _causal_conv(w_ref, hist, cur):
    n = hist.shape[0]
    ext = jnp.concatenate([hist, cur], axis=0)
    x1 = pltpu.roll(ext, 1, axis=0)[n:, :]
    x2 = pltpu.roll(ext, 2, axis=0)[n:, :]
    return w_ref[0:1, :] * x2 + w_ref[1:2, :] * x1 + w_ref[2:3, :] * cur


def _silu_gate(hc):
    return (jax.nn.silu(hc[:, D_FF:]) * hc[:, :D_FF]).astype(BF16)


def _kv_kernel(mem_ref, g_ref, wkv_ref, k_ref, v_ref, kb_ref, vb_ref):
    m = _rms(mem_ref[...], g_ref[...]).astype(BF16)
    k = _dot(m, wkv_ref[:, :D])
    v = _dot(m, wkv_ref[:, D:])
    for h in range(HEADS):
        hc = slice(h * HEAD_DIM, (h + 1) * HEAD_DIM)
        k_ref[:, h, :] = k[:, hc]
        v_ref[:, h, :] = v[:, hc]
    kb_ref[...] = k.astype(BF16)
    vb_ref[...] = v.astype(BF16)


def _mem_kv(mem2d, norm_mem_g, wkv):
    depth = wkv.shape[0]
    rows = mem2d.shape[0]
    out_f = jax.ShapeDtypeStruct((depth, rows, HEADS, HEAD_DIM), F32)
    out_b = jax.ShapeDtypeStruct((depth, rows, D), BF16)
    oblk = pl.BlockSpec((None, TM_KV, D), lambda l, i: (l, i, 0))
    oblk_f = pl.BlockSpec((None, TM_KV, HEADS, HEAD_DIM), lambda l, i: (l, i, 0, 0))
    return pl.pallas_call(
        _kv_kernel,
        grid=(depth, rows // TM_KV),
        in_specs=[
            pl.BlockSpec((TM_KV, D), lambda l, i: (i, 0)),
            pl.BlockSpec((None, 1, D), lambda l, i: (l, 0, 0)),
            pl.BlockSpec((None, D, 2 * D), lambda l, i: (l, 0, 0)),
        ],
        out_specs=[oblk_f, oblk_f, oblk, oblk],
        out_shape=[out_f, out_f, out_b, out_b],
        compiler_params=pltpu.CompilerParams(
            dimension_semantics=("arbitrary", "arbitrary"), vmem_limit_bytes=VMEM_LIMIT),
        name="mem_kv",
    )(mem2d, norm_mem_g, wkv)


def _attn_steps(q_ref, k_ref, v_ref, gm_ref, mixab_ref, mix_ref):
    hb = q_ref.shape[1]
    nb = 2 * hb
    nq = HEADS * nb
    nk = N_MEM * HEADS
    tile = HEAD_DIM
    row = lax.broadcasted_iota(jnp.int32, (nq, nk), 0)
    col = lax.broadcasted_iota(jnp.int32, (nq, nk), 1)
    same_head = (row >> (nb.bit_length() - 1)) == (col & (HEADS - 1))
    row_b = (row & (hb - 1)) >> 1
    out_b = (lax.broadcasted_iota(jnp.int32, (nq, 1), 0) & (hb - 1)) >> 1
    q = q_ref[...].reshape(nb, D)
    q_all = jnp.concatenate([q[:, h * HEAD_DIM:(h + 1) * HEAD_DIM] for h in range(HEADS)],
                            axis=0).astype(BF16)
    st = {"y": jnp.zeros((nq, HEAD_DIM), F32)}

    def score_tile(b, j):
        if j == 0:
            st["k"] = k_ref[b].reshape(nk, HEAD_DIM).astype(BF16)
            st["sc"] = []
        st["sc"].append(lax.dot_general(q_all, st["k"][j * tile:(j + 1) * tile],
                                        (((1,), (1,)), ((), ())), preferred_element_type=F32))
        if j == nk // tile - 1:
            sc = jnp.concatenate(st["sc"], axis=1) * (HEAD_DIM ** -0.5)
            sc = jnp.where(same_head & (row_b == b), sc, MASKED)
            e = jnp.exp(sc - jnp.max(sc, axis=-1, keepdims=True))
            st["p", b] = (e * (1.0 / jnp.sum(e, axis=-1, keepdims=True))).astype(BF16)

    def value_tile(b, j):
        if j == 0:
            st["v"] = v_ref[b].reshape(nk, HEAD_DIM).astype(BF16)
            st["o"] = jnp.zeros((nq, HEAD_DIM), F32)
        st["o"] = st["o"] + _dot(st["p", b][:, j * tile:(j + 1) * tile],
                                 st["v"][j * tile:(j + 1) * tile])
        if j == nk // tile - 1:
            st["y"] = jnp.where(out_b == b, st["o"], st["y"])
            if b == ATT_BB - 1:
                y = jnp.concatenate([st["y"][h * nb:(h + 1) * nb] for h in range(HEADS)], axis=1)
                mix_ref[...] = mixab_ref[...] + gm_ref[...] * y.reshape(2, hb, D)

    tiles = [(b, j) for b in range(ATT_BB) for j in range(nk // tile)]
    return ([functools.partial(score_tile, b, j) for b, j in tiles]
            + [functools.partial(value_tile, b, j) for b, j in tiles])


def _mixer_kernel(x_ref, nm_ref, win_ref, caw_ref, lng_ref, sgw_ref, sgb_ref, kb_ref, vb_ref,
                  sq_ref, sgm_ref, smixab_ref, sk_ref, sv_ref,
                  mix_ref, state_ref, smix_ref, a_scr, *, subs, prev_rows):
    @pl.when(pl.program_id(1) == 0)
    def _():
        a_scr[...] = jnp.zeros((prev_rows, D), F32)

    ri = lax.broadcasted_iota(jnp.int32, (SG_CHUNK, SG_CHUNK), 0)
    ci = lax.broadcasted_iota(jnp.int32, (SG_CHUNK, SG_CHUNK), 1)
    sgw = [jnp.where(ci <= ri, sgw_ref[g], 0.0).astype(BF16) for g in range(SG_GROUPS)]
    offs = [sum(subs[:t]) for t in range(len(subs))]
    hist = [a_scr[...]]

    sample_steps = _attn_steps(sq_ref, sk_ref, sv_ref, sgm_ref, smixab_ref, smix_ref)
    chunk = 2 * SG_CHUNK
    for t in range(len(subs)):
        rs = slice(offs[t], offs[t] + subs[t])
        z = _rms(x_ref[rs, :], nm_ref[...]).astype(BF16)

        def proj(i, ride_along=False):
            cols = []
            for j in range(i * D, (i + 1) * D, chunk):
                cols.append(_dot(z, win_ref[:, j:j + chunk]))
                if ride_along and sample_steps:
                    sample_steps.pop(0)()
            return jnp.concatenate(cols, axis=1)

        v, q, u = proj(V), proj(Q), proj(U)
        vnb = _layernorm(v, lng_ref[...]).astype(BF16)
        s_rows = []
        for c in range(subs[t] // SG_CHUNK):
            s_cols = []
            for g in range(SG_GROUPS):
                vg = vnb[c * SG_CHUNK:(c + 1) * SG_CHUNK, g * SG_COLS:(g + 1) * SG_COLS]
                s_cols.append(_dot(sgw[g], vg) + sgb_ref[:, g:g + 1])
            s_rows.append(jnp.concatenate(s_cols, axis=1))
        s = jnp.concatenate(s_rows, axis=0)
        scores = [
            lax.dot_general(q[:, h * HEAD_DIM:(h + 1) * HEAD_DIM].astype(BF16),
                            kb_ref[:, h * HEAD_DIM:(h + 1) * HEAD_DIM],
                            (((1,), (1,)), ((), ())), preferred_element_type=F32)
            for h in range(HEADS)]

        gb, c_, h_ = proj(GB, True), proj(C, True), proj(H, True)
        ubs = jax.nn.sigmoid(gb) * u * s
        a = c_ * h_
        conv = _causal_conv(caw_ref, hist[0], a)
        hist[0] = a[subs[t] - prev_rows:, :]
        o_cols = []
        for h in range(HEADS):
            sc = scores[h] * (HEAD_DIM ** -0.5)
            e = jnp.exp(sc - jnp.max(sc, axis=-1, keepdims=True))
            p = e * (1.0 / jnp.sum(e, axis=-1, keepdims=True))
            o_cols.append(_dot(p.astype(BF16), vb_ref[:, h * HEAD_DIM:(h + 1) * HEAD_DIM]))
        y_m = jnp.concatenate(o_cols, axis=1)

        b_, ga, gm = proj(B_, True), proj(GA, True), proj(GM)
        acc = jax.nn.sigmoid(ga) * (b_ * conv) + ubs + jax.nn.sigmoid(gm) * y_m
        mix_ref[rs, :] = acc.astype(BF16)
    for step in sample_steps:
        step()

    a_scr[...] = hist[0]

    @pl.when(pl.program_id(1) == pl.num_programs(1) - 1)
    def _():
        state_ref[...] = hist[0][prev_rows - 2:, :]


def _mixer_prompt(l, x, norm_mix_g, w_in, conv_a_w, sg_ln_g, sg_w, sg_b_t, kb, vb,
                  sq, sgm, smixab, cache_k, cache_v):
    bsz, t, _ = x.shape
    tm = sum(SUBS_MIX)
    assert len(SUBS_MIX) >= 2
    n_t = t // tm
    prev_rows = SUBLANES
    hb = 2 * ATT_BB
    assert hb & (hb - 1) == 0 and sq.shape[1] == hb * bsz * n_t
    wspec = lambda shape: _resident((None,) + shape, lambda b, i: (l,) + (0,) * len(shape))
    small = pl.BlockSpec((2, hb, D), lambda b, i: (0, b * n_t + i, 0))
    big = pl.BlockSpec((None, ATT_BB, N_MEM, HEADS, HEAD_DIM),
                       lambda b, i: (l, b * n_t + i, 0, 0, 0))
    return pl.pallas_call(
        functools.partial(_mixer_kernel, subs=SUBS_MIX, prev_rows=prev_rows),
        grid=(bsz, t // tm),
        in_specs=[
            pl.BlockSpec((None, tm, D), lambda b, i: (b, i, 0)),
            wspec((1, D)),
            wspec((D, D_IN)),
            wspec((3, D)),
            wspec((1, D)),
            wspec((SG_GROUPS, SG_CHUNK, SG_CHUNK)),
            wspec((SG_CHUNK, SG_GROUPS)),
            pl.BlockSpec((None, N_MEM, D), lambda b, i: (l * bsz + b, 0, 0)),
            pl.BlockSpec((None, N_MEM, D), lambda b, i: (l * bsz + b, 0, 0)),
            small, small, small, big, big,
        ],
        out_specs=[
            pl.BlockSpec((None, tm, D), lambda b, i: (b, i, 0)),
            pl.BlockSpec((None, 2, D), lambda b, i: (b, 0, 0)),
            small,
        ],
        out_shape=[
            jax.ShapeDtypeStruct((bsz, t, D), BF16),
            jax.ShapeDtypeStruct((bsz, 2, D), F32),
            jax.ShapeDtypeStruct(sq.shape, F32),
        ],
        scratch_shapes=[pltpu.VMEM((prev_rows, D), F32)],
        compiler_params=pltpu.CompilerParams(
            dimension_semantics=("arbitrary", "arbitrary"), vmem_limit_bytes=VMEM_LIMIT),
        name="mixer_prompt",
    )(x, norm_mix_g, w_in, conv_a_w, sg_ln_g, sg_w, sg_b_t, kb, vb, sq, sgm, smixab, cache_k, cache_v)


def _ffn_kernel(*refs, final, subs, prev_rows):
    if final:
        (x_ref, mix_ref, wo_ref, g_ref, wup_ref, cw_ref, cb_ref, wdn_ref, gfin_ref,
         out_ref, state_ref, h_scr) = refs
    else:
        (x_ref, mix_ref, wo_ref, g_ref, wup_ref, cw_ref, cb_ref, wdn_ref,
         out_ref, state_ref, h_scr) = refs

    @pl.when(pl.program_id(1) == 0)
    def _():
        h_scr[...] = jnp.zeros((prev_rows, D_FF2), F32)

    offs = [sum(subs[:t]) for t in range(len(subs))]
    slices = [slice(o, o + n) for o, n in zip(offs, subs)]
    x1 = [x_ref[rs, :] + _dot(mix_ref[rs, :], wo_ref[...]) for rs in slices]
    h = [_dot(_rms(x1[t], g_ref[...]).astype(BF16), wup_ref[...]) for t in range(len(subs))]
    hist = h_scr[...]
    for t, rs in enumerate(slices):
        hc = _causal_conv(cw_ref, hist, h[t]) + cb_ref[...]
        hist = h[t][subs[t] - prev_rows:, :]
        x2 = x1[t] + _dot(_silu_gate(hc), wdn_ref[...])
        out_ref[rs, :] = _rms(x2, gfin_ref[...]) if final else x2
    h_scr[...] = hist

    @pl.when(pl.program_id(1) == pl.num_programs(1) - 1)
    def _():
        state_ref[...] = hist[prev_rows - 2:, :]


def _ffn_prompt(l, final, x, mix, w_o, norm_ffn_g, w_up, conv_f_w, conv_f_b, w_down, norm_final_g):
    bsz, t, _ = x.shape
    tm, prev_rows = sum(SUBS_FFN), SUBLANES
    tok = pl.BlockSpec((None, tm, D), lambda b, i: (b, i, 0))
    wspec = lambda shape: _resident((None,) + shape, lambda b, i: (l,) + (0,) * len(shape))
    x_shape = jax.ShapeDtypeStruct((bsz, t, D), F32)
    in_specs = [tok, tok, wspec((D, D)), wspec((1, D)), wspec((D, D_FF2)), wspec((3, D_FF2)),
                wspec((1, D_FF2)), wspec((D_FF, D))]
    args = [x, mix, w_o, norm_ffn_g, w_up, conv_f_w, conv_f_b, w_down]
    out_specs = [tok, pl.BlockSpec((None, 2, D_FF2), lambda b, i: (b, 0, 0))]
    out_shape = [x_shape, jax.ShapeDtypeStruct((bsz, 2, D_FF2), F32)]
    if final:
        in_specs.append(_resident((1, D), lambda b, i: (0, 0)))
        args.append(norm_final_g)
    return pl.pallas_call(
        functools.partial(_ffn_kernel, final=final, subs=SUBS_FFN, prev_rows=prev_rows),
        grid=(bsz, t // tm),
        in_specs=in_specs,
        out_specs=out_specs,
        out_shape=out_shape,
        scratch_shapes=[pltpu.VMEM((prev_rows, D_FF2), F32)],
        compiler_params=pltpu.CompilerParams(
            dimension_semantics=("arbitrary", "arbitrary"), vmem_limit_bytes=VMEM_LIMIT),
        name="ffn_prompt",
    )(*args)


SAMPLE_STEPS = 4


def _put(scr, x):
    scr[SUBLANES:SUBLANES + x.shape[0], :] = x


def _get(scr, rows, k):
    return scr[SUBLANES - k:SUBLANES - k + rows, :]


def _zero_margins(scr, rows):
    scr[0:SUBLANES, :] = jnp.zeros((SUBLANES, scr.shape[1]), F32)
    scr[SUBLANES + rows:, :] = jnp.zeros((SUBLANES, scr.shape[1]), F32)


def _sample_conv(scr, w_ref, cur0, cur1, hist):
    rows = cur0.shape[0]
    first = (lax.broadcasted_iota(jnp.int32, (rows, 1), 0) & 1) == 0
    _put(scr, hist)
    hist_up = _get(scr, rows, -1)
    _put(scr, cur0)
    cur0_dn = _get(scr, rows, 1)
    cur0_up = _get(scr, rows, -1)
    _put(scr, cur1)
    cur1_dn = _get(scr, rows, 1)
    w0, w1, w2 = w_ref[0:1, :], w_ref[1:2, :], w_ref[2:3, :]
    c0 = w0 * hist + w1 * jnp.where(first, hist_up, cur0_dn) + w2 * cur0
    c1 = w0 * cur0 + w1 * jnp.where(first, cur0_up, cur1_dn) + w2 * cur1
    return c0, c1


def _mixer_sample_kernel(x_ref, nm_ref, win_ref, caw_ref, lng_ref, sgc_ref, sgb_ref, hist_ref,
                         mixab_ref, gm_ref, q_ref, vn_ref, state_ref, scr):
    rows = x_ref.shape[1]
    halves = lambda a: (a[:rows], a[rows:])
    _zero_margins(scr, rows)
    z = _rms(x_ref[...].reshape(2 * rows, D), nm_ref[...]).astype(BF16)
    p = {}
    for group in PROJ_GROUPS:
        for i in group:
            p[i] = _dot(z, win_ref[:, i * D:(i + 1) * D])

    vn0, vn1 = halves(_layernorm(p[V], lng_ref[...]))
    vn_ref[0] = vn0
    vn_ref[1] = vn1
    tiles = lambda a: a.reshape(rows // SUBLANES, SUBLANES, D)
    coef = lambda k: sgc_ref[k][None]
    _put(scr, vn0)
    vn0_dn, vn0_up = tiles(_get(scr, rows, 1)), tiles(_get(scr, rows, -1))
    _put(scr, vn1)
    vn1_dn = tiles(_get(scr, rows, 1))
    s0 = coef(0) * tiles(vn0) + coef(1) * vn0_dn + sgb_ref[0][None]
    s1 = (coef(2) * tiles(vn0) + coef(3) * vn0_up + coef(4) * vn0_dn
          + coef(5) * tiles(vn1) + coef(6) * vn1_dn + sgb_ref[1][None])
    s = jnp.concatenate([s0.reshape(rows, D), s1.reshape(rows, D)], axis=0)

    a0, a1 = halves(p[C] * p[H])
    hist = hist_ref[...].reshape(rows, D)
    conv = jnp.concatenate(_sample_conv(scr, caw_ref, a0, a1, hist), axis=0)
    state_ref[...] = a1.reshape(state_ref.shape)
    acc = jax.nn.sigmoid(p[GA]) * (p[B_] * conv)
    acc = acc + jax.nn.sigmoid(p[GB]) * (p[U] * s)
    mixab_ref[...] = acc.reshape(2, rows, D)
    gm_ref[...] = jax.nn.sigmoid(p[GM]).reshape(2, rows, D)
    q_ref[...] = p[Q].reshape(2, rows, D)


def _mixer_sample(l, xs, norm_mix_g, w_in, conv_a_w, sg_ln_g, sgc, sgb, cache_a):
    rows = xs.shape[1]
    tm = rows // 2
    wspec = lambda shape: _resident((None,) + shape, lambda i: (l,) + (0,) * len(shape))
    tok = pl.BlockSpec((2, tm, D), lambda i: (0, i, 0))
    hist = pl.BlockSpec((None, tm // 2, 2, D), lambda i: (l, i, 0, 0))
    out_f = jax.ShapeDtypeStruct((2, rows, D), F32)
    return pl.pallas_call(
        _mixer_sample_kernel,
        grid=(rows // tm,),
        in_specs=[tok, wspec((1, D)), wspec((D, D_IN)), wspec((3, D)), wspec((1, D)),
                  wspec((7, SUBLANES, D)), wspec((2, SUBLANES, D)), hist],
        out_specs=[tok] * 4 + [pl.BlockSpec((tm // 2, 2, D), lambda i: (i, 0, 0))],
        out_shape=[out_f] * 4 + [jax.ShapeDtypeStruct((rows // 2, 2, D), F32)],
        scratch_shapes=[pltpu.VMEM((tm + 2 * SUBLANES, D), F32)],
        compiler_params=pltpu.CompilerParams(
            dimension_semantics=("arbitrary",), vmem_limit_bytes=VMEM_LIMIT),
        name="mixer_sample",
    )(xs, norm_mix_g, w_in, conv_a_w, sg_ln_g, sgc, sgb, cache_a)


def _ffn_sample_kernel(*refs, final):
    if final:
        (x_ref, mix_ref, wo_ref, g_ref, wup_ref, cw_ref, cb_ref, wdn_ref, hist_ref, gfin_ref,
         xo_ref, state_ref, y_ref, scr) = refs
    else:
        (x_ref, mix_ref, wo_ref, g_ref, wup_ref, cw_ref, cb_ref, wdn_ref, hist_ref,
         xo_ref, state_ref, scr) = refs
    rows = x_ref.shape[1]
    _zero_margins(scr, rows)
    x1 = x_ref[...].reshape(2 * rows, D) + _dot(
        mix_ref[...].reshape(2 * rows, D).astype(BF16), wo_ref[...])
    h = _dot(_rms(x1, g_ref[...]).astype(BF16), wup_ref[...])
    h0, h1 = h[:rows], h[rows:]
    hist = hist_ref[...].reshape(rows, D_FF2)
    conv = jnp.concatenate(_sample_conv(scr, cw_ref, h0, h1, hist), axis=0)
    state_ref[...] = h1.reshape(state_ref.shape)
    x2 = x1 + _dot(_silu_gate(conv + cb_ref[...]), wdn_ref[...])
    xo_ref[...] = x2.reshape(2, rows, D)
    if final:
        y_ref[...] = _rms(x2, gfin_ref[...]).reshape(2, rows, D)


def _ffn_sample(l, final, xs, mix, w_o, norm_ffn_g, w_up, conv_f_w, conv_f_b, w_down, cache_f,
                norm_final_g):
    rows = xs.shape[1]
    tm = rows // 2
    tok = pl.BlockSpec((2, tm, D), lambda i: (0, i, 0))
    wspec = lambda shape: _resident((None,) + shape, lambda i: (l,) + (0,) * len(shape))
    x_shape = jax.ShapeDtypeStruct((2, rows, D), F32)
    in_specs = [tok, tok, wspec((D, D)), wspec((1, D)), wspec((D, D_FF2)), wspec((3, D_FF2)),
                wspec((1, D_FF2)), wspec((D_FF, D)),
                pl.BlockSpec((None, tm // 2, 2, D_FF2), lambda i: (l, i, 0, 0))]
    args = [xs, mix, w_o, norm_ffn_g, w_up, conv_f_w, conv_f_b, w_down, cache_f]
    out_specs = [tok, pl.BlockSpec((tm // 2, 2, D_FF2), lambda i: (i, 0, 0))]
    out_shape = [x_shape, jax.ShapeDtypeStruct((rows // 2, 2, D_FF2), F32)]
    if final:
        in_specs.append(_resident((1, D), lambda i: (0, 0)))
        args.append(norm_final_g)
        out_specs.append(tok)
        out_shape.append(x_shape)
    return pl.pallas_call(
        functools.partial(_ffn_sample_kernel, final=final),
        grid=(rows // tm,),
        in_specs=in_specs,
        out_specs=out_specs,
        out_shape=out_shape,
        scratch_shapes=[pltpu.VMEM((tm + 2 * SUBLANES, D_FF2), F32)],
        compiler_params=pltpu.CompilerParams(
            dimension_semantics=("arbitrary",), vmem_limit_bytes=VMEM_LIMIT),
        name="ffn_sample",
    )(*args)


def _sample_gating_tables(sg_w, sg_b):
    w = sg_w[:, :, :SAMPLE_STEPS, :SAMPLE_STEPS]
    zero = jnp.zeros_like(w[..., 0, 0])
    by_parity = [(w[..., 0, 0], w[..., 1, 1]), (zero, w[..., 1, 0]), (w[..., 2, 0], w[..., 3, 1]),
                 (w[..., 2, 1], zero), (zero, w[..., 3, 0]), (w[..., 2, 2], w[..., 3, 3]),
                 (zero, w[..., 3, 2])]

    def table(pairs):
        t = jnp.stack([jnp.stack(p, axis=1) for p in pairs], axis=1)
        t = jnp.tile(t, (1, 1, SUBLANES // 2, 1))
        return jnp.repeat(t, SG_COLS, axis=3)

    b = sg_b[:, :, :SAMPLE_STEPS]
    return table(by_parity), table([(b[..., 0], b[..., 1]), (b[..., 2], b[..., 3])])


def kernel(x_prompt, x_sample, mem_prompt, cache_conv_a, cache_conv_ffn, cache_mem_k, cache_mem_v, norm_mix_g, w_in, conv_a_w, sg_ln_g, sg_w, sg_b, norm_mem_g, w_k, w_v, w_o, norm_ffn_g, w_up, conv_f_w, conv_f_b, w_down, norm_final_g):
    depth = w_in.shape[0]
    bp = x_prompt.shape[0]
    bs, ts, _ = x_sample.shape
    assert ts == SAMPLE_STEPS

    w_in_b = w_in.astype(BF16)
    w_o_b = w_o.astype(BF16)
    w_up_b = w_up.astype(BF16)
    w_down_b = w_down.astype(BF16)
    wkv_b = jnp.concatenate([w_k, w_v], axis=-1).astype(BF16)

    nmg = norm_mix_g[:, None, :]
    lng = sg_ln_g[:, None, :]
    nfg = norm_ffn_g[:, None, :]
    nmemg = norm_mem_g[:, None, :]
    cfb = conv_f_b[:, None, :]
    gfin = norm_final_g[None, :]
    sg_b_t = jnp.swapaxes(sg_b, 1, 2)
    sgc, sgbias = _sample_gating_tables(sg_w, sg_b)
    to_halves = lambda a: jnp.swapaxes(a.reshape(bs, 2, 2, D), 0, 1).reshape(2, bs * 2, D)
    from_halves = lambda a: jnp.swapaxes(a.reshape(2, bs, 2, D), 0, 1).reshape(bs, ts, D)
    xs = to_halves(x_sample)

    k_f, v_f, k_b, v_b = _mem_kv(mem_prompt.reshape(bp * N_MEM, D), nmemg, wkv_b)
    k_b = k_b.reshape(depth * bp, N_MEM, D)
    v_b = v_b.reshape(depth * bp, N_MEM, D)

    xp = x_prompt
    pa, pf, sa, sf, sv = [], [], [], [], []
    ys = None
    for l in range(depth):
        final = l == depth - 1
        mixab, gm, q, vn, a_state_s = _mixer_sample(l, xs, nmg, w_in_b, conv_a_w, lng, sgc,
                                                    sgbias, cache_conv_a)
        mix, a_state, mix_s = _mixer_prompt(l, xp, nmg, w_in_b, conv_a_w, lng, sg_w, sg_b_t,
                                            k_b, v_b, q, gm, mixab, cache_mem_k, cache_mem_v)
        xp, f_state = _ffn_prompt(l, final, xp, mix, w_o_b, nfg, w_up_b, conv_f_w, cfb, w_down_b,
                                  gfin)
        pa.append(a_state)
        pf.append(f_state)
        outs = _ffn_sample(l, final, xs, mix_s, w_o_b, nfg, w_up_b, conv_f_w, cfb, w_down_b,
                           cache_conv_ffn, gfin)
        xs, f_state_s = outs[0], outs[1]
        if final:
            ys = outs[2]
        sa.append(a_state_s)
        sf.append(f_state_s)
        sv.append(from_halves(vn))

    kv_shape = (depth, bp, N_MEM, HEADS, HEAD_DIM)
    return (xp, from_halves(ys),
            jnp.stack(pa), jnp.stack(pf), k_f.reshape(kv_shape), v_f.reshape(kv_shape),
            jnp.stack(sa), jnp.stack(sf), jnp.stack(sv))
```

```python
import functools

import jax
import jax.numpy as jnp
from jax import lax
from jax.experimental import pallas as pl
from jax.experimental.pallas import tpu as pltpu

F32 = jnp.float32
BF16 = jnp.bfloat16

D = 1024
D_IN = 9 * D
D_FF = 2816
D_FF2 = 2 * D_FF
N_MEM = 256
HEADS = 4
HEAD_DIM = D // HEADS
SG_GROUPS = 4
SG_COLS = D // SG_GROUPS
SG_CHUNK = 128
EPS = 1e-6
H, C, B_, U, V, Q, GA, GB, GM = range(9)
PROJ_GROUPS = [(V, Q, U), (GB, C, H), (B_, GA, GM)]

V7X_VMEM_BYTES = 64 * 1024 * 1024
VMEM_LIMIT = V7X_VMEM_BYTES - 4 * 1024 * 1024
SUBLANES = 8

SUBS_MIX = (256, 256)
SUBS_FFN = (256, 256, 256, 256)
TM_KV = 512
ATT_BB = 4
MASKED = -1e30


def _rms(x, g):
    y = x * lax.rsqrt(jnp.mean(x * x, axis=-1, keepdims=True) + EPS)
    return y * g


def _layernorm(x, g):
    xc = x - jnp.mean(x, axis=-1, keepdims=True)
    return xc * lax.rsqrt(jnp.mean(xc * xc, axis=-1, keepdims=True) + EPS) * g


def _dot(a, b):
    return jnp.dot(a, b, preferred_element_type=F32)


def _resident(shape, index_map):
    return pl.BlockSpec(shape, index_map, pipeline_mode=pl.Buffered(1))


def _causal_conv(w_ref, hist, cur):
    n = hist.shape[0]
    ext = jnp.concatenate([hist, cur], axis=0)
    x1 = pltpu.roll(ext, 1, axis=0)[n:, :]
    x2 = pltpu.roll(ext, 2, axis=0)[n:, :]
    return w_ref[0:1, :] * x2 + w_ref[1:2, :] * x1 + w_ref[2:3, :] * cur


def _silu_gate(hc):
    return (jax.nn.silu(hc[:, D_FF:]) * hc[:, :D_FF]).astype(BF16)


def _kv_kernel(mem_ref, g_ref, wkv_ref, win_ref, k_ref, v_ref, kb_ref, vb_ref, winb_ref):
    winb_ref[...] = win_ref[...].astype(BF16)
    m = _rms(mem_ref[...], g_ref[...]).astype(BF16)
    k = _dot(m, wkv_ref[:, :D])
    v = _dot(m, wkv_ref[:, D:])
    for h in range(HEADS):
        hc = slice(h * HEAD_DIM, (h + 1) * HEAD_DIM)
        k_ref[:, h, :] = k[:, hc]
        v_ref[:, h, :] = v[:, hc]
    kb_ref[...] = k.astype(BF16)
    vb_ref[...] = v.astype(BF16)


def _mem_kv(mem2d, norm_mem_g, wkv, w_in):
    depth = wkv.shape[0]
    rows = mem2d.shape[0]
    steps = rows // TM_KV
    win_blk = pl.BlockSpec((None, D // steps, D_IN), lambda l, i: (l, i, 0))
    out_f = jax.ShapeDtypeStruct((depth, rows, HEADS, HEAD_DIM), F32)
    out_b = jax.ShapeDtypeStruct((depth, rows, D), BF16)
    oblk = pl.BlockSpec((None, TM_KV, D), lambda l, i: (l, i, 0))
    oblk_f = pl.BlockSpec((None, TM_KV, HEADS, HEAD_DIM), lambda l, i: (l, i, 0, 0))
    return pl.pallas_call(
        _kv_kernel,
        grid=(depth, rows // TM_KV),
        in_specs=[
            pl.BlockSpec((TM_KV, D), lambda l, i: (i, 0)),
            pl.BlockSpec((None, 1, D), lambda l, i: (l, 0, 0)),
            pl.BlockSpec((None, D, 2 * D), lambda l, i: (l, 0, 0)),
            win_blk,
        ],
        out_specs=[oblk_f, oblk_f, oblk, oblk, win_blk],
        out_shape=[out_f, out_f, out_b, out_b, jax.ShapeDtypeStruct(w_in.shape, BF16)],
        compiler_params=pltpu.CompilerParams(
            dimension_semantics=("arbitrary", "arbitrary"), vmem_limit_bytes=VMEM_LIMIT),
        name="mem_kv",
    )(mem2d, norm_mem_g, wkv, w_in)


def _attn_steps(q_ref, k_ref, v_ref, gm_ref, mixab_ref, mix_ref):
    hb = q_ref.shape[1]
    nb = 2 * hb
    nq = HEADS * nb
    nk = N_MEM * HEADS
    tile = HEAD_DIM
    row = lax.broadcasted_iota(jnp.int32, (nq, nk), 0)
    col = lax.broadcasted_iota(jnp.int32, (nq, nk), 1)
    same_head = (row >> (nb.bit_length() - 1)) == (col & (HEADS - 1))
    row_b = (row & (hb - 1)) >> 1
    out_b = (lax.broadcasted_iota(jnp.int32, (nq, 1), 0) & (hb - 1)) >> 1
    q = q_ref[...].reshape(nb, D)
    q_all = jnp.concatenate([q[:, h * HEAD_DIM:(h + 1) * HEAD_DIM] for h in range(HEADS)],
                            axis=0).astype(BF16)
    st = {"y": jnp.zeros((nq, HEAD_DIM), F32)}

    def score_tile(b, j):
        if j == 0:
            st["k"] = k_ref[b].reshape(nk, HEAD_DIM).astype(BF16)
            st["sc"] = []
        st["sc"].append(lax.dot_general(q_all, st["k"][j * tile:(j + 1) * tile],
                                        (((1,), (1,)), ((), ())), preferred_element_type=F32))
        if j == nk // tile - 1:
            sc = jnp.concatenate(st["sc"], axis=1) * (HEAD_DIM ** -0.5)
            sc = jnp.where(same_head & (row_b == b), sc, MASKED)
            e = jnp.exp(sc - jnp.max(sc, axis=-1, keepdims=True))
            st["p", b] = (e * (1.0 / jnp.sum(e, axis=-1, keepdims=True))).astype(BF16)

    def value_tile(b, j):
        if j == 0:
            st["v"] = v_ref[b].reshape(nk, HEAD_DIM).astype(BF16)
            st["o"] = jnp.zeros((nq, HEAD_DIM), F32)
        st["o"] = st["o"] + _dot(st["p", b][:, j * tile:(j + 1) * tile],
                                 st["v"][j * tile:(j + 1) * tile])
        if j == nk // tile - 1:
            st["y"] = jnp.where(out_b == b, st["o"], st["y"])
            if b == ATT_BB - 1:
                y = jnp.concatenate([st["y"][h * nb:(h + 1) * nb] for h in range(HEADS)], axis=1)
                mix_ref[...] = mixab_ref[...] + gm_ref[...] * y.reshape(2, hb, D)

    tiles = [(b, j) for b in range(ATT_BB) for j in range(nk // tile)]
    return ([functools.partial(score_tile, b, j) for b, j in tiles]
            + [functools.partial(value_tile, b, j) for b, j in tiles])


def _mixer_kernel(x_ref, nm_ref, win_ref, caw_ref, lng_ref, sgw_ref, sgb_ref, kb_ref, vb_ref,
                  sq_ref, sgm_ref, smixab_ref, sk_ref, sv_ref, wo_ref, wup_ref, wdn_ref,
                  mix_ref, state_ref, smix_ref, wob_ref, wupb_ref, wdnb_ref, a_scr,
                  *, subs, prev_rows):
    wob_ref[...] = wo_ref[...].astype(BF16)
    wupb_ref[...] = wup_ref[...].astype(BF16)
    wdnb_ref[...] = wdn_ref[...].astype(BF16)

    @pl.when(pl.program_id(1) == 0)
    def _():
        a_scr[...] = jnp.zeros((prev_rows, D), F32)

    ri = lax.broadcasted_iota(jnp.int32, (SG_CHUNK, SG_CHUNK), 0)
    ci = lax.broadcasted_iota(jnp.int32, (SG_CHUNK, SG_CHUNK), 1)
    sgw = [jnp.where(ci <= ri, sgw_ref[g], 0.0).astype(BF16) for g in range(SG_GROUPS)]
    offs = [sum(subs[:t]) for t in range(len(subs))]
    hist = [a_scr[...]]

    sample_steps = _attn_steps(sq_ref, sk_ref, sv_ref, sgm_ref, smixab_ref, smix_ref)
    chunk = 2 * SG_CHUNK
    n_chunks = D // chunk

    def take(n):
        steps = sample_steps[:n]
        del sample_steps[:n]
        return steps

    def proj(z, i, riders=()):
        riders = list(riders)
        cols = []
        for j in range(i * D, (i + 1) * D, chunk):
            cols.append(_dot(z, win_ref[:, j:j + chunk]))
            if riders:
                riders.pop(0)()
        assert not riders
        return jnp.concatenate(cols, axis=1)

    for step in take(3):
        step()
    for t in range(len(subs)):
        rs = slice(offs[t], offs[t] + subs[t])
        z = _rms(x_ref[rs, :], nm_ref[...]).astype(BF16)
        heads = [slice(h * HEAD_DIM, (h + 1) * HEAD_DIM) for h in range(HEADS)]

        v = proj(z, V, take(n_chunks))
        q = proj(z, Q, take(n_chunks))
        vnb = _layernorm(v, lng_ref[...]).astype(BF16)
        s_part = {}

        def spatial(jobs):
            for c, g in jobs:
                vg = vnb[c * SG_CHUNK:(c + 1) * SG_CHUNK, g * SG_COLS:(g + 1) * SG_COLS]
                s_part[c, g] = _dot(sgw[g], vg) + sgb_ref[:, g:g + 1]

        jobs = [(c, g) for c in range(subs[t] // SG_CHUNK) for g in range(SG_GROUPS)]
        per = -(-len(jobs) // n_chunks)
        u = proj(z, U, [functools.partial(spatial, jobs[k * per:(k + 1) * per])
                        for k in range(n_chunks)])
        s = jnp.concatenate([jnp.concatenate([s_part[c, g] for g in range(SG_GROUPS)], axis=1)
                             for c in range(subs[t] // SG_CHUNK)], axis=0)

        scores = {}

        def score(h):
            scores[h] = lax.dot_general(q[:, heads[h]].astype(BF16), kb_ref[:, heads[h]],
                                        (((1,), (1,)), ((), ())), preferred_element_type=F32)

        gb = proj(z, GB, [functools.partial(score, h) for h in range(HEADS)])
        c_ = proj(z, C, take(n_chunks))
        h_ = proj(z, H, take(n_chunks))
        ubs = jax.nn.sigmoid(gb) * u * s
        a = c_ * h_
        conv = _causal_conv(caw_ref, hist[0], a)
        hist[0] = a[subs[t] - prev_rows:, :]

        outs = {}

        def value(h):
            sc = scores[h] * (HEAD_DIM ** -0.5)
            e = jnp.exp(sc - jnp.max(sc, axis=-1, keepdims=True))
            p = e * (1.0 / jnp.sum(e, axis=-1, keepdims=True))
            outs[h] = _dot(p.astype(BF16), vb_ref[:, heads[h]])

        b_ = proj(z, B_, [functools.partial(value, h) for h in range(HEADS)])
        ga = proj(z, GA, take(n_chunks))
        gm = proj(z, GM)
        y_m = jnp.concatenate([outs[h] for h in range(HEADS)], axis=1)
        acc = jax.nn.sigmoid(ga) * (b_ * conv) + ubs + jax.nn.sigmoid(gm) * y_m
        mix_ref[rs, :] = acc.astype(BF16)
    for step in sample_steps:
        step()

    a_scr[...] = hist[0]

    @pl.when(pl.program_id(1) == pl.num_programs(1) - 1)
    def _():
        state_ref[...] = hist[0][prev_rows - 2:, :]


def _mixer_prompt(l, x, norm_mix_g, w_in, conv_a_w, sg_ln_g, sg_w, sg_b_t, kb, vb,
                  sq, sgm, smixab, cache_k, cache_v, w_o, w_up, w_down):
    bsz, t, _ = x.shape
    tm = sum(SUBS_MIX)
    assert len(SUBS_MIX) >= 2
    n_t = t // tm
    prev_rows = SUBLANES
    hb = 2 * ATT_BB
    assert hb & (hb - 1) == 0 and sq.shape[1] == hb * bsz * n_t
    wspec = lambda shape: _resident((None,) + shape, lambda b, i: (l,) + (0,) * len(shape))
    small = pl.BlockSpec((2, hb, D), lambda b, i: (0, b * n_t + i, 0))
    big = pl.BlockSpec((None, ATT_BB, N_MEM, HEADS, HEAD_DIM),
                       lambda b, i: (l, b * n_t + i, 0, 0, 0))
    steps = bsz * n_t
    bf16_rows = 2 * SUBLANES

    def cast_blocks(w):
        rows = w.shape[1]
        per = steps
        while rows % per or (rows // per) % bf16_rows:
            per //= 2
        blk = (rows // per,) + w.shape[2:]
        zeros = (0,) * (w.ndim - 2)
        src = pl.BlockSpec((None,) + blk, lambda b, i: (l, (b * n_t + i) * per // steps) + zeros)
        dst = pl.BlockSpec(blk, lambda b, i: ((b * n_t + i) * per // steps,) + zeros)
        return src, dst, jax.ShapeDtypeStruct(w.shape[1:], BF16)

    casts = [cast_blocks(w) for w in (w_o, w_up, w_down)]
    return pl.pallas_call(
        functools.partial(_mixer_kernel, subs=SUBS_MIX, prev_rows=prev_rows),
        grid=(bsz, t // tm),
        in_specs=[
            pl.BlockSpec((None, tm, D), lambda b, i: (b, i, 0)),
            wspec((1, D)),
            wspec((D, D_IN)),
            wspec((3, D)),
            wspec((1, D)),
            wspec((SG_GROUPS, SG_CHUNK, SG_CHUNK)),
            wspec((SG_CHUNK, SG_GROUPS)),
            pl.BlockSpec((None, N_MEM, D), lambda b, i: (l * bsz + b, 0, 0)),
            pl.BlockSpec((None, N_MEM, D), lambda b, i: (l * bsz + b, 0, 0)),
            small, small, small, big, big,
        ] + [c[0] for c in casts],
        out_specs=[
            pl.BlockSpec((None, tm, D), lambda b, i: (b, i, 0)),
            pl.BlockSpec((None, 2, D), lambda b, i: (b, 0, 0)),
            small,
        ] + [c[1] for c in casts],
        out_shape=[
            jax.ShapeDtypeStruct((bsz, t, D), BF16),
            jax.ShapeDtypeStruct((bsz, 2, D), F32),
            jax.ShapeDtypeStruct(sq.shape, F32),
        ] + [c[2] for c in casts],
        scratch_shapes=[pltpu.VMEM((prev_rows, D), F32)],
        compiler_params=pltpu.CompilerParams(
            dimension_semantics=("arbitrary", "arbitrary"), vmem_limit_bytes=VMEM_LIMIT),
        name="mixer_prompt",
    )(x, norm_mix_g, w_in, conv_a_w, sg_ln_g, sg_w, sg_b_t, kb, vb, sq, sgm, smixab, cache_k, cache_v,
      w_o, w_up, w_down)


def _ffn_kernel(*refs, final, subs, prev_rows):
    if final:
        (x_ref, mix_ref, wo_ref, g_ref, wup_ref, cw_ref, cb_ref, wdn_ref, gfin_ref,
         out_ref, state_ref, h_scr) = refs
    else:
        (x_ref, mix_ref, wo_ref, g_ref, wup_ref, cw_ref, cb_ref, wdn_ref,
         out_ref, state_ref, h_scr) = refs

    @pl.when(pl.program_id(1) == 0)
    def _():
        h_scr[...] = jnp.zeros((prev_rows, D_FF2), F32)

    offs = [sum(subs[:t]) for t in range(len(subs))]
    slices = [slice(o, o + n) for o, n in zip(offs, subs)]
    x1 = [x_ref[rs, :] + _dot(mix_ref[rs, :], wo_ref[...]) for rs in slices]
    h = [_dot(_rms(x1[t], g_ref[...]).astype(BF16), wup_ref[...]) for t in range(len(subs))]
    hist = h_scr[...]
    for t, rs in enumerate(slices):
        hc = _causal_conv(cw_ref, hist, h[t]) + cb_ref[...]
        hist = h[t][subs[t] - prev_rows:, :]
        x2 = x1[t] + _dot(_silu_gate(hc), wdn_ref[...])
        out_ref[rs, :] = _rms(x2, gfin_ref[...]) if final else x2
    h_scr[...] = hist

    @pl.when(pl.program_id(1) == pl.num_programs(1) - 1)
    def _():
        state_ref[...] = hist[prev_rows - 2:, :]


def _ffn_prompt(l, final, x, mix, w_o, norm_ffn_g, w_up, conv_f_w, conv_f_b, w_down, norm_final_g):
    bsz, t, _ = x.shape
    tm, prev_rows = sum(SUBS_FFN), SUBLANES
    tok = pl.BlockSpec((None, tm, D), lambda b, i: (b, i, 0))
    wspec = lambda shape: _resident((None,) + shape, lambda b, i: (l,) + (0,) * len(shape))
    mat = lambda shape: _resident(shape, lambda b, i: (0, 0))
    x_shape = jax.ShapeDtypeStruct((bsz, t, D), F32)
    in_specs = [tok, tok, mat((D, D)), wspec((1, D)), mat((D, D_FF2)), wspec((3, D_FF2)),
                wspec((1, D_FF2)), mat((D_FF, D))]
    args = [x, mix, w_o, norm_ffn_g, w_up, conv_f_w, conv_f_b, w_down]
    out_specs = [tok, pl.BlockSpec((None, 2, D_FF2), lambda b, i: (b, 0, 0))]
    out_shape = [x_shape, jax.ShapeDtypeStruct((bsz, 2, D_FF2), F32)]
    if final:
        in_specs.append(_resident((1, D), lambda b, i: (0, 0)))
        args.append(norm_final_g)
    return pl.pallas_call(
        functools.partial(_ffn_kernel, final=final, subs=SUBS_FFN, prev_rows=prev_rows),
        grid=(bsz, t // tm),
        in_specs=in_specs,
        out_specs=out_specs,
        out_shape=out_shape,
        scratch_shapes=[pltpu.VMEM((prev_rows, D_FF2), F32)],
        compiler_params=pltpu.CompilerParams(
            dimension_semantics=("arbitrary", "arbitrary"), vmem_limit_bytes=VMEM_LIMIT),
        name="ffn_prompt",
    )(*args)


SAMPLE_STEPS = 4


def _put(scr, x):
    scr[SUBLANES:SUBLANES + x.shape[0], :] = x


def _get(scr, rows, k):
    return scr[SUBLANES - k:SUBLANES - k + rows, :]


def _zero_margins(scr, rows):
    scr[0:SUBLANES, :] = jnp.zeros((SUBLANES, scr.shape[1]), F32)
    scr[SUBLANES + rows:, :] = jnp.zeros((SUBLANES, scr.shape[1]), F32)


def _sample_conv(scr, w_ref, cur0, cur1, hist):
    rows = cur0.shape[0]
    first = (lax.broadcasted_iota(jnp.int32, (rows, 1), 0) & 1) == 0
    _put(scr, hist)
    hist_up = _get(scr, rows, -1)
    _put(scr, cur0)
    cur0_dn = _get(scr, rows, 1)
    cur0_up = _get(scr, rows, -1)
    _put(scr, cur1)
    cur1_dn = _get(scr, rows, 1)
    w0, w1, w2 = w_ref[0:1, :], w_ref[1:2, :], w_ref[2:3, :]
    c0 = w0 * hist + w1 * jnp.where(first, hist_up, cur0_dn) + w2 * cur0
    c1 = w0 * cur0 + w1 * jnp.where(first, cur0_up, cur1_dn) + w2 * cur1
    return c0, c1


def _mixer_sample_kernel(x_ref, nm_ref, win_ref, caw_ref, lng_ref, sgc_ref, sgb_ref, hist_ref,
                         mixab_ref, gm_ref, q_ref, vn_ref, state_ref, scr):
    rows = x_ref.shape[1]
    halves = lambda a: (a[:rows], a[rows:])
    _zero_margins(scr, rows)
    z = _rms(x_ref[...].reshape(2 * rows, D), nm_ref[...]).astype(BF16)
    p = {}
    for group in PROJ_GROUPS:
        for i in group:
            p[i] = _dot(z, win_ref[:, i * D:(i + 1) * D])

    vn0, vn1 = halves(_layernorm(p[V], lng_ref[...]))
    vn_ref[0] = vn0
    vn_ref[1] = vn1
    tiles = lambda a: a.reshape(rows // SUBLANES, SUBLANES, D)
    coef = lambda k: sgc_ref[k][None]
    _put(scr, vn0)
    vn0_dn, vn0_up = tiles(_get(scr, rows, 1)), tiles(_get(scr, rows, -1))
    _put(scr, vn1)
    vn1_dn = tiles(_get(scr, rows, 1))
    s0 = coef(0) * tiles(vn0) + coef(1) * vn0_dn + sgb_ref[0][None]
    s1 = (coef(2) * tiles(vn0) + coef(3) * vn0_up + coef(4) * vn0_dn
          + coef(5) * tiles(vn1) + coef(6) * vn1_dn + sgb_ref[1][None])
    s = jnp.concatenate([s0.reshape(rows, D), s1.reshape(rows, D)], axis=0)

    a0, a1 = halves(p[C] * p[H])
    hist = hist_ref[...].reshape(rows, D)
    conv = jnp.concatenate(_sample_conv(scr, caw_ref, a0, a1, hist), axis=0)
    state_ref[...] = a1.reshape(state_ref.shape)
    acc = jax.nn.sigmoid(p[GA]) * (p[B_] * conv)
    acc = acc + jax.nn.sigmoid(p[GB]) * (p[U] * s)
    mixab_ref[...] = acc.reshape(2, rows, D)
    gm_ref[...] = jax.nn.sigmoid(p[GM]).reshape(2, rows, D)
    q_ref[...] = p[Q].reshape(2, rows, D)


def _mixer_sample(l, xs, norm_mix_g, w_in, conv_a_w, sg_ln_g, sgc, sgb, cache_a):
    rows = xs.shape[1]
    tm = rows // 2
    wspec = lambda shape: _resident((None,) + shape, lambda i: (l,) + (0,) * len(shape))
    tok = pl.BlockSpec((2, tm, D), lambda i: (0, i, 0))
    hist = pl.BlockSpec((None, tm // 2, 2, D), lambda i: (l, i, 0, 0))
    out_f = jax.ShapeDtypeStruct((2, rows, D), F32)
    return pl.pallas_call(
        _mixer_sample_kernel,
        grid=(rows // tm,),
        in_specs=[tok, wspec((1, D)), wspec((D, D_IN)), wspec((3, D)), wspec((1, D)),
                  wspec((7, SUBLANES, D)), wspec((2, SUBLANES, D)), hist],
        out_specs=[tok] * 4 + [pl.BlockSpec((tm // 2, 2, D), lambda i: (i, 0, 0))],
        out_shape=[out_f] * 4 + [jax.ShapeDtypeStruct((rows // 2, 2, D), F32)],
        scratch_shapes=[pltpu.VMEM((tm + 2 * SUBLANES, D), F32)],
        compiler_params=pltpu.CompilerParams(
            dimension_semantics=("arbitrary",), vmem_limit_bytes=VMEM_LIMIT),
        name="mixer_sample",
    )(xs, norm_mix_g, w_in, conv_a_w, sg_ln_g, sgc, sgb, cache_a)


def _ffn_sample_kernel(*refs, final):
    if final:
        (x_ref, mix_ref, wo_ref, g_ref, wup_ref, cw_ref, cb_ref, wdn_ref, hist_ref, gfin_ref,
         xo_ref, state_ref, y_ref, scr) = refs
    else:
        (x_ref, mix_ref, wo_ref, g_ref, wup_ref, cw_ref, cb_ref, wdn_ref, hist_ref,
         xo_ref, state_ref, scr) = refs
    rows = x_ref.shape[1]
    _zero_margins(scr, rows)
    x1 = x_ref[...].reshape(2 * rows, D) + _dot(
        mix_ref[...].reshape(2 * rows, D).astype(BF16), wo_ref[...])
    h = _dot(_rms(x1, g_ref[...]).astype(BF16), wup_ref[...])
    h0, h1 = h[:rows], h[rows:]
    hist = hist_ref[...].reshape(rows, D_FF2)
    conv = jnp.concatenate(_sample_conv(scr, cw_ref, h0, h1, hist), axis=0)
    state_ref[...] = h1.reshape(state_ref.shape)
    x2 = x1 + _dot(_silu_gate(conv + cb_ref[...]), wdn_ref[...])
    xo_ref[...] = x2.reshape(2, rows, D)
    if final:
        y_ref[...] = _rms(x2, gfin_ref[...]).reshape(2, rows, D)


def _ffn_sample(l, final, xs, mix, w_o, norm_ffn_g, w_up, conv_f_w, conv_f_b, w_down, cache_f,
                norm_final_g):
    rows = xs.shape[1]
    tm = rows // 2
    tok = pl.BlockSpec((2, tm, D), lambda i: (0, i, 0))
    wspec = lambda shape: _resident((None,) + shape, lambda i: (l,) + (0,) * len(shape))
    mat = lambda shape: _resident(shape, lambda i: (0, 0))
    x_shape = jax.ShapeDtypeStruct((2, rows, D), F32)
    in_specs = [tok, tok, mat((D, D)), wspec((1, D)), mat((D, D_FF2)), wspec((3, D_FF2)),
                wspec((1, D_FF2)), mat((D_FF, D)),
                pl.BlockSpec((None, tm // 2, 2, D_FF2), lambda i: (l, i, 0, 0))]
    args = [xs, mix, w_o, norm_ffn_g, w_up, conv_f_w, conv_f_b, w_down, cache_f]
    out_specs = [tok, pl.BlockSpec((tm // 2, 2, D_FF2), lambda i: (i, 0, 0))]
    out_shape = [x_shape, jax.ShapeDtypeStruct((rows // 2, 2, D_FF2), F32)]
    if final:
        in_specs.append(_resident((1, D), lambda i: (0, 0)))
        args.append(norm_final_g)
        out_specs.append(tok)
        out_shape.append(x_shape)
    return pl.pallas_call(
        functools.partial(_ffn_sample_kernel, final=final),
        grid=(rows // tm,),
        in_specs=in_specs,
        out_specs=out_specs,
        out_shape=out_shape,
        scratch_shapes=[pltpu.VMEM((tm + 2 * SUBLANES, D_FF2), F32)],
        compiler_params=pltpu.CompilerParams(
            dimension_semantics=("arbitrary",), vmem_limit_bytes=VMEM_LIMIT),
        name="ffn_sample",
    )(*args)


def _sample_gating_tables(sg_w, sg_b):
    w = sg_w[:, :, :SAMPLE_STEPS, :SAMPLE_STEPS]
    zero = jnp.zeros_like(w[..., 0, 0])
    by_parity = [(w[..., 0, 0], w[..., 1, 1]), (zero, w[..., 1, 0]), (w[..., 2, 0], w[..., 3, 1]),
                 (w[..., 2, 1], zero), (zero, w[..., 3, 0]), (w[..., 2, 2], w[..., 3, 3]),
                 (zero, w[..., 3, 2])]

    def table(pairs):
        t = jnp.stack([jnp.stack(p, axis=1) for p in pairs], axis=1)
        t = jnp.tile(t, (1, 1, SUBLANES // 2, 1))
        return jnp.repeat(t, SG_COLS, axis=3)

    b = sg_b[:, :, :SAMPLE_STEPS]
    return table(by_parity), table([(b[..., 0], b[..., 1]), (b[..., 2], b[..., 3])])


def kernel(x_prompt, x_sample, mem_prompt, cache_conv_a, cache_conv_ffn, cache_mem_k, cache_mem_v, norm_mix_g, w_in, conv_a_w, sg_ln_g, sg_w, sg_b, norm_mem_g, w_k, w_v, w_o, norm_ffn_g, w_up, conv_f_w, conv_f_b, w_down, norm_final_g):
    depth = w_in.shape[0]
    bp = x_prompt.shape[0]
    bs, ts, _ = x_sample.shape
    assert ts == SAMPLE_STEPS

    wkv_b = jnp.concatenate([w_k, w_v], axis=-1).astype(BF16)

    nmg = norm_mix_g[:, None, :]
    lng = sg_ln_g[:, None, :]
    nfg = norm_ffn_g[:, None, :]
    nmemg = norm_mem_g[:, None, :]
    cfb = conv_f_b[:, None, :]
    gfin = norm_final_g[None, :]
    sg_b_t = jnp.swapaxes(sg_b, 1, 2)
    sgc, sgbias = _sample_gating_tables(sg_w, sg_b)
    to_halves = lambda a: jnp.swapaxes(a.reshape(bs, 2, 2, D), 0, 1).reshape(2, bs * 2, D)
    from_halves = lambda a: jnp.swapaxes(a.reshape(2, bs, 2, D), 0, 1).reshape(bs, ts, D)
    xs = to_halves(x_sample)

    k_f, v_f, k_b, v_b, w_in_b = _mem_kv(mem_prompt.reshape(bp * N_MEM, D), nmemg, wkv_b, w_in)
    k_b = k_b.reshape(depth * bp, N_MEM, D)
    v_b = v_b.reshape(depth * bp, N_MEM, D)

    xp = x_prompt
    pa, pf, sa, sf, sv = [], [], [], [], []
    ys = None
    for l in range(depth):
        final = l == depth - 1
        mixab, gm, q, vn, a_state_s = _mixer_sample(l, xs, nmg, w_in_b, conv_a_w, lng, sgc,
                                                    sgbias, cache_conv_a)
        mix, a_state, mix_s, w_o_b, w_up_b, w_down_b = _mixer_prompt(
            l, xp, nmg, w_in_b, conv_a_w, lng, sg_w, sg_b_t, k_b, v_b, q, gm, mixab,
            cache_mem_k, cache_mem_v, w_o, w_up, w_down)
        xp, f_state = _ffn_prompt(l, final, xp, mix, w_o_b, nfg, w_up_b, conv_f_w, cfb, w_down_b,
                                  gfin)
        pa.append(a_state)
        pf.append(f_state)
        outs = _ffn_sample(l, final, xs, mix_s, w_o_b, nfg, w_up_b, conv_f_w, cfb, w_down_b,
                           cache_conv_ffn, gfin)
        xs, f_state_s = outs[0], outs[1]
        if final:
            ys = outs[2]
        sa.append(a_state_s)
        sf.append(f_state_s)
        sv.append(from_halves(vn))

    kv_shape = (depth, bp, N_MEM, HEADS, HEAD_DIM)
    return (xp, from_halves(ys),
            jnp.stack(pa), jnp.stack(pf), k_f.reshape(kv_shape), v_f.reshape(kv_shape),
            jnp.stack(sa), jnp.stack(sf), jnp.stack(sv))
```

```python
import functools

import jax
import jax.numpy as jnp
from jax import lax
from jax.experimental import pallas as pl
from jax.experimental.pallas import tpu as pltpu

F32 = jnp.float32
BF16 = jnp.bfloat16

D = 1024
D_IN = 9 * D
D_FF = 2816
D_FF2 = 2 * D_FF
N_MEM = 256
HEADS = 4
HEAD_DIM = D // HEADS
SG_GROUPS = 4
SG_COLS = D // SG_GROUPS
SG_CHUNK = 128
EPS = 1e-6
H, C, B_, U, V, Q, GA, GB, GM = range(9)
PROJ_GROUPS = [(V, Q, U), (GB, C, H), (B_, GA, GM)]

V7X_VMEM_BYTES = 64 * 1024 * 1024
VMEM_LIMIT = V7X_VMEM_BYTES - 4 * 1024 * 1024
SUBLANES = 8

SUBS_MIX = (256, 256)
SUBS_FFN = (256, 256, 256, 256)
TM_KV = 512
ATT_BB = 4
MASKED = -1e30


def _rms(x, g):
    y = x * lax.rsqrt(jnp.mean(x * x, axis=-1, keepdims=True) + EPS)
    return y * g


def _layernorm(x, g):
    xc = x - jnp.mean(x, axis=-1, keepdims=True)
    return xc * lax.rsqrt(jnp.mean(xc * xc, axis=-1, keepdims=True) + EPS) * g


def _dot(a, b):
    return jnp.dot(a, b, preferred_element_type=F32)


def _resident(shape, index_map):
    return pl.BlockSpec(shape, index_map, pipeline_mode=pl.Buffered(1))


def _causal_conv(w_ref, hist, cur):
    n = hist.shape[0]
    ext = jnp.concatenate([hist, cur], axis=0)
    x1 = pltpu.roll(ext, 1, axis=0)[n:, :]
    x2 = pltpu.roll(ext, 2, axis=0)[n:, :]
    return w_ref[0:1, :] * x2 + w_ref[1:2, :] * x1 + w_ref[2:3, :] * cur


def _silu_gate(hc):
    return (jax.nn.silu(hc[:, D_FF:]) * hc[:, :D_FF]).astype(BF16)


def _kv_kernel(mem_ref, g_ref, wkv_ref, win_ref, k_ref, v_ref, kb_ref, vb_ref, winb_ref):
    winb_ref[...] = win_ref[...].astype(BF16)
    m = _rms(mem_ref[...], g_ref[...]).astype(BF16)
    k = _dot(m, wkv_ref[:, :D])
    v = _dot(m, wkv_ref[:, D:])
    for h in range(HEADS):
        hc = slice(h * HEAD_DIM, (h + 1) * HEAD_DIM)
        k_ref[:, h, :] = k[:, hc]
        v_ref[:, h, :] = v[:, hc]
    kb_ref[...] = k.astype(BF16)
    vb_ref[...] = v.astype(BF16)


def _mem_kv(mem2d, norm_mem_g, wkv, w_in):
    depth = wkv.shape[0]
    rows = mem2d.shape[0]
    steps = rows // TM_KV
    win_blk = pl.BlockSpec((None, D // steps, D_IN), lambda l, i: (l, i, 0))
    out_f = jax.ShapeDtypeStruct((depth, rows, HEADS, HEAD_DIM), F32)
    out_b = jax.ShapeDtypeStruct((depth, rows, D), BF16)
    oblk = pl.BlockSpec((None, TM_KV, D), lambda l, i: (l, i, 0))
    oblk_f = pl.BlockSpec((None, TM_KV, HEADS, HEAD_DIM), lambda l, i: (l, i, 0, 0))
    return pl.pallas_call(
        _kv_kernel,
        grid=(depth, rows // TM_KV),
        in_specs=[
            pl.BlockSpec((TM_KV, D), lambda l, i: (i, 0)),
            pl.BlockSpec((None, 1, D), lambda l, i: (l, 0, 0)),
            pl.BlockSpec((None, D, 2 * D), lambda l, i: (l, 0, 0)),
            win_blk,
        ],
        out_specs=[oblk_f, oblk_f, oblk, oblk, win_blk],
        out_shape=[out_f, out_f, out_b, out_b, jax.ShapeDtypeStruct(w_in.shape, BF16)],
        compiler_params=pltpu.CompilerParams(
            dimension_semantics=("arbitrary", "arbitrary"), vmem_limit_bytes=VMEM_LIMIT),
        name="mem_kv",
    )(mem2d, norm_mem_g, wkv, w_in)


def _attn_steps(qgm_ref, k_ref, v_ref, mix_ref):
    hb = qgm_ref.shape[2]
    nb = 2 * hb
    nq = HEADS * nb
    nk = N_MEM * HEADS
    tile = HEAD_DIM
    row = lax.broadcasted_iota(jnp.int32, (nq, nk), 0)
    col = lax.broadcasted_iota(jnp.int32, (nq, nk), 1)
    same_head = (row >> (nb.bit_length() - 1)) == (col & (HEADS - 1))
    row_b = (row & (hb - 1)) >> 1
    out_b = (lax.broadcasted_iota(jnp.int32, (nq, 1), 0) & (hb - 1)) >> 1
    q = qgm_ref[0].reshape(nb, D)
    q_all = jnp.concatenate([q[:, h * HEAD_DIM:(h + 1) * HEAD_DIM] for h in range(HEADS)],
                            axis=0).astype(BF16)
    st = {"y": jnp.zeros((nq, HEAD_DIM), F32)}

    def score_tile(b, j):
        if j == 0:
            st["k"] = k_ref[b].reshape(nk, HEAD_DIM).astype(BF16)
            st["sc"] = []
        st["sc"].append(lax.dot_general(q_all, st["k"][j * tile:(j + 1) * tile],
                                        (((1,), (1,)), ((), ())), preferred_element_type=F32))
        if j == nk // tile - 1:
            sc = jnp.concatenate(st["sc"], axis=1) * (HEAD_DIM ** -0.5)
            sc = jnp.where(same_head & (row_b == b), sc, MASKED)
            e = jnp.exp(sc - jnp.max(sc, axis=-1, keepdims=True))
            st["p", b] = (e * (1.0 / jnp.sum(e, axis=-1, keepdims=True))).astype(BF16)

    def value_tile(b, j):
        if j == 0:
            st["v"] = v_ref[b].reshape(nk, HEAD_DIM).astype(BF16)
            st["o"] = jnp.zeros((nq, HEAD_DIM), F32)
        st["o"] = st["o"] + _dot(st["p", b][:, j * tile:(j + 1) * tile],
                                 st["v"][j * tile:(j + 1) * tile])
        if j == nk // tile - 1:
            st["y"] = jnp.where(out_b == b, st["o"], st["y"])
            if b == ATT_BB - 1:
                y = jnp.concatenate([st["y"][h * nb:(h + 1) * nb] for h in range(HEADS)], axis=1)
                mix_ref[...] = qgm_ref[2] + qgm_ref[1] * y.reshape(2, hb, D)

    tiles = [(b, j) for b in range(ATT_BB) for j in range(nk // tile)]
    return ([functools.partial(score_tile, b, j) for b, j in tiles]
            + [functools.partial(value_tile, b, j) for b, j in tiles])


def _mixer_kernel(x_ref, nm_ref, win_ref, caw_ref, lng_ref, sgw_ref, sgb_ref, kb_ref, vb_ref,
                  sqgm_ref, sk_ref, sv_ref, wo_ref, wup_ref, wdn_ref,
                  mix_ref, state_ref, smix_ref, wob_ref, wupb_ref, wdnb_ref, a_scr,
                  *, subs, prev_rows):
    wob_ref[...] = wo_ref[...].astype(BF16)
    wupb_ref[...] = wup_ref[...].astype(BF16)
    wdnb_ref[...] = wdn_ref[...].astype(BF16)

    @pl.when(pl.program_id(1) == 0)
    def _():
        a_scr[...] = jnp.zeros((prev_rows, D), F32)

    ri = lax.broadcasted_iota(jnp.int32, (SG_CHUNK, SG_CHUNK), 0)
    ci = lax.broadcasted_iota(jnp.int32, (SG_CHUNK, SG_CHUNK), 1)
    sgw = [jnp.where(ci <= ri, sgw_ref[g], 0.0).astype(BF16) for g in range(SG_GROUPS)]
    offs = [sum(subs[:t]) for t in range(len(subs))]
    hist = [a_scr[...]]

    sample_steps = _attn_steps(sqgm_ref, sk_ref, sv_ref, smix_ref)
    chunk = 2 * SG_CHUNK
    n_chunks = D // chunk

    def take(n):
        steps = sample_steps[:n]
        del sample_steps[:n]
        return steps

    def proj(z, i, riders=()):
        riders = list(riders)
        cols = []
        for j in range(i * D, (i + 1) * D, chunk):
            cols.append(_dot(z, win_ref[:, j:j + chunk]))
            if riders:
                riders.pop(0)()
        assert not riders
        return jnp.concatenate(cols, axis=1)

    for step in take(3):
        step()
    for t in range(len(subs)):
        rs = slice(offs[t], offs[t] + subs[t])
        z = _rms(x_ref[rs, :], nm_ref[...]).astype(BF16)
        heads = [slice(h * HEAD_DIM, (h + 1) * HEAD_DIM) for h in range(HEADS)]

        v = proj(z, V, take(n_chunks))
        q = proj(z, Q, take(n_chunks))
        vnb = _layernorm(v, lng_ref[...]).astype(BF16)
        s_part = {}

        def spatial(jobs):
            for c, g in jobs:
                vg = vnb[c * SG_CHUNK:(c + 1) * SG_CHUNK, g * SG_COLS:(g + 1) * SG_COLS]
                s_part[c, g] = _dot(sgw[g], vg) + sgb_ref[:, g:g + 1]

        jobs = [(c, g) for c in range(subs[t] // SG_CHUNK) for g in range(SG_GROUPS)]
        per = -(-len(jobs) // n_chunks)
        u = proj(z, U, [functools.partial(spatial, jobs[k * per:(k + 1) * per])
                        for k in range(n_chunks)])
        s = jnp.concatenate([jnp.concatenate([s_part[c, g] for g in range(SG_GROUPS)], axis=1)
                             for c in range(subs[t] // SG_CHUNK)], axis=0)

        scores = {}

        def score(h):
            scores[h] = lax.dot_general(q[:, heads[h]].astype(BF16), kb_ref[:, heads[h]],
                                        (((1,), (1,)), ((), ())), preferred_element_type=F32)

        gb = proj(z, GB, [functools.partial(score, h) for h in range(HEADS)])
        c_ = proj(z, C, take(n_chunks))
        h_ = proj(z, H, take(n_chunks))
        ubs = jax.nn.sigmoid(gb) * u * s
        a = c_ * h_
        conv = _causal_conv(caw_ref, hist[0], a)
        hist[0] = a[subs[t] - prev_rows:, :]

        outs = {}

        def value(h):
            sc = scores[h] * (HEAD_DIM ** -0.5)
            e = jnp.exp(sc - jnp.max(sc, axis=-1, keepdims=True))
            p = e * (1.0 / jnp.sum(e, axis=-1, keepdims=True))
            outs[h] = _dot(p.astype(BF16), vb_ref[:, heads[h]])

        b_ = proj(z, B_, [functools.partial(value, h) for h in range(HEADS)])
        ga = proj(z, GA, take(n_chunks))
        gm = proj(z, GM)
        y_m = jnp.concatenate([outs[h] for h in range(HEADS)], axis=1)
        acc = jax.nn.sigmoid(ga) * (b_ * conv) + ubs + jax.nn.sigmoid(gm) * y_m
        mix_ref[rs, :] = acc.astype(BF16)
    for step in sample_steps:
        step()

    a_scr[...] = hist[0]

    @pl.when(pl.program_id(1) == pl.num_programs(1) - 1)
    def _():
        state_ref[...] = hist[0][prev_rows - 2:, :]


def _mixer_prompt(l, x, norm_mix_g, w_in, conv_a_w, sg_ln_g, sg_w, sg_b_t, kb, vb,
                  sqgm, cache_k, cache_v, w_o, w_up, w_down):
    bsz, t, _ = x.shape
    tm = sum(SUBS_MIX)
    assert len(SUBS_MIX) >= 2
    n_t = t // tm
    prev_rows = SUBLANES
    hb = 2 * ATT_BB
    assert hb & (hb - 1) == 0 and sqgm.shape[2] == hb * bsz * n_t
    wspec = lambda shape: _resident((None,) + shape, lambda b, i: (l,) + (0,) * len(shape))
    small = pl.BlockSpec((2, hb, D), lambda b, i: (0, b * n_t + i, 0))
    small3 = pl.BlockSpec((3, 2, hb, D), lambda b, i: (0, 0, b * n_t + i, 0))
    big = pl.BlockSpec((None, ATT_BB, N_MEM, HEADS, HEAD_DIM),
                       lambda b, i: (l, b * n_t + i, 0, 0, 0))
    steps = bsz * n_t
    bf16_rows = 2 * SUBLANES

    def cast_blocks(w):
        rows = w.shape[1]
        per = steps
        while rows % per or (rows // per) % bf16_rows:
            per //= 2
        blk = (rows // per,) + w.shape[2:]
        zeros = (0,) * (w.ndim - 2)
        src = pl.BlockSpec((None,) + blk, lambda b, i: (l, (b * n_t + i) * per // steps) + zeros)
        dst = pl.BlockSpec(blk, lambda b, i: ((b * n_t + i) * per // steps,) + zeros)
        return src, dst, jax.ShapeDtypeStruct(w.shape[1:], BF16)

    casts = [cast_blocks(w) for w in (w_o, w_up, w_down)]
    return pl.pallas_call(
        functools.partial(_mixer_kernel, subs=SUBS_MIX, prev_rows=prev_rows),
        grid=(bsz, t // tm),
        in_specs=[
            pl.BlockSpec((None, tm, D), lambda b, i: (b, i, 0)),
            wspec((1, D)),
            wspec((D, D_IN)),
            wspec((3, D)),
            wspec((1, D)),
            wspec((SG_GROUPS, SG_CHUNK, SG_CHUNK)),
            wspec((SG_CHUNK, SG_GROUPS)),
            pl.BlockSpec((None, N_MEM, D), lambda b, i: (l * bsz + b, 0, 0)),
            pl.BlockSpec((None, N_MEM, D), lambda b, i: (l * bsz + b, 0, 0)),
            small3, big, big,
        ] + [c[0] for c in casts],
        out_specs=[
            pl.BlockSpec((None, tm, D), lambda b, i: (b, i, 0)),
            pl.BlockSpec((None, 2, D), lambda b, i: (b, 0, 0)),
            small,
        ] + [c[1] for c in casts],
        out_shape=[
            jax.ShapeDtypeStruct((bsz, t, D), BF16),
            jax.ShapeDtypeStruct((bsz, 2, D), F32),
            jax.ShapeDtypeStruct(sqgm.shape[1:], F32),
        ] + [c[2] for c in casts],
        scratch_shapes=[pltpu.VMEM((prev_rows, D), F32)],
        compiler_params=pltpu.CompilerParams(
            dimension_semantics=("arbitrary", "arbitrary"), vmem_limit_bytes=VMEM_LIMIT),
        name="mixer_prompt",
    )(x, norm_mix_g, w_in, conv_a_w, sg_ln_g, sg_w, sg_b_t, kb, vb, sqgm, cache_k, cache_v,
      w_o, w_up, w_down)


def _ffn_kernel(*refs, final, subs, prev_rows):
    if final:
        (x_ref, mix_ref, wo_ref, g_ref, wup_ref, cw_ref, cb_ref, wdn_ref, gfin_ref,
         out_ref, state_ref, h_scr) = refs
    else:
        (x_ref, mix_ref, wo_ref, g_ref, wup_ref, cw_ref, cb_ref, wdn_ref,
         out_ref, state_ref, h_scr) = refs

    @pl.when(pl.program_id(1) == 0)
    def _():
        h_scr[...] = jnp.zeros((prev_rows, D_FF2), F32)

    offs = [sum(subs[:t]) for t in range(len(subs))]
    slices = [slice(o, o + n) for o, n in zip(offs, subs)]
    x1 = [x_ref[rs, :] + _dot(mix_ref[rs, :], wo_ref[...]) for rs in slices]
    h = [_dot(_rms(x1[t], g_ref[...]).astype(BF16), wup_ref[...]) for t in range(len(subs))]
    hist = h_scr[...]
    for t, rs in enumerate(slices):
        hc = _causal_conv(cw_ref, hist, h[t]) + cb_ref[...]
        hist = h[t][subs[t] - prev_rows:, :]
        x2 = x1[t] + _dot(_silu_gate(hc), wdn_ref[...])
        out_ref[rs, :] = _rms(x2, gfin_ref[...]) if final else x2
    h_scr[...] = hist

    @pl.when(pl.program_id(1) == pl.num_programs(1) - 1)
    def _():
        state_ref[...] = hist[prev_rows - 2:, :]


def _ffn_prompt(l, final, x, mix, w_o, norm_ffn_g, w_up, conv_f_w, conv_f_b, w_down, norm_final_g):
    bsz, t, _ = x.shape
    tm, prev_rows = sum(SUBS_FFN), SUBLANES
    tok = pl.BlockSpec((None, tm, D), lambda b, i: (b, i, 0))
    wspec = lambda shape: _resident((None,) + shape, lambda b, i: (l,) + (0,) * len(shape))
    mat = lambda shape: _resident(shape, lambda b, i: (0, 0))
    x_shape = jax.ShapeDtypeStruct((bsz, t, D), F32)
    in_specs = [tok, tok, mat((D, D)), wspec((1, D)), mat((D, D_FF2)), wspec((3, D_FF2)),
                wspec((1, D_FF2)), mat((D_FF, D))]
    args = [x, mix, w_o, norm_ffn_g, w_up, conv_f_w, conv_f_b, w_down]
    out_specs = [tok, pl.BlockSpec((None, 2, D_FF2), lambda b, i: (b, 0, 0))]
    out_shape = [x_shape, jax.ShapeDtypeStruct((bsz, 2, D_FF2), F32)]
    if final:
        in_specs.append(_resident((1, D), lambda b, i: (0, 0)))
        args.append(norm_final_g)
    return pl.pallas_call(
        functools.partial(_ffn_kernel, final=final, subs=SUBS_FFN, prev_rows=prev_rows),
        grid=(bsz, t // tm),
        in_specs=in_specs,
        out_specs=out_specs,
        out_shape=out_shape,
        scratch_shapes=[pltpu.VMEM((prev_rows, D_FF2), F32)],
        compiler_params=pltpu.CompilerParams(
            dimension_semantics=("arbitrary", "arbitrary"), vmem_limit_bytes=VMEM_LIMIT),
        name="ffn_prompt",
    )(*args)


SAMPLE_STEPS = 4


def _put(scr, x):
    scr[SUBLANES:SUBLANES + x.shape[0], :] = x


def _get(scr, rows, k):
    return scr[SUBLANES - k:SUBLANES - k + rows, :]


def _zero_margins(scr, rows):
    scr[0:SUBLANES, :] = jnp.zeros((SUBLANES, scr.shape[1]), F32)
    scr[SUBLANES + rows:, :] = jnp.zeros((SUBLANES, scr.shape[1]), F32)


def _sample_conv(scr, w_ref, cur0, cur1, hist):
    rows = cur0.shape[0]
    first = (lax.broadcasted_iota(jnp.int32, (rows, 1), 0) & 1) == 0
    _put(scr, hist)
    hist_up = _get(scr, rows, -1)
    _put(scr, cur0)
    cur0_dn = _get(scr, rows, 1)
    cur0_up = _get(scr, rows, -1)
    _put(scr, cur1)
    cur1_dn = _get(scr, rows, 1)
    w0, w1, w2 = w_ref[0:1, :], w_ref[1:2, :], w_ref[2:3, :]
    c0 = w0 * hist + w1 * jnp.where(first, hist_up, cur0_dn) + w2 * cur0
    c1 = w0 * cur0 + w1 * jnp.where(first, cur0_up, cur1_dn) + w2 * cur1
    return c0, c1


def _mixer_sample_kernel(x_ref, nm_ref, win_ref, caw_ref, lng_ref, sgc_ref, sgb_ref, hist_ref,
                         qgm_ref, vn_ref, state_ref, scr):
    rows = x_ref.shape[1]
    halves = lambda a: (a[:rows], a[rows:])
    _zero_margins(scr, rows)
    z = _rms(x_ref[...].reshape(2 * rows, D), nm_ref[...]).astype(BF16)
    p = {}
    for group in PROJ_GROUPS:
        for i in group:
            p[i] = _dot(z, win_ref[:, i * D:(i + 1) * D])

    vn0, vn1 = halves(_layernorm(p[V], lng_ref[...]))
    vn_ref[0] = vn0
    vn_ref[1] = vn1
    tiles = lambda a: a.reshape(rows // SUBLANES, SUBLANES, D)
    coef = lambda k: sgc_ref[k][None]
    _put(scr, vn0)
    vn0_dn, vn0_up = tiles(_get(scr, rows, 1)), tiles(_get(scr, rows, -1))
    _put(scr, vn1)
    vn1_dn = tiles(_get(scr, rows, 1))
    s0 = coef(0) * tiles(vn0) + coef(1) * vn0_dn + sgb_ref[0][None]
    s1 = (coef(2) * tiles(vn0) + coef(3) * vn0_up + coef(4) * vn0_dn
          + coef(5) * tiles(vn1) + coef(6) * vn1_dn + sgb_ref[1][None])
    s = jnp.concatenate([s0.reshape(rows, D), s1.reshape(rows, D)], axis=0)

    a0, a1 = halves(p[C] * p[H])
    hist = hist_ref[...].reshape(rows, D)
    conv = jnp.concatenate(_sample_conv(scr, caw_ref, a0, a1, hist), axis=0)
    state_ref[...] = a1.reshape(state_ref.shape)
    acc = jax.nn.sigmoid(p[GA]) * (p[B_] * conv)
    acc = acc + jax.nn.sigmoid(p[GB]) * (p[U] * s)
    qgm_ref[0] = p[Q].reshape(2, rows, D)
    qgm_ref[1] = jax.nn.sigmoid(p[GM]).reshape(2, rows, D)
    qgm_ref[2] = acc.reshape(2, rows, D)


def _mixer_sample(l, xs, norm_mix_g, w_in, conv_a_w, sg_ln_g, sgc, sgb, cache_a):
    rows = xs.shape[1]
    tm = rows // 2
    wspec = lambda shape: _resident((None,) + shape, lambda i: (l,) + (0,) * len(shape))
    tok = pl.BlockSpec((2, tm, D), lambda i: (0, i, 0))
    hist = pl.BlockSpec((None, tm // 2, 2, D), lambda i: (l, i, 0, 0))
    out_f = jax.ShapeDtypeStruct((2, rows, D), F32)
    return pl.pallas_call(
        _mixer_sample_kernel,
        grid=(rows // tm,),
        in_specs=[tok, wspec((1, D)), wspec((D, D_IN)), wspec((3, D)), wspec((1, D)),
                  wspec((7, SUBLANES, D)), wspec((2, SUBLANES, D)), hist],
        out_specs=[pl.BlockSpec((3, 2, tm, D), lambda i: (0, 0, i, 0)), tok,
                   pl.BlockSpec((tm // 2, 2, D), lambda i: (i, 0, 0))],
        out_shape=[jax.ShapeDtypeStruct((3, 2, rows, D), F32), out_f,
                   jax.ShapeDtypeStruct((rows // 2, 2, D), F32)],
        scratch_shapes=[pltpu.VMEM((tm + 2 * SUBLANES, D), F32)],
        compiler_params=pltpu.CompilerParams(
            dimension_semantics=("arbitrary",), vmem_limit_bytes=VMEM_LIMIT),
        name="mixer_sample",
    )(xs, norm_mix_g, w_in, conv_a_w, sg_ln_g, sgc, sgb, cache_a)


def _ffn_sample_kernel(*refs, final):
    if final:
        (x_ref, mix_ref, wo_ref, g_ref, wup_ref, cw_ref, cb_ref, wdn_ref, hist_ref, gfin_ref,
         xo_ref, state_ref, y_ref, scr) = refs
    else:
        (x_ref, mix_ref, wo_ref, g_ref, wup_ref, cw_ref, cb_ref, wdn_ref, hist_ref,
         xo_ref, state_ref, scr) = refs
    rows = x_ref.shape[1]
    _zero_margins(scr, rows)
    x1 = x_ref[...].reshape(2 * rows, D) + _dot(
        mix_ref[...].reshape(2 * rows, D).astype(BF16), wo_ref[...])
    h = _dot(_rms(x1, g_ref[...]).astype(BF16), wup_ref[...])
    h0, h1 = h[:rows], h[rows:]
    hist = hist_ref[...].reshape(rows, D_FF2)
    conv = jnp.concatenate(_sample_conv(scr, cw_ref, h0, h1, hist), axis=0)
    state_ref[...] = h1.reshape(state_ref.shape)
    x2 = x1 + _dot(_silu_gate(conv + cb_ref[...]), wdn_ref[...])
    xo_ref[...] = x2.reshape(2, rows, D)
    if final:
        y_ref[...] = _rms(x2, gfin_ref[...]).reshape(2, rows, D)


def _ffn_sample(l, final, xs, mix, w_o, norm_ffn_g, w_up, conv_f_w, conv_f_b, w_down, cache_f,
                norm_final_g):
    rows = xs.shape[1]
    tm = rows // 2
    tok = pl.BlockSpec((2, tm, D), lambda i: (0, i, 0))
    wspec = lambda shape: _resident((None,) + shape, lambda i: (l,) + (0,) * len(shape))
    mat = lambda shape: _resident(shape, lambda i: (0, 0))
    x_shape = jax.ShapeDtypeStruct((2, rows, D), F32)
    in_specs = [tok, tok, mat((D, D)), wspec((1, D)), mat((D, D_FF2)), wspec((3, D_FF2)),
                wspec((1, D_FF2)), mat((D_FF, D)),
                pl.BlockSpec((None, tm // 2, 2, D_FF2), lambda i: (l, i, 0, 0))]
    args = [xs, mix, w_o, norm_ffn_g, w_up, conv_f_w, conv_f_b, w_down, cache_f]
    out_specs = [tok, pl.BlockSpec((tm // 2, 2, D_FF2), lambda i: (i, 0, 0))]
    out_shape = [x_shape, jax.ShapeDtypeStruct((rows // 2, 2, D_FF2), F32)]
    if final:
        in_specs.append(_resident((1, D), lambda i: (0, 0)))
        args.append(norm_final_g)
        out_specs.append(tok)
        out_shape.append(x_shape)
    return pl.pallas_call(
        functools.partial(_ffn_sample_kernel, final=final),
        grid=(rows // tm,),
        in_specs=in_specs,
        out_specs=out_specs,
        out_shape=out_shape,
        scratch_shapes=[pltpu.VMEM((tm + 2 * SUBLANES, D_FF2), F32)],
        compiler_params=pltpu.CompilerParams(
            dimension_semantics=("arbitrary",), vmem_limit_bytes=VMEM_LIMIT),
        name="ffn_sample",
    )(*args)


def _sample_gating_tables(sg_w, sg_b):
    w = sg_w[:, :, :SAMPLE_STEPS, :SAMPLE_STEPS]
    zero = jnp.zeros_like(w[..., 0, 0])
    by_parity = [(w[..., 0, 0], w[..., 1, 1]), (zero, w[..., 1, 0]), (w[..., 2, 0], w[..., 3, 1]),
                 (w[..., 2, 1], zero), (zero, w[..., 3, 0]), (w[..., 2, 2], w[..., 3, 3]),
                 (zero, w[..., 3, 2])]

    def table(pairs):
        t = jnp.stack([jnp.stack(p, axis=1) for p in pairs], axis=1)
        t = jnp.tile(t, (1, 1, SUBLANES // 2, 1))
        return jnp.repeat(t, SG_COLS, axis=3)

    b = sg_b[:, :, :SAMPLE_STEPS]
    return table(by_parity), table([(b[..., 0], b[..., 1]), (b[..., 2], b[..., 3])])


def kernel(x_prompt, x_sample, mem_prompt, cache_conv_a, cache_conv_ffn, cache_mem_k, cache_mem_v, norm_mix_g, w_in, conv_a_w, sg_ln_g, sg_w, sg_b, norm_mem_g, w_k, w_v, w_o, norm_ffn_g, w_up, conv_f_w, conv_f_b, w_down, norm_final_g):
    depth = w_in.shape[0]
    bp = x_prompt.shape[0]
    bs, ts, _ = x_sample.shape
    assert ts == SAMPLE_STEPS

    wkv_b = jnp.concatenate([w_k, w_v], axis=-1).astype(BF16)

    nmg = norm_mix_g[:, None, :]
    lng = sg_ln_g[:, None, :]
    nfg = norm_ffn_g[:, None, :]
    nmemg = norm_mem_g[:, None, :]
    cfb = conv_f_b[:, None, :]
    gfin = norm_final_g[None, :]
    sg_b_t = jnp.swapaxes(sg_b, 1, 2)
    sgc, sgbias = _sample_gating_tables(sg_w, sg_b)
    to_halves = lambda a: jnp.swapaxes(a.reshape(bs, 2, 2, D), 0, 1).reshape(2, bs * 2, D)
    from_halves = lambda a: jnp.swapaxes(a.reshape(2, bs, 2, D), 0, 1).reshape(bs, ts, D)
    xs = to_halves(x_sample)

    k_f, v_f, k_b, v_b, w_in_b = _mem_kv(mem_prompt.reshape(bp * N_MEM, D), nmemg, wkv_b, w_in)
    k_b = k_b.reshape(depth * bp, N_MEM, D)
    v_b = v_b.reshape(depth * bp, N_MEM, D)

    xp = x_prompt
    pa, pf, sa, sf, sv = [], [], [], [], []
    ys = None
    for l in range(depth):
        final = l == depth - 1
        qgm, vn, a_state_s = _mixer_sample(l, xs, nmg, w_in_b, conv_a_w, lng, sgc, sgbias,
                                           cache_conv_a)
        mix, a_state, mix_s, w_o_b, w_up_b, w_down_b = _mixer_prompt(
            l, xp, nmg, w_in_b, conv_a_w, lng, sg_w, sg_b_t, k_b, v_b, qgm,
            cache_mem_k, cache_mem_v, w_o, w_up, w_down)
        xp, f_state = _ffn_prompt(l, final, xp, mix, w_o_b, nfg, w_up_b, conv_f_w, cfb, w_down_b,
                                  gfin)
        pa.append(a_state)
        pf.append(f_state)
        outs = _ffn_sample(l, final, xs, mix_s, w_o_b, nfg, w_up_b, conv_f_w, cfb, w_down_b,
                           cache_conv_ffn, gfin)
        xs, f_state_s = outs[0], outs[1]
        if final:
            ys = outs[2]
        sa.append(a_state_s)
        sf.append(f_state_s)
        sv.append(from_halves(vn))

    kv_shape = (depth, bp, N_MEM, HEADS, HEAD_DIM)
    return (xp, from_halves(ys),
            jnp.stack(pa), jnp.stack(pf), k_f.reshape(kv_shape), v_f.reshape(kv_shape),
            jnp.stack(sa), jnp.stack(sf), jnp.stack(sv))
```

```python
import functools

import jax
import jax.numpy as jnp
from jax import lax
from jax.experimental import pallas as pl
from jax.experimental.pallas import tpu as pltpu

F32 = jnp.float32
BF16 = jnp.bfloat16

D = 1024
D_IN = 9 * D
D_FF = 2816
D_FF2 = 2 * D_FF
N_MEM = 256
HEADS = 4
HEAD_DIM = D // HEADS
SG_GROUPS = 4
SG_COLS = D // SG_GROUPS
SG_CHUNK = 128
EPS = 1e-6
H, C, B_, U, V, Q, GA, GB, GM = range(9)
PROJ_GROUPS = [(V, Q, U), (GB, C, H), (B_, GA, GM)]

V7X_VMEM_BYTES = 64 * 1024 * 1024
VMEM_LIMIT = V7X_VMEM_BYTES - 4 * 1024 * 1024
SUBLANES = 8

SUBS_MIX = (256, 256)
SUBS_FFN = (256, 256, 256, 256)
TM_KV = 512
ATT_BB = 4
MASKED = -1e30


def _rms(x, g):
    y = x * lax.rsqrt(jnp.mean(x * x, axis=-1, keepdims=True) + EPS)
    return y * g


def _layernorm(x, g):
    xc = x - jnp.mean(x, axis=-1, keepdims=True)
    return xc * lax.rsqrt(jnp.mean(xc * xc, axis=-1, keepdims=True) + EPS) * g


def _dot(a, b):
    return jnp.dot(a, b, preferred_element_type=F32)


def _resident(shape, index_map):
    return pl.BlockSpec(shape, index_map, pipeline_mode=pl.Buffered(1))


def _causal_conv(w_ref, hist, cur):
    n = hist.shape[0]
    ext = jnp.concatenate([hist, cur], axis=0)
    x1 = pltpu.roll(ext, 1, axis=0)[n:, :]
    x2 = pltpu.roll(ext, 2, axis=0)[n:, :]
    return w_ref[0:1, :] * x2 + w_ref[1:2, :] * x1 + w_ref[2:3, :] * cur


def _silu_gate(hc):
    return (jax.nn.silu(hc[:, D_FF:]) * hc[:, :D_FF]).astype(BF16)


def _kv_kernel(mem_ref, g_ref, wkv_ref, win_ref, k_ref, v_ref, kb_ref, vb_ref, winb_ref):
    winb_ref[...] = win_ref[...].astype(BF16)
    m = _rms(mem_ref[...], g_ref[...]).astype(BF16)
    k = _dot(m, wkv_ref[:, :D])
    v = _dot(m, wkv_ref[:, D:])
    for h in range(HEADS):
        hc = slice(h * HEAD_DIM, (h + 1) * HEAD_DIM)
        k_ref[:, h, :] = k[:, hc]
        v_ref[:, h, :] = v[:, hc]
    kb_ref[...] = k.astype(BF16)
    vb_ref[...] = v.astype(BF16)


def _mem_kv(mem2d, norm_mem_g, wkv, w_in):
    depth = wkv.shape[0]
    rows = mem2d.shape[0]
    steps = rows // TM_KV
    win_blk = pl.BlockSpec((None, D // steps, D_IN), lambda l, i: (l, i, 0))
    out_f = jax.ShapeDtypeStruct((depth, rows, HEADS, HEAD_DIM), F32)
    out_b = jax.ShapeDtypeStruct((depth, rows, D), BF16)
    oblk = pl.BlockSpec((None, TM_KV, D), lambda l, i: (l, i, 0))
    oblk_f = pl.BlockSpec((None, TM_KV, HEADS, HEAD_DIM), lambda l, i: (l, i, 0, 0))
    return pl.pallas_call(
        _kv_kernel,
        grid=(depth, rows // TM_KV),
        in_specs=[
            pl.BlockSpec((TM_KV, D), lambda l, i: (i, 0)),
            pl.BlockSpec((None, 1, D), lambda l, i: (l, 0, 0)),
            pl.BlockSpec((None, D, 2 * D), lambda l, i: (l, 0, 0)),
            win_blk,
        ],
        out_specs=[oblk_f, oblk_f, oblk, oblk, win_blk],
        out_shape=[out_f, out_f, out_b, out_b, jax.ShapeDtypeStruct(w_in.shape, BF16)],
        compiler_params=pltpu.CompilerParams(
            dimension_semantics=("arbitrary", "arbitrary"), vmem_limit_bytes=VMEM_LIMIT),
        name="mem_kv",
    )(mem2d, norm_mem_g, wkv, w_in)


def _attn_steps(qgm_ref, k_ref, v_ref, mix_ref):
    hb = qgm_ref.shape[2]
    nb = 2 * hb
    nq = HEADS * nb
    nk = N_MEM * HEADS
    tile = HEAD_DIM
    row = lax.broadcasted_iota(jnp.int32, (nq, nk), 0)
    col = lax.broadcasted_iota(jnp.int32, (nq, nk), 1)
    same_head = (row >> (nb.bit_length() - 1)) == (col & (HEADS - 1))
    row_b = (row & (hb - 1)) >> 1
    out_b = (lax.broadcasted_iota(jnp.int32, (nq, 1), 0) & (hb - 1)) >> 1
    q = qgm_ref[0].reshape(nb, D)
    q_all = jnp.concatenate([q[:, h * HEAD_DIM:(h + 1) * HEAD_DIM] for h in range(HEADS)],
                            axis=0).astype(BF16)
    st = {"y": jnp.zeros((nq, HEAD_DIM), F32)}

    def score_tile(b, j):
        if j == 0:
            st["k"] = k_ref[b].reshape(nk, HEAD_DIM).astype(BF16)
            st["sc"] = []
        st["sc"].append(lax.dot_general(q_all, st["k"][j * tile:(j + 1) * tile],
                                        (((1,), (1,)), ((), ())), preferred_element_type=F32))
        if j == nk // tile - 1:
            sc = jnp.concatenate(st["sc"], axis=1) * (HEAD_DIM ** -0.5)
            sc = jnp.where(same_head & (row_b == b), sc, MASKED)
            e = jnp.exp(sc - jnp.max(sc, axis=-1, keepdims=True))
            st["p", b] = (e * (1.0 / jnp.sum(e, axis=-1, keepdims=True))).astype(BF16)

    def value_tile(b, j):
        if j == 0:
            st["v"] = v_ref[b].reshape(nk, HEAD_DIM).astype(BF16)
            st["o"] = jnp.zeros((nq, HEAD_DIM), F32)
        st["o"] = st["o"] + _dot(st["p", b][:, j * tile:(j + 1) * tile],
                                 st["v"][j * tile:(j + 1) * tile])
        if j == nk // tile - 1:
            st["y"] = jnp.where(out_b == b, st["o"], st["y"])
            if b == ATT_BB - 1:
                y = jnp.concatenate([st["y"][h * nb:(h + 1) * nb] for h in range(HEADS)], axis=1)
                mix_ref[...] = qgm_ref[2] + qgm_ref[1] * y.reshape(2, hb, D)

    tiles = [(b, j) for b in range(ATT_BB) for j in range(nk // tile)]
    return ([functools.partial(score_tile, b, j) for b, j in tiles]
            + [functools.partial(value_tile, b, j) for b, j in tiles])


def _mixer_kernel(x_ref, nm_ref, win_ref, caw_ref, lng_ref, sgw_ref, sgb_ref, kb_ref, vb_ref,
                  sqgm_ref, sk_ref, sv_ref, wup_ref, wdn_ref,
                  mix_ref, state_ref, smix_ref, wupb_ref, wdnb_ref, a_scr,
                  *, subs, prev_rows):
    wupb_ref[...] = wup_ref[...].astype(BF16)
    wdnb_ref[...] = wdn_ref[...].astype(BF16)

    @pl.when(pl.program_id(1) == 0)
    def _():
        a_scr[...] = jnp.zeros((prev_rows, D), F32)

    ri = lax.broadcasted_iota(jnp.int32, (SG_CHUNK, SG_CHUNK), 0)
    ci = lax.broadcasted_iota(jnp.int32, (SG_CHUNK, SG_CHUNK), 1)
    sgw = [jnp.where(ci <= ri, sgw_ref[g], 0.0).astype(BF16) for g in range(SG_GROUPS)]
    offs = [sum(subs[:t]) for t in range(len(subs))]
    hist = [a_scr[...]]

    sample_steps = _attn_steps(sqgm_ref, sk_ref, sv_ref, smix_ref)
    chunk = 2 * SG_CHUNK
    n_chunks = D // chunk

    def take(n):
        steps = sample_steps[:n]
        del sample_steps[:n]
        return steps

    def proj(z, i, riders=()):
        riders = list(riders)
        cols = []
        for j in range(i * D, (i + 1) * D, chunk):
            cols.append(_dot(z, win_ref[:, j:j + chunk]))
            if riders:
                riders.pop(0)()
        assert not riders
        return jnp.concatenate(cols, axis=1)

    for step in take(3):
        step()
    for t in range(len(subs)):
        rs = slice(offs[t], offs[t] + subs[t])
        z = _rms(x_ref[rs, :], nm_ref[...]).astype(BF16)
        heads = [slice(h * HEAD_DIM, (h + 1) * HEAD_DIM) for h in range(HEADS)]

        v = proj(z, V, take(n_chunks))
        q = proj(z, Q, take(n_chunks))
        vnb = _layernorm(v, lng_ref[...]).astype(BF16)
        s_part = {}

        def spatial(jobs):
            for c, g in jobs:
                vg = vnb[c * SG_CHUNK:(c + 1) * SG_CHUNK, g * SG_COLS:(g + 1) * SG_COLS]
                s_part[c, g] = _dot(sgw[g], vg) + sgb_ref[:, g:g + 1]

        jobs = [(c, g) for c in range(subs[t] // SG_CHUNK) for g in range(SG_GROUPS)]
        per = -(-len(jobs) // n_chunks)
        u = proj(z, U, [functools.partial(spatial, jobs[k * per:(k + 1) * per])
                        for k in range(n_chunks)])
        s = jnp.concatenate([jnp.concatenate([s_part[c, g] for g in range(SG_GROUPS)], axis=1)
                             for c in range(subs[t] // SG_CHUNK)], axis=0)

        scores = {}

        def score(h):
            scores[h] = lax.dot_general(q[:, heads[h]].astype(BF16), kb_ref[:, heads[h]],
                                        (((1,), (1,)), ((), ())), preferred_element_type=F32)

        gb = proj(z, GB, [functools.partial(score, h) for h in range(HEADS)])
        c_ = proj(z, C, take(n_chunks))
        h_ = proj(z, H, take(n_chunks))
        ubs = jax.nn.sigmoid(gb) * u * s
        a = c_ * h_
        conv = _causal_conv(caw_ref, hist[0], a)
        hist[0] = a[subs[t] - prev_rows:, :]

        outs = {}

        def value(h):
            sc = scores[h] * (HEAD_DIM ** -0.5)
            e = jnp.exp(sc - jnp.max(sc, axis=-1, keepdims=True))
            p = e * (1.0 / jnp.sum(e, axis=-1, keepdims=True))
            outs[h] = _dot(p.astype(BF16), vb_ref[:, heads[h]])

        b_ = proj(z, B_, [functools.partial(value, h) for h in range(HEADS)])
        ga = proj(z, GA, take(n_chunks))
        gm = proj(z, GM)
        y_m = jnp.concatenate([outs[h] for h in range(HEADS)], axis=1)
        acc = jax.nn.sigmoid(ga) * (b_ * conv) + ubs + jax.nn.sigmoid(gm) * y_m
        mix_ref[rs, :] = acc.astype(BF16)
    for step in sample_steps:
        step()

    a_scr[...] = hist[0]

    @pl.when(pl.program_id(1) == pl.num_programs(1) - 1)
    def _():
        state_ref[...] = hist[0][prev_rows - 2:, :]


def _mixer_prompt(l, x, norm_mix_g, w_in, conv_a_w, sg_ln_g, sg_w, sg_b_t, kb, vb,
                  sqgm, cache_k, cache_v, w_up, w_down):
    bsz, t, _ = x.shape
    tm = sum(SUBS_MIX)
    assert len(SUBS_MIX) >= 2
    n_t = t // tm
    prev_rows = SUBLANES
    hb = 2 * ATT_BB
    assert hb & (hb - 1) == 0 and sqgm.shape[2] == hb * bsz * n_t
    wspec = lambda shape: _resident((None,) + shape, lambda b, i: (l,) + (0,) * len(shape))
    small = pl.BlockSpec((2, hb, D), lambda b, i: (0, b * n_t + i, 0))
    small3 = pl.BlockSpec((3, 2, hb, D), lambda b, i: (0, 0, b * n_t + i, 0))
    big = pl.BlockSpec((None, ATT_BB, N_MEM, HEADS, HEAD_DIM),
                       lambda b, i: (l, b * n_t + i, 0, 0, 0))
    steps = bsz * n_t
    bf16_rows = 2 * SUBLANES

    def cast_blocks(w):
        rows = w.shape[1]
        per = steps
        while rows % per or (rows // per) % bf16_rows:
            per //= 2
        blk = (rows // per,) + w.shape[2:]
        zeros = (0,) * (w.ndim - 2)
        src = pl.BlockSpec((None,) + blk, lambda b, i: (l, (b * n_t + i) * per // steps) + zeros)
        dst = pl.BlockSpec(blk, lambda b, i: ((b * n_t + i) * per // steps,) + zeros)
        return src, dst, jax.ShapeDtypeStruct(w.shape[1:], BF16)

    casts = [cast_blocks(w) for w in (w_up, w_down)]
    return pl.pallas_call(
        functools.partial(_mixer_kernel, subs=SUBS_MIX, prev_rows=prev_rows),
        grid=(bsz, t // tm),
        in_specs=[
            pl.BlockSpec((None, tm, D), lambda b, i: (b, i, 0)),
            wspec((1, D)),
            wspec((D, D_IN)),
            wspec((3, D)),
            wspec((1, D)),
            wspec((SG_GROUPS, SG_CHUNK, SG_CHUNK)),
            wspec((SG_CHUNK, SG_GROUPS)),
            pl.BlockSpec((None, N_MEM, D), lambda b, i: (l * bsz + b, 0, 0)),
            pl.BlockSpec((None, N_MEM, D), lambda b, i: (l * bsz + b, 0, 0)),
            small3, big, big,
        ] + [c[0] for c in casts],
        out_specs=[
            pl.BlockSpec((None, tm, D), lambda b, i: (b, i, 0)),
            pl.BlockSpec((None, 2, D), lambda b, i: (b, 0, 0)),
            small,
        ] + [c[1] for c in casts],
        out_shape=[
            jax.ShapeDtypeStruct((bsz, t, D), BF16),
            jax.ShapeDtypeStruct((bsz, 2, D), F32),
            jax.ShapeDtypeStruct(sqgm.shape[1:], F32),
        ] + [c[2] for c in casts],
        scratch_shapes=[pltpu.VMEM((prev_rows, D), F32)],
        compiler_params=pltpu.CompilerParams(
            dimension_semantics=("arbitrary", "arbitrary"), vmem_limit_bytes=VMEM_LIMIT),
        name="mixer_prompt",
    )(x, norm_mix_g, w_in, conv_a_w, sg_ln_g, sg_w, sg_b_t, kb, vb, sqgm, cache_k, cache_v,
      w_up, w_down)


def _ffn_kernel(*refs, final, subs, prev_rows):
    if final:
        (x_ref, mix_ref, wo_ref, g_ref, wup_ref, cw_ref, cb_ref, wdn_ref, gfin_ref,
         out_ref, state_ref, h_scr) = refs
    else:
        (x_ref, mix_ref, wo_ref, g_ref, wup_ref, cw_ref, cb_ref, wdn_ref,
         out_ref, state_ref, h_scr) = refs

    @pl.when(pl.program_id(1) == 0)
    def _():
        h_scr[...] = jnp.zeros((prev_rows, D_FF2), F32)

    offs = [sum(subs[:t]) for t in range(len(subs))]
    slices = [slice(o, o + n) for o, n in zip(offs, subs)]
    wo = wo_ref[...].astype(BF16)
    x1 = [x_ref[rs, :] + _dot(mix_ref[rs, :], wo) for rs in slices]
    h = [_dot(_rms(x1[t], g_ref[...]).astype(BF16), wup_ref[...]) for t in range(len(subs))]
    hist = h_scr[...]
    for t, rs in enumerate(slices):
        hc = _causal_conv(cw_ref, hist, h[t]) + cb_ref[...]
        hist = h[t][subs[t] - prev_rows:, :]
        x2 = x1[t] + _dot(_silu_gate(hc), wdn_ref[...])
        out_ref[rs, :] = _rms(x2, gfin_ref[...]) if final else x2
    h_scr[...] = hist

    @pl.when(pl.program_id(1) == pl.num_programs(1) - 1)
    def _():
        state_ref[...] = hist[prev_rows - 2:, :]


def _ffn_prompt(l, final, x, mix, w_o, norm_ffn_g, w_up, conv_f_w, conv_f_b, w_down, norm_final_g):
    bsz, t, _ = x.shape
    tm, prev_rows = sum(SUBS_FFN), SUBLANES
    tok = pl.BlockSpec((None, tm, D), lambda b, i: (b, i, 0))
    wspec = lambda shape: _resident((None,) + shape, lambda b, i: (l,) + (0,) * len(shape))
    mat = lambda shape: _resident(shape, lambda b, i: (0, 0))
    x_shape = jax.ShapeDtypeStruct((bsz, t, D), F32)
    in_specs = [tok, tok, wspec((D, D)), wspec((1, D)), mat((D, D_FF2)), wspec((3, D_FF2)),
                wspec((1, D_FF2)), mat((D_FF, D))]
    args = [x, mix, w_o, norm_ffn_g, w_up, conv_f_w, conv_f_b, w_down]
    out_specs = [tok, pl.BlockSpec((None, 2, D_FF2), lambda b, i: (b, 0, 0))]
    out_shape = [x_shape, jax.ShapeDtypeStruct((bsz, 2, D_FF2), F32)]
    if final:
        in_specs.append(_resident((1, D), lambda b, i: (0, 0)))
        args.append(norm_final_g)
    return pl.pallas_call(
        functools.partial(_ffn_kernel, final=final, subs=SUBS_FFN, prev_rows=prev_rows),
        grid=(bsz, t // tm),
        in_specs=in_specs,
        out_specs=out_specs,
        out_shape=out_shape,
        scratch_shapes=[pltpu.VMEM((prev_rows, D_FF2), F32)],
        compiler_params=pltpu.CompilerParams(
            dimension_semantics=("arbitrary", "arbitrary"), vmem_limit_bytes=VMEM_LIMIT),
        name="ffn_prompt",
    )(*args)


SAMPLE_STEPS = 4


def _put(scr, x):
    scr[SUBLANES:SUBLANES + x.shape[0], :] = x


def _get(scr, rows, k):
    return scr[SUBLANES - k:SUBLANES - k + rows, :]


def _zero_margins(scr, rows):
    scr[0:SUBLANES, :] = jnp.zeros((SUBLANES, scr.shape[1]), F32)
    scr[SUBLANES + rows:, :] = jnp.zeros((SUBLANES, scr.shape[1]), F32)


def _sample_conv(scr, w_ref, cur0, cur1, hist):
    rows = cur0.shape[0]
    first = (lax.broadcasted_iota(jnp.int32, (rows, 1), 0) & 1) == 0
    _put(scr, hist)
    hist_up = _get(scr, rows, -1)
    _put(scr, cur0)
    cur0_dn = _get(scr, rows, 1)
    cur0_up = _get(scr, rows, -1)
    _put(scr, cur1)
    cur1_dn = _get(scr, rows, 1)
    w0, w1, w2 = w_ref[0:1, :], w_ref[1:2, :], w_ref[2:3, :]
    c0 = w0 * hist + w1 * jnp.where(first, hist_up, cur0_dn) + w2 * cur0
    c1 = w0 * cur0 + w1 * jnp.where(first, cur0_up, cur1_dn) + w2 * cur1
    return c0, c1


def _mixer_sample_kernel(x_ref, nm_ref, win_ref, caw_ref, lng_ref, sgc_ref, sgb_ref, hist_ref,
                         qgm_ref, vn_ref, state_ref, scr):
    rows = x_ref.shape[1]
    halves = lambda a: (a[:rows], a[rows:])
    _zero_margins(scr, rows)
    z = _rms(x_ref[...].reshape(2 * rows, D), nm_ref[...]).astype(BF16)
    p = {}
    for group in PROJ_GROUPS:
        for i in group:
            p[i] = _dot(z, win_ref[:, i * D:(i + 1) * D])

    vn0, vn1 = halves(_layernorm(p[V], lng_ref[...]))
    vn_ref[0] = vn0
    vn_ref[1] = vn1
    tiles = lambda a: a.reshape(rows // SUBLANES, SUBLANES, D)
    coef = lambda k: sgc_ref[k][None]
    _put(scr, vn0)
    vn0_dn, vn0_up = tiles(_get(scr, rows, 1)), tiles(_get(scr, rows, -1))
    _put(scr, vn1)
    vn1_dn = tiles(_get(scr, rows, 1))
    s0 = coef(0) * tiles(vn0) + coef(1) * vn0_dn + sgb_ref[0][None]
    s1 = (coef(2) * tiles(vn0) + coef(3) * vn0_up + coef(4) * vn0_dn
          + coef(5) * tiles(vn1) + coef(6) * vn1_dn + sgb_ref[1][None])
    s = jnp.concatenate([s0.reshape(rows, D), s1.reshape(rows, D)], axis=0)

    a0, a1 = halves(p[C] * p[H])
    hist = hist_ref[...].reshape(rows, D)
    conv = jnp.concatenate(_sample_conv(scr, caw_ref, a0, a1, hist), axis=0)
    state_ref[...] = a1.reshape(state_ref.shape)
    acc = jax.nn.sigmoid(p[GA]) * (p[B_] * conv)
    acc = acc + jax.nn.sigmoid(p[GB]) * (p[U] * s)
    qgm_ref[0] = p[Q].reshape(2, rows, D)
    qgm_ref[1] = jax.nn.sigmoid(p[GM]).reshape(2, rows, D)
    qgm_ref[2] = acc.reshape(2, rows, D)


def _mixer_sample(l, xs, norm_mix_g, w_in, conv_a_w, sg_ln_g, sgc, sgb, cache_a):
    rows = xs.shape[1]
    tm = rows // 2
    wspec = lambda shape: _resident((None,) + shape, lambda i: (l,) + (0,) * len(shape))
    tok = pl.BlockSpec((2, tm, D), lambda i: (0, i, 0))
    hist = pl.BlockSpec((None, tm // 2, 2, D), lambda i: (l, i, 0, 0))
    out_f = jax.ShapeDtypeStruct((2, rows, D), F32)
    return pl.pallas_call(
        _mixer_sample_kernel,
        grid=(rows // tm,),
        in_specs=[tok, wspec((1, D)), wspec((D, D_IN)), wspec((3, D)), wspec((1, D)),
                  wspec((7, SUBLANES, D)), wspec((2, SUBLANES, D)), hist],
        out_specs=[pl.BlockSpec((3, 2, tm, D), lambda i: (0, 0, i, 0)), tok,
                   pl.BlockSpec((tm // 2, 2, D), lambda i: (i, 0, 0))],
        out_shape=[jax.ShapeDtypeStruct((3, 2, rows, D), F32), out_f,
                   jax.ShapeDtypeStruct((rows // 2, 2, D), F32)],
        scratch_shapes=[pltpu.VMEM((tm + 2 * SUBLANES, D), F32)],
        compiler_params=pltpu.CompilerParams(
            dimension_semantics=("arbitrary",), vmem_limit_bytes=VMEM_LIMIT),
        name="mixer_sample",
    )(xs, norm_mix_g, w_in, conv_a_w, sg_ln_g, sgc, sgb, cache_a)


def _ffn_sample_kernel(*refs, final):
    if final:
        (x_ref, mix_ref, wo_ref, g_ref, wup_ref, cw_ref, cb_ref, wdn_ref, hist_ref, gfin_ref,
         xo_ref, state_ref, y_ref, scr) = refs
    else:
        (x_ref, mix_ref, wo_ref, g_ref, wup_ref, cw_ref, cb_ref, wdn_ref, hist_ref,
         xo_ref, state_ref, scr) = refs
    rows = x_ref.shape[1]
    _zero_margins(scr, rows)
    x1 = x_ref[...].reshape(2 * rows, D) + _dot(
        mix_ref[...].reshape(2 * rows, D).astype(BF16), wo_ref[...].astype(BF16))
    h = _dot(_rms(x1, g_ref[...]).astype(BF16), wup_ref[...])
    h0, h1 = h[:rows], h[rows:]
    hist = hist_ref[...].reshape(rows, D_FF2)
    conv = jnp.concatenate(_sample_conv(scr, cw_ref, h0, h1, hist), axis=0)
    state_ref[...] = h1.reshape(state_ref.shape)
    x2 = x1 + _dot(_silu_gate(conv + cb_ref[...]), wdn_ref[...])
    xo_ref[...] = x2.reshape(2, rows, D)
    if final:
        y_ref[...] = _rms(x2, gfin_ref[...]).reshape(2, rows, D)


def _ffn_sample(l, final, xs, mix, w_o, norm_ffn_g, w_up, conv_f_w, conv_f_b, w_down, cache_f,
                norm_final_g):
    rows = xs.shape[1]
    tm = rows // 2
    tok = pl.BlockSpec((2, tm, D), lambda i: (0, i, 0))
    wspec = lambda shape: _resident((None,) + shape, lambda i: (l,) + (0,) * len(shape))
    mat = lambda shape: _resident(shape, lambda i: (0, 0))
    x_shape = jax.ShapeDtypeStruct((2, rows, D), F32)
    in_specs = [tok, tok, wspec((D, D)), wspec((1, D)), mat((D, D_FF2)), wspec((3, D_FF2)),
                wspec((1, D_FF2)), mat((D_FF, D)),
                pl.BlockSpec((None, tm // 2, 2, D_FF2), lambda i: (l, i, 0, 0))]
    args = [xs, mix, w_o, norm_ffn_g, w_up, conv_f_w, conv_f_b, w_down, cache_f]
    out_specs = [tok, pl.BlockSpec((tm // 2, 2, D_FF2), lambda i: (i, 0, 0))]
    out_shape = [x_shape, jax.ShapeDtypeStruct((rows // 2, 2, D_FF2), F32)]
    if final:
        in_specs.append(_resident((1, D), lambda i: (0, 0)))
        args.append(norm_final_g)
        out_specs.append(tok)
        out_shape.append(x_shape)
    return pl.pallas_call(
        functools.partial(_ffn_sample_kernel, final=final),
        grid=(rows // tm,),
        in_specs=in_specs,
        out_specs=out_specs,
        out_shape=out_shape,
        scratch_shapes=[pltpu.VMEM((tm + 2 * SUBLANES, D_FF2), F32)],
        compiler_params=pltpu.CompilerParams(
            dimension_semantics=("arbitrary",), vmem_limit_bytes=VMEM_LIMIT),
        name="ffn_sample",
    )(*args)


def _sample_gating_tables(sg_w, sg_b):
    w = sg_w[:, :, :SAMPLE_STEPS, :SAMPLE_STEPS]
    zero = jnp.zeros_like(w[..., 0, 0])
    by_parity = [(w[..., 0, 0], w[..., 1, 1]), (zero, w[..., 1, 0]), (w[..., 2, 0], w[..., 3, 1]),
                 (w[..., 2, 1], zero), (zero, w[..., 3, 0]), (w[..., 2, 2], w[..., 3, 3]),
                 (zero, w[..., 3, 2])]

    def table(pairs):
        t = jnp.stack([jnp.stack(p, axis=1) for p in pairs], axis=1)
        t = jnp.tile(t, (1, 1, SUBLANES // 2, 1))
        return jnp.repeat(t, SG_COLS, axis=3)

    b = sg_b[:, :, :SAMPLE_STEPS]
    return table(by_parity), table([(b[..., 0], b[..., 1]), (b[..., 2], b[..., 3])])


def kernel(x_prompt, x_sample, mem_prompt, cache_conv_a, cache_conv_ffn, cache_mem_k, cache_mem_v, norm_mix_g, w_in, conv_a_w, sg_ln_g, sg_w, sg_b, norm_mem_g, w_k, w_v, w_o, norm_ffn_g, w_up, conv_f_w, conv_f_b, w_down, norm_final_g):
    depth = w_in.shape[0]
    bp = x_prompt.shape[0]
    bs, ts, _ = x_sample.shape
    assert ts == SAMPLE_STEPS

    wkv_b = jnp.concatenate([w_k, w_v], axis=-1).astype(BF16)

    nmg = norm_mix_g[:, None, :]
    lng = sg_ln_g[:, None, :]
    nfg = norm_ffn_g[:, None, :]
    nmemg = norm_mem_g[:, None, :]
    cfb = conv_f_b[:, None, :]
    gfin = norm_final_g[None, :]
    sg_b_t = jnp.swapaxes(sg_b, 1, 2)
    sgc, sgbias = _sample_gating_tables(sg_w, sg_b)
    to_halves = lambda a: jnp.swapaxes(a.reshape(bs, 2, 2, D), 0, 1).reshape(2, bs * 2, D)
    from_halves = lambda a: jnp.swapaxes(a.reshape(2, bs, 2, D), 0, 1).reshape(bs, ts, D)
    xs = to_halves(x_sample)

    k_f, v_f, k_b, v_b, w_in_b = _mem_kv(mem_prompt.reshape(bp * N_MEM, D), nmemg, wkv_b, w_in)
    k_b = k_b.reshape(depth * bp, N_MEM, D)
    v_b = v_b.reshape(depth * bp, N_MEM, D)

    xp = x_prompt
    pa, pf, sa, sf, sv = [], [], [], [], []
    ys = None
    for l in range(depth):
        final = l == depth - 1
        qgm, vn, a_state_s = _mixer_sample(l, xs, nmg, w_in_b, conv_a_w, lng, sgc, sgbias,
                                           cache_conv_a)
        mix, a_state, mix_s, w_up_b, w_down_b = _mixer_prompt(
            l, xp, nmg, w_in_b, conv_a_w, lng, sg_w, sg_b_t, k_b, v_b, qgm,
            cache_mem_k, cache_mem_v, w_up, w_down)
        xp, f_state = _ffn_prompt(l, final, xp, mix, w_o, nfg, w_up_b, conv_f_w, cfb, w_down_b,
                                  gfin)
        pa.append(a_state)
        pf.append(f_state)
        outs = _ffn_sample(l, final, xs, mix_s, w_o, nfg, w_up_b, conv_f_w, cfb, w_down_b,
                           cache_conv_ffn, gfin)
        xs, f_state_s = outs[0], outs[1]
        if final:
            ys = outs[2]
        sa.append(a_state_s)
        sf.append(f_state_s)
        sv.append(from_halves(vn))

    kv_shape = (depth, bp, N_MEM, HEADS, HEAD_DIM)
    return (xp, from_halves(ys),
            jnp.stack(pa), jnp.stack(pf), k_f.reshape(kv_shape), v_f.reshape(kv_shape),
            jnp.stack(sa), jnp.stack(sf), jnp.stack(sv))
```

```python
import functools

import jax
import jax.numpy as jnp
from jax import lax
from jax.experimental import pallas as pl
from jax.experimental.pallas import tpu as pltpu

F32 = jnp.float32
BF16 = jnp.bfloat16

D = 1024
D_IN = 9 * D
D_FF = 2816
D_FF2 = 2 * D_FF
N_MEM = 256
HEADS = 4
HEAD_DIM = D // HEADS
SG_GROUPS = 4
SG_COLS = D // SG_GROUPS
SG_CHUNK = 128
EPS = 1e-6
H, C, B_, U, V, Q, GA, GB, GM = range(9)
PROJ_GROUPS = [(V, Q, U), (GB, C, H), (B_, GA, GM)]

V7X_VMEM_BYTES = 64 * 1024 * 1024
VMEM_LIMIT = V7X_VMEM_BYTES - 4 * 1024 * 1024
SUBLANES = 8

SUBS_MIX = (256, 256)
SUBS_FFN = (256, 256, 256, 256)
TM_KV = 512
ATT_BB = 4
MASKED = -1e30


def _rms(x, g):
    y = x * lax.rsqrt(jnp.mean(x * x, axis=-1, keepdims=True) + EPS)
    return y * g


def _layernorm(x, g):
    xc = x - jnp.mean(x, axis=-1, keepdims=True)
    return xc * lax.rsqrt(jnp.mean(xc * xc, axis=-1, keepdims=True) + EPS) * g


def _dot(a, b):
    return jnp.dot(a, b, preferred_element_type=F32)


def _resident(shape, index_map):
    return pl.BlockSpec(shape, index_map, pipeline_mode=pl.Buffered(1))


def _causal_conv(w_ref, hist, cur):
    n = hist.shape[0]
    ext = jnp.concatenate([hist, cur], axis=0)
    x1 = pltpu.roll(ext, 1, axis=0)[n:, :]
    x2 = pltpu.roll(ext, 2, axis=0)[n:, :]
    return w_ref[0:1, :] * x2 + w_ref[1:2, :] * x1 + w_ref[2:3, :] * cur


def _silu_gate(hc):
    return (jax.nn.silu(hc[:, D_FF:]) * hc[:, :D_FF]).astype(BF16)


def _kv_kernel(mem_ref, g_ref, wkv_ref, win_ref, k_ref, v_ref, kb_ref, vb_ref, winb_ref):
    winb_ref[...] = win_ref[...].astype(BF16)
    m = _rms(mem_ref[...], g_ref[...]).astype(BF16)
    k = _dot(m, wkv_ref[:, :D])
    v = _dot(m, wkv_ref[:, D:])
    for h in range(HEADS):
        hc = slice(h * HEAD_DIM, (h + 1) * HEAD_DIM)
        k_ref[:, h, :] = k[:, hc]
        v_ref[:, h, :] = v[:, hc]
    kb_ref[...] = k.astype(BF16)
    vb_ref[...] = v.astype(BF16)


def _mem_kv(mem2d, norm_mem_g, wkv, w_in):
    depth = wkv.shape[0]
    rows = mem2d.shape[0]
    steps = rows // TM_KV
    win_blk = pl.BlockSpec((None, D // steps, D_IN), lambda l, i: (l, i, 0))
    out_f = jax.ShapeDtypeStruct((depth, rows, HEADS, HEAD_DIM), F32)
    out_b = jax.ShapeDtypeStruct((depth, rows, D), BF16)
    oblk = pl.BlockSpec((None, TM_KV, D), lambda l, i: (l, i, 0))
    oblk_f = pl.BlockSpec((None, TM_KV, HEADS, HEAD_DIM), lambda l, i: (l, i, 0, 0))
    return pl.pallas_call(
        _kv_kernel,
        grid=(depth, rows // TM_KV),
        in_specs=[
            pl.BlockSpec((TM_KV, D), lambda l, i: (i, 0)),
            pl.BlockSpec((None, 1, D), lambda l, i: (l, 0, 0)),
            pl.BlockSpec((None, D, 2 * D), lambda l, i: (l, 0, 0)),
            win_blk,
        ],
        out_specs=[oblk_f, oblk_f, oblk, oblk, win_blk],
        out_shape=[out_f, out_f, out_b, out_b, jax.ShapeDtypeStruct(w_in.shape, BF16)],
        compiler_params=pltpu.CompilerParams(
            dimension_semantics=("arbitrary", "arbitrary"), vmem_limit_bytes=VMEM_LIMIT),
        name="mem_kv",
    )(mem2d, norm_mem_g, wkv, w_in)


def _attn_steps(qgm_ref, k_ref, v_ref, mix_ref):
    hb = qgm_ref.shape[2]
    nb = 2 * hb
    nq = HEADS * nb
    nk = N_MEM * HEADS
    tile = HEAD_DIM
    row = lax.broadcasted_iota(jnp.int32, (nq, nk), 0)
    col = lax.broadcasted_iota(jnp.int32, (nq, nk), 1)
    same_head = (row >> (nb.bit_length() - 1)) == (col & (HEADS - 1))
    row_b = (row & (hb - 1)) >> 1
    out_b = (lax.broadcasted_iota(jnp.int32, (nq, 1), 0) & (hb - 1)) >> 1
    q = qgm_ref[0].reshape(nb, D)
    q_all = jnp.concatenate([q[:, h * HEAD_DIM:(h + 1) * HEAD_DIM] for h in range(HEADS)],
                            axis=0).astype(BF16)
    st = {"y": jnp.zeros((nq, HEAD_DIM), F32)}

    def score_tile(b, j):
        if j == 0:
            st["k"] = k_ref[b].reshape(nk, HEAD_DIM).astype(BF16)
            st["sc"] = []
        st["sc"].append(lax.dot_general(q_all, st["k"][j * tile:(j + 1) * tile],
                                        (((1,), (1,)), ((), ())), preferred_element_type=F32))
        if j == nk // tile - 1:
            sc = jnp.concatenate(st["sc"], axis=1) * (HEAD_DIM ** -0.5)
            sc = jnp.where(same_head & (row_b == b), sc, MASKED)
            e = jnp.exp(sc - jnp.max(sc, axis=-1, keepdims=True))
            st["p", b] = (e * (1.0 / jnp.sum(e, axis=-1, keepdims=True))).astype(BF16)

    def value_tile(b, j):
        if j == 0:
            st["v"] = v_ref[b].reshape(nk, HEAD_DIM).astype(BF16)
            st["o"] = jnp.zeros((nq, HEAD_DIM), F32)
        st["o"] = st["o"] + _dot(st["p", b][:, j * tile:(j + 1) * tile],
                                 st["v"][j * tile:(j + 1) * tile])
        if j == nk // tile - 1:
            st["y"] = jnp.where(out_b == b, st["o"], st["y"])
            if b == ATT_BB - 1:
                y = jnp.concatenate([st["y"][h * nb:(h + 1) * nb] for h in range(HEADS)], axis=1)
                mix_ref[...] = qgm_ref[2] + qgm_ref[1] * y.reshape(2, hb, D)

    tiles = [(b, j) for b in range(ATT_BB) for j in range(nk // tile)]
    return ([functools.partial(score_tile, b, j) for b, j in tiles]
            + [functools.partial(value_tile, b, j) for b, j in tiles])


def _mixer_kernel(x_ref, nm_ref, win_ref, caw_ref, lng_ref, sgw_ref, sgb_ref, kb_ref, vb_ref,
                  sqgm_ref, sk_ref, sv_ref, wo_ref, wup_ref, wdn_ref,
                  mix_ref, state_ref, smix_ref, wob_ref, wupb_ref, wdnb_ref, a_scr,
                  *, subs, prev_rows):
    wob_ref[...] = wo_ref[...].astype(BF16)
    wupb_ref[...] = wup_ref[...].astype(BF16)
    wdnb_ref[...] = wdn_ref[...].astype(BF16)

    @pl.when(pl.program_id(1) == 0)
    def _():
        a_scr[...] = jnp.zeros((prev_rows, D), F32)

    ri = lax.broadcasted_iota(jnp.int32, (SG_CHUNK, SG_CHUNK), 0)
    ci = lax.broadcasted_iota(jnp.int32, (SG_CHUNK, SG_CHUNK), 1)
    sgw = [jnp.where(ci <= ri, sgw_ref[g], 0.0).astype(BF16) for g in range(SG_GROUPS)]
    offs = [sum(subs[:t]) for t in range(len(subs))]
    hist = [a_scr[...]]

    sample_steps = _attn_steps(sqgm_ref, sk_ref, sv_ref, smix_ref)
    chunk = 2 * SG_CHUNK
    for _ in range(3):
        sample_steps.pop(0)()
    for t in range(len(subs)):
        rs = slice(offs[t], offs[t] + subs[t])
        z = _rms(x_ref[rs, :], nm_ref[...]).astype(BF16)

        def proj(i, ride_along=False):
            cols = []
            for j in range(i * D, (i + 1) * D, chunk):
                cols.append(_dot(z, win_ref[:, j:j + chunk]))
                if ride_along and sample_steps:
                    sample_steps.pop(0)()
            return jnp.concatenate(cols, axis=1)

        v, q, u = proj(V), proj(Q), proj(U)
        vnb = _layernorm(v, lng_ref[...]).astype(BF16)
        s_rows = []
        for c in range(subs[t] // SG_CHUNK):
            s_cols = []
            for g in range(SG_GROUPS):
                vg = vnb[c * SG_CHUNK:(c + 1) * SG_CHUNK, g * SG_COLS:(g + 1) * SG_COLS]
                s_cols.append(_dot(sgw[g], vg) + sgb_ref[:, g:g + 1])
            s_rows.append(jnp.concatenate(s_cols, axis=1))
        s = jnp.concatenate(s_rows, axis=0)
        scores = [
            lax.dot_general(q[:, h * HEAD_DIM:(h + 1) * HEAD_DIM].astype(BF16),
                            kb_ref[:, h * HEAD_DIM:(h + 1) * HEAD_DIM],
                            (((1,), (1,)), ((), ())), preferred_element_type=F32)
            for h in range(HEADS)]

        gb, c_, h_ = proj(GB, True), proj(C, True), proj(H, True)
        ubs = jax.nn.sigmoid(gb) * u * s
        a = c_ * h_
        conv = _causal_conv(caw_ref, hist[0], a)
        hist[0] = a[subs[t] - prev_rows:, :]
        o_cols = []
        for h in range(HEADS):
            sc = scores[h] * (HEAD_DIM ** -0.5)
            e = jnp.exp(sc - jnp.max(sc, axis=-1, keepdims=True))
            p = e * (1.0 / jnp.sum(e, axis=-1, keepdims=True))
            o_cols.append(_dot(p.astype(BF16), vb_ref[:, h * HEAD_DIM:(h + 1) * HEAD_DIM]))
        y_m = jnp.concatenate(o_cols, axis=1)

        b_, ga, gm = proj(B_, True), proj(GA, True), proj(GM)
        acc = jax.nn.sigmoid(ga) * (b_ * conv) + ubs + jax.nn.sigmoid(gm) * y_m
        mix_ref[rs, :] = acc.astype(BF16)
    for step in sample_steps:
        step()

    a_scr[...] = hist[0]

    @pl.when(pl.program_id(1) == pl.num_programs(1) - 1)
    def _():
        state_ref[...] = hist[0][prev_rows - 2:, :]


def _mixer_prompt(l, x, norm_mix_g, w_in, conv_a_w, sg_ln_g, sg_w, sg_b_t, kb, vb,
                  sqgm, cache_k, cache_v, w_o, w_up, w_down):
    bsz, t, _ = x.shape
    tm = sum(SUBS_MIX)
    assert len(SUBS_MIX) >= 2
    n_t = t // tm
    prev_rows = SUBLANES
    hb = 2 * ATT_BB
    assert hb & (hb - 1) == 0 and sqgm.shape[2] == hb * bsz * n_t
    wspec = lambda shape: _resident((None,) + shape, lambda b, i: (l,) + (0,) * len(shape))
    small = pl.BlockSpec((2, hb, D), lambda b, i: (0, b * n_t + i, 0))
    small3 = pl.BlockSpec((3, 2, hb, D), lambda b, i: (0, 0, b * n_t + i, 0))
    big = pl.BlockSpec((None, ATT_BB, N_MEM, HEADS, HEAD_DIM),
                       lambda b, i: (l, b * n_t + i, 0, 0, 0))
    steps = bsz * n_t
    bf16_rows = 2 * SUBLANES

    def cast_blocks(w):
        rows = w.shape[1]
        per = steps
        while rows % per or (rows // per) % bf16_rows:
            per //= 2
        blk = (rows // per,) + w.shape[2:]
        zeros = (0,) * (w.ndim - 2)
        src = pl.BlockSpec((None,) + blk, lambda b, i: (l, (b * n_t + i) * per // steps) + zeros)
        dst = pl.BlockSpec(blk, lambda b, i: ((b * n_t + i) * per // steps,) + zeros)
        return src, dst, jax.ShapeDtypeStruct(w.shape[1:], BF16)

    casts = [cast_blocks(w) for w in (w_o, w_up, w_down)]
    return pl.pallas_call(
        functools.partial(_mixer_kernel, subs=SUBS_MIX, prev_rows=prev_rows),
        grid=(bsz, t // tm),
        in_specs=[
            pl.BlockSpec((None, tm, D), lambda b, i: (b, i, 0)),
            wspec((1, D)),
            wspec((D, D_IN)),
            wspec((3, D)),
            wspec((1, D)),
            wspec((SG_GROUPS, SG_CHUNK, SG_CHUNK)),
            wspec((SG_CHUNK, SG_GROUPS)),
            pl.BlockSpec((None, N_MEM, D), lambda b, i: (l * bsz + b, 0, 0)),
            pl.BlockSpec((None, N_MEM, D), lambda b, i: (l * bsz + b, 0, 0)),
            small3, big, big,
        ] + [c[0] for c in casts],
        out_specs=[
            pl.BlockSpec((None, tm, D), lambda b, i: (b, i, 0)),
            pl.BlockSpec((None, 2, D), lambda b, i: (b, 0, 0)),
            small,
        ] + [c[1] for c in casts],
        out_shape=[
            jax.ShapeDtypeStruct((bsz, t, D), BF16),
            jax.ShapeDtypeStruct((bsz, 2, D), F32),
            jax.ShapeDtypeStruct(sqgm.shape[1:], F32),
        ] + [c[2] for c in casts],
        scratch_shapes=[pltpu.VMEM((prev_rows, D), F32)],
        compiler_params=pltpu.CompilerParams(
            dimension_semantics=("arbitrary", "arbitrary"), vmem_limit_bytes=VMEM_LIMIT),
        name="mixer_prompt",
    )(x, norm_mix_g, w_in, conv_a_w, sg_ln_g, sg_w, sg_b_t, kb, vb, sqgm, cache_k, cache_v,
      w_o, w_up, w_down)


def _ffn_kernel(*refs, final, subs, prev_rows):
    if final:
        (x_ref, mix_ref, wo_ref, g_ref, wup_ref, cw_ref, cb_ref, wdn_ref, gfin_ref,
         out_ref, state_ref, h_scr) = refs
    else:
        (x_ref, mix_ref, wo_ref, g_ref, wup_ref, cw_ref, cb_ref, wdn_ref,
         out_ref, state_ref, h_scr) = refs

    @pl.when(pl.program_id(1) == 0)
    def _():
        h_scr[...] = jnp.zeros((prev_rows, D_FF2), F32)

    offs = [sum(subs[:t]) for t in range(len(subs))]
    slices = [slice(o, o + n) for o, n in zip(offs, subs)]
    x1 = [x_ref[rs, :] + _dot(mix_ref[rs, :], wo_ref[...]) for rs in slices]
    h = [_dot(_rms(x1[t], g_ref[...]).astype(BF16), wup_ref[...]) for t in range(len(subs))]
    hist = h_scr[...]
    for t, rs in enumerate(slices):
        hc = _causal_conv(cw_ref, hist, h[t]) + cb_ref[...]
        hist = h[t][subs[t] - prev_rows:, :]
        x2 = x1[t] + _dot(_silu_gate(hc), wdn_ref[...])
        out_ref[rs, :] = _rms(x2, gfin_ref[...]) if final else x2
    h_scr[...] = hist

    @pl.when(pl.program_id(1) == pl.num_programs(1) - 1)
    def _():
        state_ref[...] = hist[prev_rows - 2:, :]


def _ffn_prompt(l, final, x, mix, w_o, norm_ffn_g, w_up, conv_f_w, conv_f_b, w_down, norm_final_g):
    bsz, t, _ = x.shape
    tm, prev_rows = sum(SUBS_FFN), SUBLANES
    tok = pl.BlockSpec((None, tm, D), lambda b, i: (b, i, 0))
    wspec = lambda shape: _resident((None,) + shape, lambda b, i: (l,) + (0,) * len(shape))
    mat = lambda shape: _resident(shape, lambda b, i: (0, 0))
    x_shape = jax.ShapeDtypeStruct((bsz, t, D), F32)
    in_specs = [tok, tok, mat((D, D)), wspec((1, D)), mat((D, D_FF2)), wspec((3, D_FF2)),
                wspec((1, D_FF2)), mat((D_FF, D))]
    args = [x, mix, w_o, norm_ffn_g, w_up, conv_f_w, conv_f_b, w_down]
    out_specs = [tok, pl.BlockSpec((None, 2, D_FF2), lambda b, i: (b, 0, 0))]
    out_shape = [x_shape, jax.ShapeDtypeStruct((bsz, 2, D_FF2), F32)]
    if final:
        in_specs.append(_resident((1, D), lambda b, i: (0, 0)))
        args.append(norm_final_g)
    return pl.pallas_call(
        functools.partial(_ffn_kernel, final=final, subs=SUBS_FFN, prev_rows=prev_rows),
        grid=(bsz, t // tm),
        in_specs=in_specs,
        out_specs=out_specs,
        out_shape=out_shape,
        scratch_shapes=[pltpu.VMEM((prev_rows, D_FF2), F32)],
        compiler_params=pltpu.CompilerParams(
            dimension_semantics=("arbitrary", "arbitrary"), vmem_limit_bytes=VMEM_LIMIT),
        name="ffn_prompt",
    )(*args)


SAMPLE_STEPS = 4


def _put(scr, x):
    scr[SUBLANES:SUBLANES + x.shape[0], :] = x


def _get(scr, rows, k):
    return scr[SUBLANES - k:SUBLANES - k + rows, :]


def _zero_margins(scr, rows):
    scr[0:SUBLANES, :] = jnp.zeros((SUBLANES, scr.shape[1]), F32)
    scr[SUBLANES + rows:, :] = jnp.zeros((SUBLANES, scr.shape[1]), F32)


def _sample_conv(scr, w_ref, cur0, cur1, hist):
    rows = cur0.shape[0]
    first = (lax.broadcasted_iota(jnp.int32, (rows, 1), 0) & 1) == 0
    _put(scr, hist)
    hist_up = _get(scr, rows, -1)
    _put(scr, cur0)
    cur0_dn = _get(scr, rows, 1)
    cur0_up = _get(scr, rows, -1)
    _put(scr, cur1)
    cur1_dn = _get(scr, rows, 1)
    w0, w1, w2 = w_ref[0:1, :], w_ref[1:2, :], w_ref[2:3, :]
    c0 = w0 * hist + w1 * jnp.where(first, hist_up, cur0_dn) + w2 * cur0
    c1 = w0 * cur0 + w1 * jnp.where(first, cur0_up, cur1_dn) + w2 * cur1
    return c0, c1


def _mixer_sample_kernel(x_ref, nm_ref, win_ref, caw_ref, lng_ref, sgc_ref, sgb_ref, hist_ref,
                         qgm_ref, vn_ref, state_ref, scr):
    rows = x_ref.shape[1]
    halves = lambda a: (a[:rows], a[rows:])
    _zero_margins(scr, rows)
    z = _rms(x_ref[...].reshape(2 * rows, D), nm_ref[...]).astype(BF16)
    p = {}
    for group in PROJ_GROUPS:
        for i in group:
            p[i] = _dot(z, win_ref[:, i * D:(i + 1) * D])

    vn0, vn1 = halves(_layernorm(p[V], lng_ref[...]))
    vn_ref[0] = vn0
    vn_ref[1] = vn1
    tiles = lambda a: a.reshape(rows // SUBLANES, SUBLANES, D)
    coef = lambda k: sgc_ref[k][None]
    _put(scr, vn0)
    vn0_dn, vn0_up = tiles(_get(scr, rows, 1)), tiles(_get(scr, rows, -1))
    _put(scr, vn1)
    vn1_dn = tiles(_get(scr, rows, 1))
    s0 = coef(0) * tiles(vn0) + coef(1) * vn0_dn + sgb_ref[0][None]
    s1 = (coef(2) * tiles(vn0) + coef(3) * vn0_up + coef(4) * vn0_dn
          + coef(5) * tiles(vn1) + coef(6) * vn1_dn + sgb_ref[1][None])
    s = jnp.concatenate([s0.reshape(rows, D), s1.reshape(rows, D)], axis=0)

    a0, a1 = halves(p[C] * p[H])
    hist = hist_ref[...].reshape(rows, D)
    conv = jnp.concatenate(_sample_conv(scr, caw_ref, a0, a1, hist), axis=0)
    state_ref[...] = a1.reshape(state_ref.shape)
    acc = jax.nn.sigmoid(p[GA]) * (p[B_] * conv)
    acc = acc + jax.nn.sigmoid(p[GB]) * (p[U] * s)
    qgm_ref[0] = p[Q].reshape(2, rows, D)
    qgm_ref[1] = jax.nn.sigmoid(p[GM]).reshape(2, rows, D)
    qgm_ref[2] = acc.reshape(2, rows, D)


def _mixer_sample(l, xs, norm_mix_g, w_in, conv_a_w, sg_ln_g, sgc, sgb, cache_a):
    rows = xs.shape[1]
    tm = rows // 2
    wspec = lambda shape: _resident((None,) + shape, lambda i: (l,) + (0,) * len(shape))
    tok = pl.BlockSpec((2, tm, D), lambda i: (0, i, 0))
    hist = pl.BlockSpec((None, tm // 2, 2, D), lambda i: (l, i, 0, 0))
    out_f = jax.ShapeDtypeStruct((2, rows, D), F32)
    return pl.pallas_call(
        _mixer_sample_kernel,
        grid=(rows // tm,),
        in_specs=[tok, wspec((1, D)), wspec((D, D_IN)), wspec((3, D)), wspec((1, D)),
                  wspec((7, SUBLANES, D)), wspec((2, SUBLANES, D)), hist],
        out_specs=[pl.BlockSpec((3, 2, tm, D), lambda i: (0, 0, i, 0)), tok,
                   pl.BlockSpec((tm // 2, 2, D), lambda i: (i, 0, 0))],
        out_shape=[jax.ShapeDtypeStruct((3, 2, rows, D), F32), out_f,
                   jax.ShapeDtypeStruct((rows // 2, 2, D), F32)],
        scratch_shapes=[pltpu.VMEM((tm + 2 * SUBLANES, D), F32)],
        compiler_params=pltpu.CompilerParams(
            dimension_semantics=("arbitrary",), vmem_limit_bytes=VMEM_LIMIT),
        name="mixer_sample",
    )(xs, norm_mix_g, w_in, conv_a_w, sg_ln_g, sgc, sgb, cache_a)


def _ffn_sample_kernel(*refs, final):
    if final:
        (x_ref, mix_ref, wo_ref, g_ref, wup_ref, cw_ref, cb_ref, wdn_ref, hist_ref, gfin_ref,
         xo_ref, state_ref, y_ref, scr) = refs
    else:
        (x_ref, mix_ref, wo_ref, g_ref, wup_ref, cw_ref, cb_ref, wdn_ref, hist_ref,
         xo_ref, state_ref, scr) = refs
    rows = x_ref.shape[1]
    _zero_margins(scr, rows)
    x1 = x_ref[...].reshape(2 * rows, D) + _dot(
        mix_ref[...].reshape(2 * rows, D).astype(BF16), wo_ref[...])
    h = _dot(_rms(x1, g_ref[...]).astype(BF16), wup_ref[...])
    h0, h1 = h[:rows], h[rows:]
    hist = hist_ref[...].reshape(rows, D_FF2)
    conv = jnp.concatenate(_sample_conv(scr, cw_ref, h0, h1, hist), axis=0)
    state_ref[...] = h1.reshape(state_ref.shape)
    x2 = x1 + _dot(_silu_gate(conv + cb_ref[...]), wdn_ref[...])
    xo_ref[...] = x2.reshape(2, rows, D)
    if final:
        y_ref[...] = _rms(x2, gfin_ref[...]).reshape(2, rows, D)


def _ffn_sample(l, final, xs, mix, w_o, norm_ffn_g, w_up, conv_f_w, conv_f_b, w_down, cache_f,
                norm_final_g):
    rows = xs.shape[1]
    tm = rows // 2
    tok = pl.BlockSpec((2, tm, D), lambda i: (0, i, 0))
    wspec = lambda shape: _resident((None,) + shape, lambda i: (l,) + (0,) * len(shape))
    mat = lambda shape: _resident(shape, lambda i: (0, 0))
    x_shape = jax.ShapeDtypeStruct((2, rows, D), F32)
    in_specs = [tok, tok, mat((D, D)), wspec((1, D)), mat((D, D_FF2)), wspec((3, D_FF2)),
                wspec((1, D_FF2)), mat((D_FF, D)),
                pl.BlockSpec((None, tm // 2, 2, D_FF2), lambda i: (l, i, 0, 0))]
    args = [xs, mix, w_o, norm_ffn_g, w_up, conv_f_w, conv_f_b, w_down, cache_f]
    out_specs = [tok, pl.BlockSpec((tm // 2, 2, D_FF2), lambda i: (i, 0, 0))]
    out_shape = [x_shape, jax.ShapeDtypeStruct((rows // 2, 2, D_FF2), F32)]
    if final:
        in_specs.append(_resident((1, D), lambda i: (0, 0)))
        args.append(norm_final_g)
        out_specs.append(tok)
        out_shape.append(x_shape)
    return pl.pallas_call(
        functools.partial(_ffn_sample_kernel, final=final),
        grid=(rows // tm,),
        in_specs=in_specs,
        out_specs=out_specs,
        out_shape=out_shape,
        scratch_shapes=[pltpu.VMEM((tm + 2 * SUBLANES, D_FF2), F32)],
        compiler_params=pltpu.CompilerParams(
            dimension_semantics=("arbitrary",), vmem_limit_bytes=VMEM_LIMIT),
        name="ffn_sample",
    )(*args)


def _sample_gating_tables(sg_w, sg_b):
    w = sg_w[:, :, :SAMPLE_STEPS, :SAMPLE_STEPS]
    zero = jnp.zeros_like(w[..., 0, 0])
    by_parity = [(w[..., 0, 0], w[..., 1, 1]), (zero, w[..., 1, 0]), (w[..., 2, 0], w[..., 3, 1]),
                 (w[..., 2, 1], zero), (zero, w[..., 3, 0]), (w[..., 2, 2], w[..., 3, 3]),
                 (zero, w[..., 3, 2])]

    def table(pairs):
        t = jnp.stack([jnp.stack(p, axis=1) for p in pairs], axis=1)
        t = jnp.tile(t, (1, 1, SUBLANES // 2, 1))
        return jnp.repeat(t, SG_COLS, axis=3)

    b = sg_b[:, :, :SAMPLE_STEPS]
    return table(by_parity), table([(b[..., 0], b[..., 1]), (b[..., 2], b[..., 3])])


def kernel(x_prompt, x_sample, mem_prompt, cache_conv_a, cache_conv_ffn, cache_mem_k, cache_mem_v, norm_mix_g, w_in, conv_a_w, sg_ln_g, sg_w, sg_b, norm_mem_g, w_k, w_v, w_o, norm_ffn_g, w_up, conv_f_w, conv_f_b, w_down, norm_final_g):
    depth = w_in.shape[0]
    bp = x_prompt.shape[0]
    bs, ts, _ = x_sample.shape
    assert ts == SAMPLE_STEPS

    wkv_b = jnp.concatenate([w_k, w_v], axis=-1).astype(BF16)

    nmg = norm_mix_g[:, None, :]
    lng = sg_ln_g[:, None, :]
    nfg = norm_ffn_g[:, None, :]
    nmemg = norm_mem_g[:, None, :]
    cfb = conv_f_b[:, None, :]
    gfin = norm_final_g[None, :]
    sg_b_t = jnp.swapaxes(sg_b, 1, 2)
    sgc, sgbias = _sample_gating_tables(sg_w, sg_b)
    to_halves = lambda a: jnp.swapaxes(a.reshape(bs, 2, 2, D), 0, 1).reshape(2, bs * 2, D)
    from_halves = lambda a: jnp.swapaxes(a.reshape(2, bs, 2, D), 0, 1).reshape(bs, ts, D)
    xs = to_halves(x_sample)

    k_f, v_f, k_b, v_b, w_in_b = _mem_kv(mem_prompt.reshape(bp * N_MEM, D), nmemg, wkv_b, w_in)
    k_b = k_b.reshape(depth * bp, N_MEM, D)
    v_b = v_b.reshape(depth * bp, N_MEM, D)

    xp = x_prompt
    pa, pf, sa, sf, sv = [], [], [], [], []
    ys = None
    for l in range(depth):
        final = l == depth - 1
        qgm, vn, a_state_s = _mixer_sample(l, xs, nmg, w_in_b, conv_a_w, lng, sgc, sgbias,
                                           cache_conv_a)
        mix, a_state, mix_s, w_o_b, w_up_b, w_down_b = _mixer_prompt(
            l, xp, nmg, w_in_b, conv_a_w, lng, sg_w, sg_b_t, k_b, v_b, qgm,
            cache_mem_k, cache_mem_v, w_o, w_up, w_down)
        xp, f_state = _ffn_prompt(l, final, xp, mix, w_o_b, nfg, w_up_b, conv_f_w, cfb, w_down_b,
                                  gfin)
        pa.append(a_state)
        pf.append(f_state)
        outs = _ffn_sample(l, final, xs, mix_s, w_o_b, nfg, w_up_b, conv_f_w, cfb, w_down_b,
                           cache_conv_ffn, gfin)
        xs, f_state_s = outs[0], outs[1]
        if final:
            ys = outs[2]
        sa.append(a_state_s)
        sf.append(f_state_s)
        sv.append(from_halves(vn))

    kv_shape = (depth, bp, N_MEM, HEADS, HEAD_DIM)
    return (xp, from_halves(ys),
            jnp.stack(pa), jnp.stack(pf), k_f.reshape(kv_shape), v_f.reshape(kv_shape),
            jnp.stack(sa), jnp.stack(sf), jnp.stack(sv))
```

```python
import functools

import jax
import jax.numpy as jnp
from jax import lax
from jax.experimental import pallas as pl
from jax.experimental.pallas import tpu as pltpu

F32 = jnp.float32
BF16 = jnp.bfloat16

D = 1024
D_IN = 9 * D
D_FF = 2816
D_FF2 = 2 * D_FF
N_MEM = 256
HEADS = 4
HEAD_DIM = D // HEADS
SG_GROUPS = 4
SG_COLS = D // SG_GROUPS
SG_CHUNK = 128
EPS = 1e-6
H, C, B_, U, V, Q, GA, GB, GM = range(9)
PROJ_GROUPS = [(V, Q, U), (GB, C, H), (B_, GA, GM)]

V7X_VMEM_BYTES = 64 * 1024 * 1024
VMEM_LIMIT = V7X_VMEM_BYTES - 4 * 1024 * 1024
SUBLANES = 8

SUBS_MIX = (256, 256)
SUBS_FFN = (256, 256, 256, 256)
TM_KV = 512
ATT_BB = 4
MASKED = -1e30


def _rms(x, g):
    y = x * lax.rsqrt(jnp.mean(x * x, axis=-1, keepdims=True) + EPS)
    return y * g


def _layernorm(x, g):
    xc = x - jnp.mean(x, axis=-1, keepdims=True)
    return xc * lax.rsqrt(jnp.mean(xc * xc, axis=-1, keepdims=True) + EPS) * g


def _dot(a, b):
    return jnp.dot(a, b, preferred_element_type=F32)


def _resident(shape, index_map):
    return pl.BlockSpec(shape, index_map, pipeline_mode=pl.Buffered(1))


def _causal_conv(w_ref, hist, cur):
    n = hist.shape[0]
    ext = jnp.concatenate([hist, cur], axis=0)
    x1 = pltpu.roll(ext, 1, axis=0)[n:, :]
    x2 = pltpu.roll(ext, 2, axis=0)[n:, :]
    return w_ref[0:1, :] * x2 + w_ref[1:2, :] * x1 + w_ref[2:3, :] * cur


def _silu_gate(hc):
    return (jax.nn.silu(hc[:, D_FF:]) * hc[:, :D_FF]).astype(BF16)


def _kv_kernel(mem_ref, g_ref, wkv_ref, win_ref, k_ref, v_ref, kb_ref, vb_ref, winb_ref):
    winb_ref[...] = win_ref[...].astype(BF16)
    m = _rms(mem_ref[...], g_ref[...]).astype(BF16)
    k = _dot(m, wkv_ref[:, :D])
    v = _dot(m, wkv_ref[:, D:])
    for h in range(HEADS):
        hc = slice(h * HEAD_DIM, (h + 1) * HEAD_DIM)
        k_ref[:, h, :] = k[:, hc]
        v_ref[:, h, :] = v[:, hc]
    kb_ref[...] = k.astype(BF16)
    vb_ref[...] = v.astype(BF16)


def _mem_kv(mem2d, norm_mem_g, wkv, w_in):
    depth = wkv.shape[0]
    rows = mem2d.shape[0]
    steps = rows // TM_KV
    win_blk = pl.BlockSpec((None, D // steps, D_IN), lambda l, i: (l, i, 0))
    out_f = jax.ShapeDtypeStruct((depth, rows, HEADS, HEAD_DIM), F32)
    out_b = jax.ShapeDtypeStruct((depth, rows, D), BF16)
    oblk = pl.BlockSpec((None, TM_KV, D), lambda l, i: (l, i, 0))
    oblk_f = pl.BlockSpec((None, TM_KV, HEADS, HEAD_DIM), lambda l, i: (l, i, 0, 0))
    return pl.pallas_call(
        _kv_kernel,
        grid=(depth, rows // TM_KV),
        in_specs=[
            pl.BlockSpec((TM_KV, D), lambda l, i: (i, 0)),
            pl.BlockSpec((None, 1, D), lambda l, i: (l, 0, 0)),
            pl.BlockSpec((None, D, 2 * D), lambda l, i: (l, 0, 0)),
            win_blk,
        ],
        out_specs=[oblk_f, oblk_f, oblk, oblk, win_blk],
        out_shape=[out_f, out_f, out_b, out_b, jax.ShapeDtypeStruct(w_in.shape, BF16)],
        compiler_params=pltpu.CompilerParams(
            dimension_semantics=("arbitrary", "arbitrary"), vmem_limit_bytes=VMEM_LIMIT),
        name="mem_kv",
    )(mem2d, norm_mem_g, wkv, w_in)


def _attn_steps(qgm_ref, k_ref, v_ref, mix_ref):
    hb = qgm_ref.shape[2]
    nb = 2 * hb
    nq = HEADS * nb
    nk = N_MEM * HEADS
    tile = HEAD_DIM
    row = lax.broadcasted_iota(jnp.int32, (nq, nk), 0)
    col = lax.broadcasted_iota(jnp.int32, (nq, nk), 1)
    same_head = (row >> (nb.bit_length() - 1)) == (col & (HEADS - 1))
    row_b = (row & (hb - 1)) >> 1
    out_b = (lax.broadcasted_iota(jnp.int32, (nq, 1), 0) & (hb - 1)) >> 1
    q = qgm_ref[0].reshape(nb, D)
    q_all = jnp.concatenate([q[:, h * HEAD_DIM:(h + 1) * HEAD_DIM] for h in range(HEADS)],
                            axis=0).astype(BF16)
    st = {"y": jnp.zeros((nq, HEAD_DIM), F32)}

    def score_tile(b, j):
        if j == 0:
            st["k"] = k_ref[b].reshape(nk, HEAD_DIM).astype(BF16)
            st["sc"] = []
        st["sc"].append(lax.dot_general(q_all, st["k"][j * tile:(j + 1) * tile],
                                        (((1,), (1,)), ((), ())), preferred_element_type=F32))
        if j == nk // tile - 1:
            sc = jnp.concatenate(st["sc"], axis=1) * (HEAD_DIM ** -0.5)
            sc = jnp.where(same_head & (row_b == b), sc, MASKED)
            e = jnp.exp(sc - jnp.max(sc, axis=-1, keepdims=True))
            st["p", b] = (e * (1.0 / jnp.sum(e, axis=-1, keepdims=True))).astype(BF16)

    def value_tile(b, j):
        if j == 0:
            st["v"] = v_ref[b].reshape(nk, HEAD_DIM).astype(BF16)
            st["o"] = jnp.zeros((nq, HEAD_DIM), F32)
        st["o"] = st["o"] + _dot(st["p", b][:, j * tile:(j + 1) * tile],
                                 st["v"][j * tile:(j + 1) * tile])
        if j == nk // tile - 1:
            st["y"] = jnp.where(out_b == b, st["o"], st["y"])
            if b == ATT_BB - 1:
                y = jnp.concatenate([st["y"][h * nb:(h + 1) * nb] for h in range(HEADS)], axis=1)
                mix_ref[...] = qgm_ref[2] + qgm_ref[1] * y.reshape(2, hb, D)

    tiles = [(b, j) for b in range(ATT_BB) for j in range(nk // tile)]
    return ([functools.partial(score_tile, b, j) for b, j in tiles]
            + [functools.partial(value_tile, b, j) for b, j in tiles])


def _mixer_kernel(x_ref, nm_ref, win_ref, caw_ref, lng_ref, sgw_ref, sgb_ref, kb_ref, vb_ref,
                  sqgm_ref, sk_ref, sv_ref, wo_ref, wup_ref, wdn_ref,
                  mix_ref, state_ref, smix_ref, wob_ref, wupb_ref, wdnb_ref, a_scr,
                  *, subs, prev_rows):
    wob_ref[...] = wo_ref[...].astype(BF16)
    wupb_ref[...] = wup_ref[...].astype(BF16)
    wdnb_ref[...] = wdn_ref[...].astype(BF16)

    @pl.when(pl.program_id(1) == 0)
    def _():
        a_scr[...] = jnp.zeros((prev_rows, D), F32)

    ri = lax.broadcasted_iota(jnp.int32, (SG_CHUNK, SG_CHUNK), 0)
    ci = lax.broadcasted_iota(jnp.int32, (SG_CHUNK, SG_CHUNK), 1)
    sgw = [jnp.where(ci <= ri, sgw_ref[g], 0.0).astype(BF16) for g in range(SG_GROUPS)]
    offs = [sum(subs[:t]) for t in range(len(subs))]
    hist = [a_scr[...]]

    sample_steps = _attn_steps(sqgm_ref, sk_ref, sv_ref, smix_ref)
    chunk = 2 * SG_CHUNK
    for _ in range(5):
        sample_steps.pop(0)()
    for t in range(len(subs)):
        rs = slice(offs[t], offs[t] + subs[t])
        z = _rms(x_ref[rs, :], nm_ref[...]).astype(BF16)

        def proj(i, ride_along=False):
            cols = []
            for j in range(i * D, (i + 1) * D, chunk):
                cols.append(_dot(z, win_ref[:, j:j + chunk]))
                if ride_along and sample_steps:
                    sample_steps.pop(0)()
            return jnp.concatenate(cols, axis=1)

        v, q, u = proj(V), proj(Q), proj(U)
        vnb = _layernorm(v, lng_ref[...]).astype(BF16)
        s_rows = []
        for c in range(subs[t] // SG_CHUNK):
            s_cols = []
            for g in range(SG_GROUPS):
                vg = vnb[c * SG_CHUNK:(c + 1) * SG_CHUNK, g * SG_COLS:(g + 1) * SG_COLS]
                s_cols.append(_dot(sgw[g], vg) + sgb_ref[:, g:g + 1])
            s_rows.append(jnp.concatenate(s_cols, axis=1))
        s = jnp.concatenate(s_rows, axis=0)
        scores = [
            lax.dot_general(q[:, h * HEAD_DIM:(h + 1) * HEAD_DIM].astype(BF16),
                            kb_ref[:, h * HEAD_DIM:(h + 1) * HEAD_DIM],
                            (((1,), (1,)), ((), ())), preferred_element_type=F32)
            for h in range(HEADS)]

        gb, c_, h_ = proj(GB, True), proj(C, True), proj(H, True)
        ubs = jax.nn.sigmoid(gb) * u * s
        a = c_ * h_
        conv = _causal_conv(caw_ref, hist[0], a)
        hist[0] = a[subs[t] - prev_rows:, :]
        o_cols = []
        for h in range(HEADS):
            sc = scores[h] * (HEAD_DIM ** -0.5)
            e = jnp.exp(sc - jnp.max(sc, axis=-1, keepdims=True))
            p = e * (1.0 / jnp.sum(e, axis=-1, keepdims=True))
            o_cols.append(_dot(p.astype(BF16), vb_ref[:, h * HEAD_DIM:(h + 1) * HEAD_DIM]))
        y_m = jnp.concatenate(o_cols, axis=1)

        b_, ga, gm = proj(B_, True), proj(GA, True), proj(GM)
        acc = jax.nn.sigmoid(ga) * (b_ * conv) + ubs + jax.nn.sigmoid(gm) * y_m
        mix_ref[rs, :] = acc.astype(BF16)
    for step in sample_steps:
        step()

    a_scr[...] = hist[0]

    @pl.when(pl.program_id(1) == pl.num_programs(1) - 1)
    def _():
        state_ref[...] = hist[0][prev_rows - 2:, :]


def _mixer_prompt(l, x, norm_mix_g, w_in, conv_a_w, sg_ln_g, sg_w, sg_b_t, kb, vb,
                  sqgm, cache_k, cache_v, w_o, w_up, w_down):
    bsz, t, _ = x.shape
    tm = sum(SUBS_MIX)
    assert len(SUBS_MIX) >= 2
    n_t = t // tm
    prev_rows = SUBLANES
    hb = 2 * ATT_BB
    assert hb & (hb - 1) == 0 and sqgm.shape[2] == hb * bsz * n_t
    wspec = lambda shape: _resident((None,) + shape, lambda b, i: (l,) + (0,) * len(shape))
    small = pl.BlockSpec((2, hb, D), lambda b, i: (0, b * n_t + i, 0))
    small3 = pl.BlockSpec((3, 2, hb, D), lambda b, i: (0, 0, b * n_t + i, 0))
    big = pl.BlockSpec((None, ATT_BB, N_MEM, HEADS, HEAD_DIM),
                       lambda b, i: (l, b * n_t + i, 0, 0, 0))
    steps = bsz * n_t
    bf16_rows = 2 * SUBLANES

    def cast_blocks(w):
        rows = w.shape[1]
        per = steps
        while rows % per or (rows // per) % bf16_rows:
            per //= 2
        blk = (rows // per,) + w.shape[2:]
        zeros = (0,) * (w.ndim - 2)
        src = pl.BlockSpec((None,) + blk, lambda b, i: (l, (b * n_t + i) * per // steps) + zeros)
        dst = pl.BlockSpec(blk, lambda b, i: ((b * n_t + i) * per // steps,) + zeros)
        return src, dst, jax.ShapeDtypeStruct(w.shape[1:], BF16)

    casts = [cast_blocks(w) for w in (w_o, w_up, w_down)]
    return pl.pallas_call(
        functools.partial(_mixer_kernel, subs=SUBS_MIX, prev_rows=prev_rows),
        grid=(bsz, t // tm),
        in_specs=[
            pl.BlockSpec((None, tm, D), lambda b, i: (b, i, 0)),
            wspec((1, D)),
            wspec((D, D_IN)),
            wspec((3, D)),
            wspec((1, D)),
            wspec((SG_GROUPS, SG_CHUNK, SG_CHUNK)),
            wspec((SG_CHUNK, SG_GROUPS)),
            pl.BlockSpec((None, N_MEM, D), lambda b, i: (l * bsz + b, 0, 0)),
            pl.BlockSpec((None, N_MEM, D), lambda b, i: (l * bsz + b, 0, 0)),
            small3, big, big,
        ] + [c[0] for c in casts],
        out_specs=[
            pl.BlockSpec((None, tm, D), lambda b, i: (b, i, 0)),
            pl.BlockSpec((None, 2, D), lambda b, i: (b, 0, 0)),
            small,
        ] + [c[1] for c in casts],
        out_shape=[
            jax.ShapeDtypeStruct((bsz, t, D), BF16),
            jax.ShapeDtypeStruct((bsz, 2, D), F32),
            jax.ShapeDtypeStruct(sqgm.shape[1:], F32),
        ] + [c[2] for c in casts],
        scratch_shapes=[pltpu.VMEM((prev_rows, D), F32)],
        compiler_params=pltpu.CompilerParams(
            dimension_semantics=("arbitrary", "arbitrary"), vmem_limit_bytes=VMEM_LIMIT),
        name="mixer_prompt",
    )(x, norm_mix_g, w_in, conv_a_w, sg_ln_g, sg_w, sg_b_t, kb, vb, sqgm, cache_k, cache_v,
      w_o, w_up, w_down)


def _ffn_kernel(*refs, final, subs, prev_rows):
    if final:
        (x_ref, mix_ref, wo_ref, g_ref, wup_ref, cw_ref, cb_ref, wdn_ref, gfin_ref,
         out_ref, state_ref, h_scr) = refs
    else:
        (x_ref, mix_ref, wo_ref, g_ref, wup_ref, cw_ref, cb_ref, wdn_ref,
         out_ref, state_ref, h_scr) = refs

    @pl.when(pl.program_id(1) == 0)
    def _():
        h_scr[...] = jnp.zeros((prev_rows, D_FF2), F32)

    offs = [sum(subs[:t]) for t in range(len(subs))]
    slices = [slice(o, o + n) for o, n in zip(offs, subs)]
    x1 = [x_ref[rs, :] + _dot(mix_ref[rs, :], wo_ref[...]) for rs in slices]
    h = [_dot(_rms(x1[t], g_ref[...]).astype(BF16), wup_ref[...]) for t in range(len(subs))]
    hist = h_scr[...]
    for t, rs in enumerate(slices):
        hc = _causal_conv(cw_ref, hist, h[t]) + cb_ref[...]
        hist = h[t][subs[t] - prev_rows:, :]
        x2 = x1[t] + _dot(_silu_gate(hc), wdn_ref[...])
        out_ref[rs, :] = _rms(x2, gfin_ref[...]) if final else x2
    h_scr[...] = hist

    @pl.when(pl.program_id(1) == pl.num_programs(1) - 1)
    def _():
        state_ref[...] = hist[prev_rows - 2:, :]


def _ffn_prompt(l, final, x, mix, w_o, norm_ffn_g, w_up, conv_f_w, conv_f_b, w_down, norm_final_g):
    bsz, t, _ = x.shape
    tm, prev_rows = sum(SUBS_FFN), SUBLANES
    tok = pl.BlockSpec((None, tm, D), lambda b, i: (b, i, 0))
    wspec = lambda shape: _resident((None,) + shape, lambda b, i: (l,) + (0,) * len(shape))
    mat = lambda shape: _resident(shape, lambda b, i: (0, 0))
    x_shape = jax.ShapeDtypeStruct((bsz, t, D), F32)
    in_specs = [tok, tok, mat((D, D)), wspec((1, D)), mat((D, D_FF2)), wspec((3, D_FF2)),
                wspec((1, D_FF2)), mat((D_FF, D))]
    args = [x, mix, w_o, norm_ffn_g, w_up, conv_f_w, conv_f_b, w_down]
    out_specs = [tok, pl.BlockSpec((None, 2, D_FF2), lambda b, i: (b, 0, 0))]
    out_shape = [x_shape, jax.ShapeDtypeStruct((bsz, 2, D_FF2), F32)]
    if final:
        in_specs.append(_resident((1, D), lambda b, i: (0, 0)))
        args.append(norm_final_g)
    return pl.pallas_call(
        functools.partial(_ffn_kernel, final=final, subs=SUBS_FFN, prev_rows=prev_rows),
        grid=(bsz, t // tm),
        in_specs=in_specs,
        out_specs=out_specs,
        out_shape=out_shape,
        scratch_shapes=[pltpu.VMEM((prev_rows, D_FF2), F32)],
        compiler_params=pltpu.CompilerParams(
            dimension_semantics=("arbitrary", "arbitrary"), vmem_limit_bytes=VMEM_LIMIT),
        name="ffn_prompt",
    )(*args)


SAMPLE_STEPS = 4


def _put(scr, x):
    scr[SUBLANES:SUBLANES + x.shape[0], :] = x


def _get(scr, rows, k):
    return scr[SUBLANES - k:SUBLANES - k + rows, :]


def _zero_margins(scr, rows):
    scr[0:SUBLANES, :] = jnp.zeros((SUBLANES, scr.shape[1]), F32)
    scr[SUBLANES + rows:, :] = jnp.zeros((SUBLANES, scr.shape[1]), F32)


def _sample_conv(scr, w_ref, cur0, cur1, hist):
    rows = cur0.shape[0]
    first = (lax.broadcasted_iota(jnp.int32, (rows, 1), 0) & 1) == 0
    _put(scr, hist)
    hist_up = _get(scr, rows, -1)
    _put(scr, cur0)
    cur0_dn = _get(scr, rows, 1)
    cur0_up = _get(scr, rows, -1)
    _put(scr, cur1)
    cur1_dn = _get(scr, rows, 1)
    w0, w1, w2 = w_ref[0:1, :], w_ref[1:2, :], w_ref[2:3, :]
    c0 = w0 * hist + w1 * jnp.where(first, hist_up, cur0_dn) + w2 * cur0
    c1 = w0 * cur0 + w1 * jnp.where(first, cur0_up, cur1_dn) + w2 * cur1
    return c0, c1


def _mixer_sample_kernel(x_ref, nm_ref, win_ref, caw_ref, lng_ref, sgc_ref, sgb_ref, hist_ref,
                         qgm_ref, vn_ref, state_ref, scr):
    rows = x_ref.shape[1]
    halves = lambda a: (a[:rows], a[rows:])
    _zero_margins(scr, rows)
    z = _rms(x_ref[...].reshape(2 * rows, D), nm_ref[...]).astype(BF16)
    p = {}
    for group in PROJ_GROUPS:
        for i in group:
            p[i] = _dot(z, win_ref[:, i * D:(i + 1) * D])

    vn0, vn1 = halves(_layernorm(p[V], lng_ref[...]))
    vn_ref[0] = vn0
    vn_ref[1] = vn1
    tiles = lambda a: a.reshape(rows // SUBLANES, SUBLANES, D)
    coef = lambda k: sgc_ref[k][None]
    _put(scr, vn0)
    vn0_dn, vn0_up = tiles(_get(scr, rows, 1)), tiles(_get(scr, rows, -1))
    _put(scr, vn1)
    vn1_dn = tiles(_get(scr, rows, 1))
    s0 = coef(0) * tiles(vn0) + coef(1) * vn0_dn + sgb_ref[0][None]
    s1 = (coef(2) * tiles(vn0) + coef(3) * vn0_up + coef(4) * vn0_dn
          + coef(5) * tiles(vn1) + coef(6) * vn1_dn + sgb_ref[1][None])
    s = jnp.concatenate([s0.reshape(rows, D), s1.reshape(rows, D)], axis=0)

    a0, a1 = halves(p[C] * p[H])
    hist = hist_ref[...].reshape(rows, D)
    conv = jnp.concatenate(_sample_conv(scr, caw_ref, a0, a1, hist), axis=0)
    state_ref[...] = a1.reshape(state_ref.shape)
    acc = jax.nn.sigmoid(p[GA]) * (p[B_] * conv)
    acc = acc + jax.nn.sigmoid(p[GB]) * (p[U] * s)
    qgm_ref[0] = p[Q].reshape(2, rows, D)
    qgm_ref[1] = jax.nn.sigmoid(p[GM]).reshape(2, rows, D)
    qgm_ref[2] = acc.reshape(2, rows, D)


def _mixer_sample(l, xs, norm_mix_g, w_in, conv_a_w, sg_ln_g, sgc, sgb, cache_a):
    rows = xs.shape[1]
    tm = rows // 2
    wspec = lambda shape: _resident((None,) + shape, lambda i: (l,) + (0,) * len(shape))
    tok = pl.BlockSpec((2, tm, D), lambda i: (0, i, 0))
    hist = pl.BlockSpec((None, tm // 2, 2, D), lambda i: (l, i, 0, 0))
    out_f = jax.ShapeDtypeStruct((2, rows, D), F32)
    return pl.pallas_call(
        _mixer_sample_kernel,
        grid=(rows // tm,),
        in_specs=[tok, wspec((1, D)), wspec((D, D_IN)), wspec((3, D)), wspec((1, D)),
                  wspec((7, SUBLANES, D)), wspec((2, SUBLANES, D)), hist],
        out_specs=[pl.BlockSpec((3, 2, tm, D), lambda i: (0, 0, i, 0)), tok,
                   pl.BlockSpec((tm // 2, 2, D), lambda i: (i, 0, 0))],
        out_shape=[jax.ShapeDtypeStruct((3, 2, rows, D), F32), out_f,
                   jax.ShapeDtypeStruct((rows // 2, 2, D), F32)],
        scratch_shapes=[pltpu.VMEM((tm + 2 * SUBLANES, D), F32)],
        compiler_params=pltpu.CompilerParams(
            dimension_semantics=("arbitrary",), vmem_limit_bytes=VMEM_LIMIT),
        name="mixer_sample",
    )(xs, norm_mix_g, w_in, conv_a_w, sg_ln_g, sgc, sgb, cache_a)


def _ffn_sample_kernel(*refs, final):
    if final:
        (x_ref, mix_ref, wo_ref, g_ref, wup_ref, cw_ref, cb_ref, wdn_ref, hist_ref, gfin_ref,
         xo_ref, state_ref, y_ref, scr) = refs
    else:
        (x_ref, mix_ref, wo_ref, g_ref, wup_ref, cw_ref, cb_ref, wdn_ref, hist_ref,
         xo_ref, state_ref, scr) = refs
    rows = x_ref.shape[1]
    _zero_margins(scr, rows)
    x1 = x_ref[...].reshape(2 * rows, D) + _dot(
        mix_ref[...].reshape(2 * rows, D).astype(BF16), wo_ref[...])
    h = _dot(_rms(x1, g_ref[...]).astype(BF16), wup_ref[...])
    h0, h1 = h[:rows], h[rows:]
    hist = hist_ref[...].reshape(rows, D_FF2)
    conv = jnp.concatenate(_sample_conv(scr, cw_ref, h0, h1, hist), axis=0)
    state_ref[...] = h1.reshape(state_ref.shape)
    x2 = x1 + _dot(_silu_gate(conv + cb_ref[...]), wdn_ref[...])
    xo_ref[...] = x2.reshape(2, rows, D)
    if final:
        y_ref[...] = _rms(x2, gfin_ref[...]).reshape(2, rows, D)


def _ffn_sample(l, final, xs, mix, w_o, norm_ffn_g, w_up, conv_f_w, conv_f_b, w_down, cache_f,
                norm_final_g):
    rows = xs.shape[1]
    tm = rows // 2
    tok = pl.BlockSpec((2, tm, D), lambda i: (0, i, 0))
    wspec = lambda shape: _resident((None,) + shape, lambda i: (l,) + (0,) * len(shape))
    mat = lambda shape: _resident(shape, lambda i: (0, 0))
    x_shape = jax.ShapeDtypeStruct((2, rows, D), F32)
    in_specs = [tok, tok, mat((D, D)), wspec((1, D)), mat((D, D_FF2)), wspec((3, D_FF2)),
                wspec((1, D_FF2)), mat((D_FF, D)),
                pl.BlockSpec((None, tm // 2, 2, D_FF2), lambda i: (l, i, 0, 0))]
    args = [xs, mix, w_o, norm_ffn_g, w_up, conv_f_w, conv_f_b, w_down, cache_f]
    out_specs = [tok, pl.BlockSpec((tm // 2, 2, D_FF2), lambda i: (i, 0, 0))]
    out_shape = [x_shape, jax.ShapeDtypeStruct((rows // 2, 2, D_FF2), F32)]
    if final:
        in_specs.append(_resident((1, D), lambda i: (0, 0)))
        args.append(norm_final_g)
        out_specs.append(tok)
        out_shape.append(x_shape)
    return pl.pallas_call(
        functools.partial(_ffn_sample_kernel, final=final),
        grid=(rows // tm,),
        in_specs=in_specs,
        out_specs=out_specs,
        out_shape=out_shape,
        scratch_shapes=[pltpu.VMEM((tm + 2 * SUBLANES, D_FF2), F32)],
        compiler_params=pltpu.CompilerParams(
            dimension_semantics=("arbitrary",), vmem_limit_bytes=VMEM_LIMIT),
        name="ffn_sample",
    )(*args)


def _sample_gating_tables(sg_w, sg_b):
    w = sg_w[:, :, :SAMPLE_STEPS, :SAMPLE_STEPS]
    zero = jnp.zeros_like(w[..., 0, 0])
    by_parity = [(w[..., 0, 0], w[..., 1, 1]), (zero, w[..., 1, 0]), (w[..., 2, 0], w[..., 3, 1]),
                 (w[..., 2, 1], zero), (zero, w[..., 3, 0]), (w[..., 2, 2], w[..., 3, 3]),
                 (zero, w[..., 3, 2])]

    def table(pairs):
        t = jnp.stack([jnp.stack(p, axis=1) for p in pairs], axis=1)
        t = jnp.tile(t, (1, 1, SUBLANES // 2, 1))
        return jnp.repeat(t, SG_COLS, axis=3)

    b = sg_b[:, :, :SAMPLE_STEPS]
    return table(by_parity), table([(b[..., 0], b[..., 1]), (b[..., 2], b[..., 3])])


def kernel(x_prompt, x_sample, mem_prompt, cache_conv_a, cache_conv_ffn, cache_mem_k, cache_mem_v, norm_mix_g, w_in, conv_a_w, sg_ln_g, sg_w, sg_b, norm_mem_g, w_k, w_v, w_o, norm_ffn_g, w_up, conv_f_w, conv_f_b, w_down, norm_final_g):
    depth = w_in.shape[0]
    bp = x_prompt.shape[0]
    bs, ts, _ = x_sample.shape
    assert ts == SAMPLE_STEPS

    wkv_b = jnp.concatenate([w_k, w_v], axis=-1).astype(BF16)

    nmg = norm_mix_g[:, None, :]
    lng = sg_ln_g[:, None, :]
    nfg = norm_ffn_g[:, None, :]
    nmemg = norm_mem_g[:, None, :]
    cfb = conv_f_b[:, None, :]
    gfin = norm_final_g[None, :]
    sg_b_t = jnp.swapaxes(sg_b, 1, 2)
    sgc, sgbias = _sample_gating_tables(sg_w, sg_b)
    to_halves = lambda a: jnp.swapaxes(a.reshape(bs, 2, 2, D), 0, 1).reshape(2, bs * 2, D)
    from_halves = lambda a: jnp.swapaxes(a.reshape(2, bs, 2, D), 0, 1).reshape(bs, ts, D)
    xs = to_halves(x_sample)

    k_f, v_f, k_b, v_b, w_in_b = _mem_kv(mem_prompt.reshape(bp * N_MEM, D), nmemg, wkv_b, w_in)
    k_b = k_b.reshape(depth * bp, N_MEM, D)
    v_b = v_b.reshape(depth * bp, N_MEM, D)

    xp = x_prompt
    pa, pf, sa, sf, sv = [], [], [], [], []
    ys = None
    for l in range(depth):
        final = l == depth - 1
        qgm, vn, a_state_s = _mixer_sample(l, xs, nmg, w_in_b, conv_a_w, lng, sgc, sgbias,
                                           cache_conv_a)
        mix, a_state, mix_s, w_o_b, w_up_b, w_down_b = _mixer_prompt(
            l, xp, nmg, w_in_b, conv_a_w, lng, sg_w, sg_b_t, k_b, v_b, qgm,
            cache_mem_k, cache_mem_v, w_o, w_up, w_down)
        xp, f_state = _ffn_prompt(l, final, xp, mix, w_o_b, nfg, w_up_b, conv_f_w, cfb, w_down_b,
                                  gfin)
        pa.append(a_state)
        pf.append(f_state)
        outs = _ffn_sample(l, final, xs, mix_s, w_o_b, nfg, w_up_b, conv_f_w, cfb, w_down_b,
                           cache_conv_ffn, gfin)
        xs, f_state_s = outs[0], outs[1]
        if final:
            ys = outs[2]
        sa.append(a_state_s)
        sf.append(f_state_s)
        sv.append(from_halves(vn))

    kv_shape = (depth, bp, N_MEM, HEADS, HEAD_DIM)
    return (xp, from_halves(ys),
            jnp.stack(pa), jnp.stack(pf), k_f.reshape(kv_shape), v_f.reshape(kv_shape),
            jnp.stack(sa), jnp.stack(sf), jnp.stack(sv))
```
